```python
import math
import jax, jax.numpy as jnp
from jax import lax
import numpy as np

D_MODEL = 1024
BATCH = 8
SEQ = 8192
DEPTH = 2
DEC_BATCH = 16
DEC_SEQ = 4096
PAST_LEN = 128

GRID_W = 64
SSD_EXPAND = 2
D_INNER = SSD_EXPAND * D_MODEL
SSD_HEAD_DIM = 64
SSD_HEADS = D_INNER // SSD_HEAD_DIM
SSD_GROUPS = 4
D_STATE = 128
D_CONV = 5
CHUNK = 128
CONV_DIM = D_INNER + 2 * SSD_GROUPS * D_STATE
FNET_GROUPS = 4
FNET_GROUP_DIM = 256
FNET_WIDTH = FNET_GROUPS * FNET_GROUP_DIM
IN_PROJ_EVEN = D_INNER + CONV_DIM + 2 * SSD_HEADS + FNET_WIDTH
MIX_EVEN = D_INNER + FNET_WIDTH
NA_HEADS = 16
NA_HEAD_DIM = D_MODEL // NA_HEADS
NA_WIN_H = 8
NA_WIN_W = 16
D_FF_DENSE = 2816
N_EXPERTS = 8
TOP_K = 2
D_FF_EXPERT = 3584
LN_EPS = 1e-5
RMS_EPS = 1e-5
DEEPNORM_ALPHA = (2 * DEPTH) ** 0.25
DEEPNORM_BETA = (8 * DEPTH) ** -0.25
N_EVEN = (DEPTH + 1) // 2
N_ODD = DEPTH // 2

kernel_name = "hybrid_ssd_fnet_natten_deepnorm_encoder"


def layer_norm(x, g, b):
    xf = x.astype(jnp.float32)
    mu = jnp.mean(xf, axis=-1, keepdims=True)
    var = jnp.mean(jnp.square(xf - mu), axis=-1, keepdims=True)
    return ((xf - mu) * lax.rsqrt(var + LN_EPS) * g + b).astype(x.dtype)


def centred_dwconv(u, w, b):
    out = lax.conv_general_dilated(
        u, w[:, None, :].astype(u.dtype), window_strides=(1,),
        padding=[(D_CONV // 2, D_CONV // 2)],
        dimension_numbers=("NWC", "WIO", "NWC"),
        feature_group_count=u.shape[-1])
    return out + b


def ssd_chunked(x, dt, A, Bm, Cm):
    f32 = jnp.float32
    b, t, h, p = x.shape
    g, n = Bm.shape[2], Bm.shape[3]
    r = h // g
    nc = t // CHUNK
    dtf = dt.astype(f32)
    xdt = (x.astype(f32) * dtf[..., None]).reshape(b, nc, CHUNK, g, r, p)
    a_cs = jnp.cumsum((dtf * A.astype(f32)).reshape(b, nc, CHUNK, g, r), axis=2)
    Bc = Bm.astype(f32).reshape(b, nc, CHUNK, g, n)
    Cc = Cm.astype(f32).reshape(b, nc, CHUNK, g, n)
    mask = jnp.tril(jnp.ones((CHUNK, CHUNK), dtype=bool))[None, None, :, :, None, None]
    seg = a_cs[:, :, :, None] - a_cs[:, :, None]
    cb = jnp.einsum("bclgn,bcsgn->bclsg", Cc, Bc)
    w = jnp.exp(jnp.where(mask, seg, -jnp.inf)) * cb[..., None]
    y_diag = jnp.einsum("bclsgr,bcsgrp->bclgrp", w, xdt)
    to_end = jnp.exp(a_cs[:, :, -1:] - a_cs)
    chunk_states = jnp.einsum("bclgn,bclgrp->bcgrpn", Bc, xdt * to_end[..., None])
    chunk_decay = jnp.exp(a_cs[:, :, -1])

    def step(carry, inp):
        s_c, d_c = inp
        return carry * d_c[..., None, None] + s_c, carry

    init = jnp.zeros((b, g, r, p, n), f32)
    _, prev = lax.scan(step, init, (jnp.moveaxis(chunk_states, 1, 0), jnp.moveaxis(chunk_decay, 1, 0)))
    prev = jnp.moveaxis(prev, 0, 1)
    y_off = jnp.einsum("bclgn,bcgrpn->bclgrp", Cc, prev) * jnp.exp(a_cs)[..., None]
    return (y_diag + y_off).reshape(b, t, h, p)


def ssd_fnet_mixer(x, w_in, conv_w, conv_b, dt_bias_f, dt_bias_b, a_log_f, a_log_b, d_skip, gnorm_w, w_out):
    f32 = jnp.float32
    b, t, _ = x.shape
    proj = x @ w_in
    z, xbc, dt_raw, u = jnp.split(proj, [D_INNER, D_INNER + CONV_DIM, D_INNER + CONV_DIM + 2 * SSD_HEADS], axis=-1)
    xbc = jax.nn.silu(centred_dwconv(xbc, conv_w, conv_b))
    xs, Bm, Cm = jnp.split(xbc, [D_INNER, D_INNER + SSD_GROUPS * D_STATE], axis=-1)
    xs = xs.reshape(b, t, SSD_HEADS, SSD_HEAD_DIM)
    Bm = Bm.reshape(b, t, SSD_GROUPS, D_STATE)
    Cm = Cm.reshape(b, t, SSD_GROUPS, D_STATE)
    dt_f = jax.nn.softplus((dt_raw[..., :SSD_HEADS] + dt_bias_f).astype(f32))
    dt_b = jax.nn.softplus((dt_raw[..., SSD_HEADS:] + dt_bias_b).astype(f32))
    flip = lambda v: jnp.flip(v, axis=1)
    y_f = ssd_chunked(xs, dt_f, -jnp.exp(a_log_f.astype(f32)), Bm, Cm)
    y_b = flip(ssd_chunked(flip(xs), flip(dt_b), -jnp.exp(a_log_b.astype(f32)), flip(Bm), flip(Cm)))
    y = y_f + y_b + d_skip.astype(f32)[:, None] * xs.astype(f32)
    y = y.reshape(b, t, D_INNER) * jax.nn.silu(z.astype(f32))
    yg = y.reshape(b, t, SSD_GROUPS, D_INNER // SSD_GROUPS)
    yg = yg * lax.rsqrt(jnp.mean(jnp.square(yg), axis=-1, keepdims=True) + RMS_EPS)
    y_ssd = (yg.reshape(b, t, D_INNER) * gnorm_w).astype(x.dtype)
    ug = u.astype(f32).reshape(b, t, FNET_GROUPS, FNET_GROUP_DIM)
    y_fn = jnp.fft.fft2(ug, axes=(1, 3), norm="ortho").real.reshape(b, t, FNET_WIDTH).astype(x.dtype)
    return jnp.concatenate([y_ssd, y_fn], axis=-1) @ w_out


def neighbourhood_attention(x, w_qkv, rpb, w_out):
    b, t, _ = x.shape
    rows = t // GRID_W
    kh = min(NA_WIN_H, rows)
    kw = NA_WIN_W
    qkv = (x @ w_qkv).reshape(b, rows, GRID_W, 3, NA_HEADS, NA_HEAD_DIM)
    q = qkv[:, :, :, 0] * (NA_HEAD_DIM ** -0.5)
    k = qkv[:, :, :, 1]
    v = qkv[:, :, :, 2]
    col = jnp.arange(GRID_W)
    col_start = jnp.clip(col - kw // 2, 0, GRID_W - kw)
    col_idx = col_start[:, None] + jnp.arange(kw)
    col_off = col_idx - col[:, None] + (NA_WIN_W - 1)

    def one_row(i):
        r0 = jnp.clip(i - kh // 2, 0, rows - kh)
        q_i = lax.dynamic_index_in_dim(q, i, axis=1, keepdims=False)
        k_nb = lax.dynamic_slice_in_dim(k, r0, kh, axis=1)[:, :, col_idx]
        v_nb = lax.dynamic_slice_in_dim(v, r0, kh, axis=1)[:, :, col_idx]
        s = jnp.einsum("bjhd,bajchd->bhjac", q_i, k_nb).astype(jnp.float32)
        row_off = r0 + jnp.arange(kh) - i + (NA_WIN_H - 1)
        bias = rpb[:, row_off[None, :, None], col_off[:, None, :]]
        pr = jax.nn.softmax(s + bias[None].astype(jnp.float32), axis=(-2, -1))
        return jnp.einsum("bhjac,bajchd->bjhd", pr.astype(v.dtype), v_nb)

    out = lax.map(one_row, jnp.arange(rows))
    out = jnp.moveaxis(out, 0, 1).reshape(b, t, D_MODEL)
    return out @ w_out


def swiglu(x, wg, wu, wd):
    return (jax.nn.silu(x @ wg) * (x @ wu)) @ wd


def moe_swiglu(x, w_router, wg, wu, wd):
    b, t, d = x.shape
    xf = x.reshape(b * t, d)
    logits = (xf @ w_router).astype(jnp.float32)
    top_v, top_i = lax.top_k(logits, TOP_K)
    gates = jax.nn.softmax(top_v, axis=-1)
    gate_full = jnp.sum(jax.nn.one_hot(top_i, N_EXPERTS, dtype=jnp.float32) * gates[..., None], axis=1)
    y = jnp.zeros_like(xf)
    for e in range(N_EXPERTS):
        y = y + gate_full[:, e:e + 1].astype(x.dtype) * swiglu(xf, wg[e], wu[e], wd[e])
    return y.reshape(b, t, d)


def trunk(x, ln_in_g, ln_in_b,
          ev_w_in, ev_conv_w, ev_conv_b, ev_dt_bias_f, ev_dt_bias_b, ev_a_log_f, ev_a_log_b, ev_d_skip,
          ev_gnorm_w, ev_w_out, ev_ln1_g, ev_ln1_b, ev_ffn_wg, ev_ffn_wu, ev_ffn_wd, ev_ln2_g, ev_ln2_b,
          od_w_qkv, od_rpb, od_w_out, od_ln1_g, od_ln1_b, od_router, od_wg, od_wu, od_wd, od_ln2_g, od_ln2_b):
    h = layer_norm(x, ln_in_g, ln_in_b)
    for l in range(DEPTH):
        if l % 2 == 0:
            e = l // 2
            mix = ssd_fnet_mixer(h, ev_w_in[e], ev_conv_w[e], ev_conv_b[e], ev_dt_bias_f[e], ev_dt_bias_b[e],
                                 ev_a_log_f[e], ev_a_log_b[e], ev_d_skip[e], ev_gnorm_w[e], ev_w_out[e])
            h = layer_norm(DEEPNORM_ALPHA * h + mix, ev_ln1_g[e], ev_ln1_b[e])
            h = layer_norm(DEEPNORM_ALPHA * h + swiglu(h, ev_ffn_wg[e], ev_ffn_wu[e], ev_ffn_wd[e]),
                           ev_ln2_g[e], ev_ln2_b[e])
        else:
            o = l // 2
            mix = neighbourhood_attention(h, od_w_qkv[o], od_rpb[o], od_w_out[o])
            h = layer_norm(DEEPNORM_ALPHA * h + mix, od_ln1_g[o], od_ln1_b[o])
            h = layer_norm(DEEPNORM_ALPHA * h + moe_swiglu(h, od_router[o], od_wg[o], od_wu[o], od_wd[o]),
                           od_ln2_g[o], od_ln2_b[o])
    return h


def setup_inputs(seed: int = 0) -> dict:
    key = jax.random.key(seed)
    ks = iter(jax.random.split(key, 48))
    nrm = lambda shape, scale: jax.random.normal(next(ks), shape, jnp.float32) * scale
    gain = lambda shape: 1.0 + nrm(shape, 0.02)

    def dt_bias(shape):
        dt = jnp.exp(jax.random.uniform(next(ks), shape, jnp.float32, math.log(1e-3), math.log(1e-1)))
        return dt + jnp.log(-jnp.expm1(-dt))

    def a_log(shape):
        return jnp.log(jax.random.uniform(next(ks), shape, jnp.float32, 1.0, 16.0))

    beta = DEEPNORM_BETA
    return {
        "x_prompt": nrm((BATCH, SEQ, D_MODEL), 1.0),
        "x_sample": nrm((DEC_BATCH, DEC_SEQ, D_MODEL), 1.0),
        "ln_in_g": gain((D_MODEL,)),
        "ln_in_b": nrm((D_MODEL,), 0.02),
        "ev_w_in": nrm((N_EVEN, D_MODEL, IN_PROJ_EVEN), D_MODEL ** -0.5),
        "ev_conv_w": nrm((N_EVEN, D_CONV, CONV_DIM), D_CONV ** -0.5),
        "ev_conv_b": nrm((N_EVEN, CONV_DIM), 0.02),
        "ev_dt_bias_f": dt_bias((N_EVEN, SSD_HEADS)),
        "ev_dt_bias_b": dt_bias((N_EVEN, SSD_HEADS)),
        "ev_a_log_f": a_log((N_EVEN, SSD_HEADS)),
        "ev_a_log_b": a_log((N_EVEN, SSD_HEADS)),
        "ev_d_skip": gain((N_EVEN, SSD_HEADS)),
        "ev_gnorm_w": gain((N_EVEN, D_INNER)),
        "ev_w_out": nrm((N_EVEN, MIX_EVEN, D_MODEL), beta * MIX_EVEN ** -0.5),
        "ev_ln1_g": gain((N_EVEN, D_MODEL)),
        "ev_ln1_b": nrm((N_EVEN, D_MODEL), 0.02),
        "ev_ffn_wg": nrm((N_EVEN, D_MODEL, D_FF_DENSE), D_MODEL ** -0.5),
        "ev_ffn_wu": nrm((N_EVEN, D_MODEL, D_FF_DENSE), D_MODEL ** -0.5),
        "ev_ffn_wd": nrm((N_EVEN, D_FF_DENSE, D_MODEL), beta * D_FF_DENSE ** -0.5),
        "ev_ln2_g": gain((N_EVEN, D_MODEL)),
        "ev_ln2_b": nrm((N_EVEN, D_MODEL), 0.02),
        "od_w_qkv": nrm((N_ODD, D_MODEL, 3 * D_MODEL), D_MODEL ** -0.5),
        "od_rpb": nrm((N_ODD, NA_HEADS, 2 * NA_WIN_H - 1, 2 * NA_WIN_W - 1), 0.02),
        "od_w_out": nrm((N_ODD, D_MODEL, D_MODEL), beta * D_MODEL ** -0.5),
        "od_ln1_g": gain((N_ODD, D_MODEL)),
        "od_ln1_b": nrm((N_ODD, D_MODEL), 0.02),
        "od_router": nrm((N_ODD, D_MODEL, N_EXPERTS), D_MODEL ** -0.5),
        "od_wg": nrm((N_ODD, N_EXPERTS, D_MODEL, D_FF_EXPERT), D_MODEL ** -0.5),
        "od_wu": nrm((N_ODD, N_EXPERTS, D_MODEL, D_FF_EXPERT), D_MODEL ** -0.5),
        "od_wd": nrm((N_ODD, N_EXPERTS, D_FF_EXPERT, D_MODEL), beta * D_FF_EXPERT ** -0.5),
        "od_ln2_g": gain((N_ODD, D_MODEL)),
        "od_ln2_b": nrm((N_ODD, D_MODEL), 0.02),
    }


def reference(x_prompt, x_sample, ln_in_g, ln_in_b,
              ev_w_in, ev_conv_w, ev_conv_b, ev_dt_bias_f, ev_dt_bias_b, ev_a_log_f, ev_a_log_b, ev_d_skip,
              ev_gnorm_w, ev_w_out, ev_ln1_g, ev_ln1_b, ev_ffn_wg, ev_ffn_wu, ev_ffn_wd, ev_ln2_g, ev_ln2_b,
              od_w_qkv, od_rpb, od_w_out, od_ln1_g, od_ln1_b, od_router, od_wg, od_wu, od_wd, od_ln2_g, od_ln2_b):
    weights = (ln_in_g, ln_in_b,
               ev_w_in, ev_conv_w, ev_conv_b, ev_dt_bias_f, ev_dt_bias_b, ev_a_log_f, ev_a_log_b, ev_d_skip,
               ev_gnorm_w, ev_w_out, ev_ln1_g, ev_ln1_b, ev_ffn_wg, ev_ffn_wu, ev_ffn_wd, ev_ln2_g, ev_ln2_b,
               od_w_qkv, od_rpb, od_w_out, od_ln1_g, od_ln1_b, od_router, od_wg, od_wu, od_wd, od_ln2_g, od_ln2_b)
    y_prompt = trunk(x_prompt, *weights)
    y_sample = trunk(x_sample, *weights)
    return (y_prompt, y_sample)
```

```python
import functools
import math

import numpy as np
import jax
import jax.numpy as jnp
from jax import lax
from jax.experimental import pallas as pl
from jax.experimental.pallas import tpu as pltpu

F32 = jnp.float32
BF16 = jnp.bfloat16

D_MODEL = 1024
GRID_W = 64
D_INNER = 2048
SSD_HEADS = 32
SSD_GROUPS = 4
HEADS_PER_GROUP = SSD_HEADS // SSD_GROUPS
GROUP_DIM = D_INNER // SSD_GROUPS
D_STATE = 128
D_CONV = 5
CHUNK = 128
CONV_DIM = D_INNER + 2 * SSD_GROUPS * D_STATE
FNET_GROUPS = 4
FNET_GROUP_DIM = 256
FNET_WIDTH = 1024
FNET_N2 = 128
NA_HEADS = 16
NA_HEAD_DIM = 64
NA_WIN_H = 8
NA_WIN_W = 16
D_FF_DENSE = 2816
N_EXPERTS = 8
D_FF_EXPERT = 3584
LN_EPS = 1e-5
RMS_EPS = 1e-5
DEPTH = 2
ALPHA = (2 * DEPTH) ** 0.25
LANES = 128
HALO = 16
NEG_BIG = -1e30
VMEM_LIMIT = 56 * 1024 * 1024


def _cparams(sem):
    return pltpu.CompilerParams(dimension_semantics=sem, vmem_limit_bytes=VMEM_LIMIT)


def _const_spec(shape):
    nd = len(shape)
    return pl.BlockSpec(shape, lambda *_: (0,) * nd)


def _dot(a, b):
    return jnp.dot(a, b, preferred_element_type=F32)


def _layer_norm(xf, g, b):
    mu = jnp.mean(xf, axis=-1, keepdims=True)
    xc = xf - mu
    var = jnp.mean(xc * xc, axis=-1, keepdims=True)
    return xc * lax.rsqrt(var + LN_EPS) * g + b


def _silu(x):
    return x * jax.nn.sigmoid(x)


def _ln_inproj_kernel(x_ref, g_ref, b_ref, wz_ref, wx_ref, wdt_ref, wu_ref,
                      h_ref, z_ref, xbc_ref, dt_ref, u_ref):
    h = _layer_norm(x_ref[...], g_ref[...], b_ref[...])
    h_ref[...] = h
    hb = h.astype(BF16)
    for j in range(0, D_INNER, 512):
        z_ref[:, j:j + 512] = _dot(hb, wz_ref[:, j:j + 512]).astype(BF16)
    for j in range(0, CONV_DIM, 512):
        xbc_ref[:, j:j + 512] = _dot(hb, wx_ref[:, j:j + 512]).astype(BF16)
    dt_ref[...] = _dot(hb, wdt_ref[...])
    for j in range(0, FNET_WIDTH, 512):
        u_ref[:, j:j + 512] = _dot(hb, wu_ref[:, j:j + 512]).astype(BF16)


def ln_inproj(x, g, b, wz, wx, wdt, wu, tm):
    n = x.shape[0]
    row = lambda w: pl.BlockSpec((tm, w), lambda i: (i, 0))
    return pl.pallas_call(
        _ln_inproj_kernel,
        grid=(n // tm,),
        in_specs=[row(D_MODEL), _const_spec((1, D_MODEL)), _const_spec((1, D_MODEL)),
                  _const_spec(wz.shape), _const_spec(wx.shape), _const_spec(wdt.shape),
                  _const_spec(wu.shape)],
        out_specs=[row(D_MODEL), row(D_INNER), row(CONV_DIM), row(LANES), row(FNET_WIDTH)],
        out_shape=[jax.ShapeDtypeStruct((n, D_MODEL), F32),
                   jax.ShapeDtypeStruct((n, D_INNER), BF16),
                   jax.ShapeDtypeStruct((n, CONV_DIM), BF16),
                   jax.ShapeDtypeStruct((n, LANES), F32),
                   jax.ShapeDtypeStruct((n, FNET_WIDTH), BF16)],
        compiler_params=_cparams(("parallel",)),
        name="ln_inproj",
    )(x, g, b, wz, wx, wdt, wu)


def _conv_silu_kernel(prev_ref, main_ref, next_ref, w_ref, b_ref, o_ref, scr, *, tt, nt):
    i = pl.program_id(1)
    prev = prev_ref[0].astype(F32)
    nxt = next_ref[0].astype(F32)
    scr[0:HALO, :] = jnp.where(i == 0, 0.0, prev)
    scr[HALO:HALO + tt, :] = main_ref[0].astype(F32)
    scr[HALO + tt:2 * HALO + tt, :] = jnp.where(i == nt - 1, 0.0, nxt)
    acc = b_ref[...] + w_ref[0:1, :] * scr[HALO - 2:HALO - 2 + tt, :]
    for k in range(1, D_CONV):
        acc = acc + w_ref[k:k + 1, :] * scr[HALO - 2 + k:HALO - 2 + k + tt, :]
    o_ref[0] = _silu(acc).astype(BF16)


def conv_silu(xbc, w, b, tt, tc):
    bsz, t, c = xbc.shape
    nt = t // tt
    hb = tt // HALO
    kern = functools.partial(_conv_silu_kernel, tt=tt, nt=nt)
    return pl.pallas_call(
        kern,
        grid=(bsz, nt, c // tc),
        in_specs=[
            pl.BlockSpec((1, HALO, tc), lambda bi, i, ci: (bi, jnp.maximum(i * hb - 1, 0), ci)),
            pl.BlockSpec((1, tt, tc), lambda bi, i, ci: (bi, i, ci)),
            pl.BlockSpec((1, HALO, tc), lambda bi, i, ci: (bi, jnp.minimum((i + 1) * hb, t // HALO - 1), ci)),
            pl.BlockSpec((D_CONV, tc), lambda bi, i, ci: (0, ci)),
            pl.BlockSpec((1, tc), lambda bi, i, ci: (0, ci)),
        ],
        out_specs=pl.BlockSpec((1, tt, tc), lambda bi, i, ci: (bi, i, ci)),
        out_shape=jax.ShapeDtypeStruct((bsz, t, c), BF16),
        scratch_shapes=[pltpu.VMEM((tt + 2 * HALO, tc), F32)],
        compiler_params=_cparams(("parallel", "parallel", "parallel")),
        name="conv_silu",
    )(xbc, xbc, xbc, w, b)


def _split3(x):
    hi = x.astype(BF16)
    r1 = x - hi.astype(F32)
    mid = r1.astype(BF16)
    lo = (r1 - mid.astype(F32)).astype(BF16)
    return hi, mid, lo


def _ssd_prep_kernel(dt_ref, bias_ref, alog_ref, tri_ref, trit_ref, sel_ref, pcol_ref, q_ref, *, cpp):
    lane = lax.broadcasted_iota(jnp.int32, (CHUNK, LANES), 1)
    fwd = lane < SSD_HEADS
    a_coef = -jnp.exp(alog_ref[...])
    for c in range(cpp):
        rows = slice(c * CHUNK, (c + 1) * CHUNK)
        raw = dt_ref[0, rows, :] + bias_ref[...]
        dt = jnp.maximum(raw, 0.0) + jnp.log1p(jnp.exp(-jnp.abs(raw)))
        a = dt * a_coef
        cs_f = jnp.dot(tri_ref[...], a, precision=lax.Precision.HIGHEST, preferred_element_type=F32)
        cs_b = jnp.dot(trit_ref[...], a, precision=lax.Precision.HIGHEST, preferred_element_type=F32)
        cs = jnp.where(fwd, cs_f, cs_b)
        end = jnp.where(fwd[0:1], cs[CHUNK - 1:CHUNK, :], cs[0:1, :])
        wend = jnp.exp(end - cs) * dt
        hi, mid, lo = _split3(cs)
        src = jnp.concatenate([hi, mid, lo, wend.astype(BF16)], axis=1)
        pc = _dot(src, sel_ref[...]).astype(BF16)
        pcol_ref[0, 0, rows, :] = pc[:, :SSD_GROUPS * LANES]
        pcol_ref[1, 0, rows, :] = pc[:, SSD_GROUPS * LANES:]
        cs_t = cs.T
        dt_t = dt.T
        for d in range(2):
            for g in range(SSD_GROUPS):
                r0 = d * SSD_HEADS + g * HEADS_PER_GROUP
                q_ref[d, 0, g, c, 0:HEADS_PER_GROUP, :] = cs_t[r0:r0 + HEADS_PER_GROUP, :]
                q_ref[d, 0, g, c, HEADS_PER_GROUP:2 * HEADS_PER_GROUP, :] = dt_t[r0:r0 + HEADS_PER_GROUP, :]


def _prep_select_matrix():
    sel = np.zeros((4 * LANES, 2 * SSD_GROUPS * LANES), np.float32)
    for d in range(2):
        for g in range(SSD_GROUPS):
            for q in range(4):
                for i in range(HEADS_PER_GROUP):
                    src = q * LANES + d * SSD_HEADS + g * HEADS_PER_GROUP + i
                    dst = d * SSD_GROUPS * LANES + g * LANES + q * HEADS_PER_GROUP + i
                    sel[src, dst] = 1.0
    return sel


def ssd_prep(dt_raw, bias_row, alog_row, cpp):
    bsz, t, _ = dt_raw.shape
    nc = t // CHUNK
    tri = np.tril(np.ones((CHUNK, CHUNK), np.float32))
    kern = functools.partial(_ssd_prep_kernel, cpp=cpp)
    return pl.pallas_call(
        kern,
        grid=(bsz, nc // cpp),
        in_specs=[pl.BlockSpec((1, cpp * CHUNK, LANES), lambda b, j: (b, j, 0)),
                  _const_spec((1, LANES)), _const_spec((1, LANES)),
                  _const_spec((CHUNK, CHUNK)), _const_spec((CHUNK, CHUNK)),
                  _const_spec((4 * LANES, 2 * SSD_GROUPS * LANES))],
        out_specs=[pl.BlockSpec((2, 1, cpp * CHUNK, SSD_GROUPS * LANES), lambda b, j: (0, b, j, 0)),
                   pl.BlockSpec((2, 1, SSD_GROUPS, cpp, 2 * HEADS_PER_GROUP, CHUNK),
                                lambda b, j: (0, b, 0, j, 0, 0))],
        out_shape=[jax.ShapeDtypeStruct((2, bsz, t, SSD_GROUPS * LANES), BF16),
                   jax.ShapeDtypeStruct((2, bsz, SSD_GROUPS, nc, 2 * HEADS_PER_GROUP, CHUNK), F32)],
        compiler_params=_cparams(("parallel", "parallel")),
        name="ssd_prep",
    )(dt_raw, bias_row, alog_row, jnp.asarray(tri), jnp.asarray(tri.T),
      jnp.asarray(_prep_select_matrix(), BF16))


def _expand_matrix():
    e = np.zeros((LANES, HEADS_PER_GROUP * LANES + GROUP_DIM), np.float32)
    for q in range(3):
        for h in range(HEADS_PER_GROUP):
            e[q * HEADS_PER_GROUP + h, h * LANES:(h + 1) * LANES] = 1.0
    for h in range(HEADS_PER_GROUP):
        e[3 * HEADS_PER_GROUP + h,
          HEADS_PER_GROUP * LANES + h * 64:HEADS_PER_GROUP * LANES + (h + 1) * 64] = 1.0
    return e


def _ssd_kernel(*refs, rev, cps, fuse):
    if fuse:
        xs_ref, b_ref, c_ref, p_ref, q_ref, e_ref, yb_ref, z_ref, dsk_ref, gn_ref, o_ref, st_ref = refs
    else:
        xs_ref, b_ref, c_ref, p_ref, q_ref, e_ref, o_ref, st_ref = refs

    @pl.when(pl.program_id(2) == 0)
    def _():
        st_ref[...] = jnp.zeros_like(st_ref)

    row = lax.broadcasted_iota(jnp.int32, (CHUNK, CHUNK), 0)
    col = lax.broadcasted_iota(jnp.int32, (CHUNK, CHUNK), 1)
    mask = (col >= row) if rev else (col <= row)
    lo_half = lax.broadcasted_iota(jnp.int32, (CHUNK, LANES), 1) < 64
    nb = HEADS_PER_GROUP * LANES
    end = 0 if rev else CHUNK - 1
    order = range(cps - 1, -1, -1) if rev else range(cps)
    for c in order:
        rows = slice(c * CHUNK, (c + 1) * CHUNK)
        xf = xs_ref[0, rows, :].astype(F32)
        bm = b_ref[0, rows, :]
        cm = c_ref[0, rows, :]
        ex = _dot(p_ref[0, 0, rows, :], e_ref[...])
        qv = q_ref[0, 0, 0, c]
        cb = lax.dot_general(cm, bm, (((1,), (1,)), ((), ())), preferred_element_type=F32)
        cs_parts = []
        y_parts = []
        for j in range(HEADS_PER_GROUP // 2):
            h1, h2 = 2 * j, 2 * j + 1
            c1 = ex[:, h1 * LANES:(h1 + 1) * LANES]
            c2 = ex[:, h2 * LANES:(h2 + 1) * LANES]
            cs_parts.append(jnp.where(lo_half, c1, c2))
            w1 = jnp.exp(jnp.where(mask, c1 - qv[h1:h1 + 1, :], -jnp.inf)) * (
                cb * qv[HEADS_PER_GROUP + h1:HEADS_PER_GROUP + h1 + 1, :])
            w2 = jnp.exp(jnp.where(mask, c2 - qv[h2:h2 + 1, :], -jnp.inf)) * (
                cb * qv[HEADS_PER_GROUP + h2:HEADS_PER_GROUP + h2 + 1, :])
            wp = jnp.concatenate([w1, w2], axis=1).astype(BF16)
            xp = xf[:, j * LANES:(j + 1) * LANES]
            rhs = jnp.concatenate([jnp.where(lo_half, xp, 0.0), jnp.where(lo_half, 0.0, xp)],
                                  axis=0).astype(BF16)
            y_parts.append(_dot(wp, rhs))
        ydiag = jnp.concatenate(y_parts, axis=1)
        expcs = jnp.exp(jnp.concatenate(cs_parts, axis=1))
        decay = expcs[end:end + 1, :]
        st = st_ref[...]
        yoff = _dot(cm, st.astype(BF16)) * expcs
        xsw = (xf * ex[:, nb:nb + GROUP_DIM]).astype(BF16)
        st_ref[...] = st * decay + lax.dot_general(bm, xsw, (((0,), (0,)), ((), ())),
                                                   preferred_element_type=F32)
        y = ydiag + yoff
        if fuse:
            y = y + yb_ref[0, rows, :] + dsk_ref[...] * xf
            y = y * _silu(z_ref[0, rows, :].astype(F32))
            ms = jnp.mean(y * y, axis=-1, keepdims=True)
            o_ref[0, rows, :] = (y * lax.rsqrt(ms + RMS_EPS) * gn_ref[...]).astype(BF16)
        else:
            o_ref[0, rows, :] = y


def ssd_scan(xbc_act, pcol, q, rev, cps, fused=None):
    bsz, t, _ = xbc_act.shape
    ncb = t // (cps * CHUNK)
    r = cps * CHUNK
    d = 1 if rev else 0
    cidx = (lambda j: ncb - 1 - j) if rev else (lambda j: j)
    e = jnp.asarray(_expand_matrix(), BF16)
    b0 = D_INNER // LANES
    c0 = b0 + SSD_GROUPS
    in_specs = [
        pl.BlockSpec((1, r, GROUP_DIM), lambda b, g, j: (b, cidx(j), g)),
        pl.BlockSpec((1, r, LANES), lambda b, g, j: (b, cidx(j), b0 + g)),
        pl.BlockSpec((1, r, LANES), lambda b, g, j: (b, cidx(j), c0 + g)),
        pl.BlockSpec((1, 1, r, LANES), lambda b, g, j: (d, b, cidx(j), g)),
        pl.BlockSpec((1, 1, 1, cps, 2 * HEADS_PER_GROUP, CHUNK), lambda b, g, j: (d, b, g, cidx(j), 0, 0)),
        _const_spec(e.shape),
    ]
    args = [xbc_act, xbc_act, xbc_act, pcol, q, e]
    if fused is not None:
        yb, z, dsk, gn = fused
        in_specs += [
            pl.BlockSpec((1, r, GROUP_DIM), lambda b, g, j: (b, cidx(j), g)),
            pl.BlockSpec((1, r, GROUP_DIM), lambda b, g, j: (b, cidx(j), g)),
            pl.BlockSpec((1, GROUP_DIM), lambda b, g, j: (0, g)),
            pl.BlockSpec((1, GROUP_DIM), lambda b, g, j: (0, g)),
        ]
        args += [yb, z, dsk, gn]
        out_dtype = BF16
    else:
        out_dtype = F32
    kern = functools.partial(_ssd_kernel, rev=rev, cps=cps, fuse=fused is not None)
    return pl.pallas_call(
        kern,
        grid=(bsz, SSD_GROUPS, ncb),
        in_specs=in_specs,
        out_specs=pl.BlockSpec((1, r, GROUP_DIM), lambda b, g, j: (b, cidx(j), g)),
        out_shape=jax.ShapeDtypeStruct((bsz, t, D_INNER), out_dtype),
        scratch_shapes=[pltpu.VMEM((D_STATE, GROUP_DIM), F32)],
        compiler_params=_cparams(("parallel", "parallel", "arbitrary")),
        name="ssd_bwd" if rev else "ssd_fwd",
    )(*args)


def _dft_cos_sin(n):
    k = np.arange(n, dtype=np.float64)
    ang = 2.0 * np.pi * np.outer(k, k) / n
    return np.cos(ang), np.sin(ang)


def _fnet_s1_kernel(u_ref, f1_ref, cos_ref, sin_ref, o_ref, *, n1, nb):
    c = FNET_WIDTH
    for q in range(nb):
        a = _dot(f1_ref[...], u_ref[0, :, q * c:(q + 1) * c])
        ar = a[:n1]
        ai = a[n1:]
        ct = jnp.concatenate([cos_ref[q]] * (c // LANES), axis=1)
        st = jnp.concatenate([sin_ref[q]] * (c // LANES), axis=1)
        o_ref[0, 0:n1, q * c:(q + 1) * c] = (ar * ct + ai * st).astype(BF16)
        o_ref[0, n1:2 * n1, q * c:(q + 1) * c] = (ai * ct - ar * st).astype(BF16)


def _fnet_s2_kernel(br_ref, bi_ref, f2_ref, wc_ref, o_ref):
    n2 = FNET_N2
    rhs = jnp.concatenate([br_ref[0, 0], bi_ref[0, 0]], axis=0)
    g = _dot(f2_ref[...], rhs).astype(BF16)
    for grp in range(FNET_GROUPS):
        cols = slice(grp * FNET_GROUP_DIM, (grp + 1) * FNET_GROUP_DIM)
        lhs = jnp.concatenate([g[:n2, cols], g[n2:, cols]], axis=1)
        o_ref[0, :, cols] = _dot(lhs, wc_ref[...]).astype(BF16)


def fnet_mix(u, nb=4):
    bsz, t, c = u.shape
    n2 = FNET_N2
    n1 = t // n2
    c1, s1 = _dft_cos_sin(n1)
    c2, s2 = _dft_cos_sin(n2)
    cc, sc = _dft_cos_sin(FNET_GROUP_DIM)
    f1 = jnp.asarray(np.concatenate([c1, -s1], axis=0), BF16)
    f2 = jnp.asarray(np.block([[c2, s2], [-s2, c2]]), BF16)
    scale = 1.0 / math.sqrt(t * FNET_GROUP_DIM)
    wc = jnp.asarray(np.concatenate([cc, sc], axis=0) * scale, BF16)
    ang = 2.0 * np.pi * np.outer(np.arange(n2), np.arange(n1)) / t
    tw_cos = jnp.asarray(np.repeat(np.cos(ang)[:, :, None], LANES, axis=2), F32)
    tw_sin = jnp.asarray(np.repeat(np.sin(ang)[:, :, None], LANES, axis=2), F32)

    u2 = u.reshape(bsz, n1, n2 * c)
    s1_out = pl.pallas_call(
        functools.partial(_fnet_s1_kernel, n1=n1, nb=nb),
        grid=(bsz, n2 // nb),
        in_specs=[pl.BlockSpec((1, n1, nb * c), lambda b, j: (b, 0, j)),
                  _const_spec((2 * n1, n1)),
                  pl.BlockSpec((nb, n1, LANES), lambda b, j: (j, 0, 0)),
                  pl.BlockSpec((nb, n1, LANES), lambda b, j: (j, 0, 0))],
        out_specs=pl.BlockSpec((1, 2 * n1, nb * c), lambda b, j: (b, 0, j)),
        out_shape=jax.ShapeDtypeStruct((bsz, 2 * n1, n2 * c), BF16),
        compiler_params=_cparams(("parallel", "parallel")),
        name="fnet_stage1",
    )(u2, f1, tw_cos, tw_sin)

    bv = s1_out.reshape(bsz, 2 * n1, n2, c)
    y = pl.pallas_call(
        _fnet_s2_kernel,
        grid=(bsz, n1),
        in_specs=[pl.BlockSpec((1, 1, n2, c), lambda b, k: (b, k, 0, 0)),
                  pl.BlockSpec((1, 1, n2, c), lambda b, k: (b, n1 + k, 0, 0)),
                  _const_spec((2 * n2, 2 * n2)),
                  _const_spec((2 * FNET_GROUP_DIM, FNET_GROUP_DIM))],
        out_specs=pl.BlockSpec((1, n2, c), lambda b, k: (b, 0, k)),
        out_shape=jax.ShapeDtypeStruct((bsz, n2, n1 * c), BF16),
        compiler_params=_cparams(("parallel", "parallel")),
        name="fnet_stage2",
    )(bv, bv, f2, wc)
    return y.reshape(bsz, t, c)


def _proj_res_ln_kernel(*refs, n_in):
    xs = refs[:n_in]
    ws = refs[n_in:2 * n_in]
    h_ref, g_ref, b_ref, o_ref = refs[2 * n_in:]
    acc = _dot(xs[0][...], ws[0][...])
    for x_ref, w_ref in zip(xs[1:], ws[1:]):
        acc = acc + _dot(x_ref[...], w_ref[...])
    o_ref[...] = _layer_norm(ALPHA * h_ref[...] + acc, g_ref[...], b_ref[...])


def proj_res_ln(xs, ws, h, g, b, tm):
    n = h.shape[0]
    n_in = len(xs)
    in_specs = [pl.BlockSpec((tm, x.shape[1]), lambda i: (i, 0)) for x in xs]
    in_specs += [_const_spec(w.shape) for w in ws]
    in_specs += [pl.BlockSpec((tm, D_MODEL), lambda i: (i, 0)),
                 _const_spec((1, D_MODEL)), _const_spec((1, D_MODEL))]
    return pl.pallas_call(
        functools.partial(_proj_res_ln_kernel, n_in=n_in),
        grid=(n // tm,),
        in_specs=in_specs,
        out_specs=pl.BlockSpec((tm, D_MODEL), lambda i: (i, 0)),
        out_shape=jax.ShapeDtypeStruct((n, D_MODEL), F32),
        compiler_params=_cparams(("parallel",)),
        name="proj_res_ln",
    )(*xs, *ws, h, g, b)


def _ffn_res_ln_kernel(h_ref, wg_ref, wu_ref, wd_ref, g_ref, b_ref, o_ref, act_ref, *, tf):
    h = h_ref[...]
    hb = h.astype(BF16)
    for j in range(0, D_FF_DENSE, tf):
        gate = _dot(hb, wg_ref[:, j:j + tf])
        up = _dot(hb, wu_ref[:, j:j + tf])
        act_ref[:, j:j + tf] = (_silu(gate) * up).astype(BF16)
    y = _dot(act_ref[...], wd_ref[...])
    o_ref[...] = _layer_norm(ALPHA * h + y, g_ref[...], b_ref[...])


def ffn_res_ln(h, wg, wu, wd, g, b, tm, tf=256):
    n = h.shape[0]
    return pl.pallas_call(
        functools.partial(_ffn_res_ln_kernel, tf=tf),
        grid=(n // tm,),
        in_specs=[pl.BlockSpec((tm, D_MODEL), lambda i: (i, 0)),
                  _const_spec(wg.shape), _const_spec(wu.shape), _const_spec(wd.shape),
                  _const_spec((1, D_MODEL)), _const_spec((1, D_MODEL))],
        out_specs=pl.BlockSpec((tm, D_MODEL), lambda i: (i, 0)),
        out_shape=jax.ShapeDtypeStruct((n, D_MODEL), F32),
        scratch_shapes=[pltpu.VMEM((tm, D_FF_DENSE), BF16)],
        compiler_params=_cparams(("parallel",)),
        name="ffn_res_ln",
    )(h, wg, wu, wd, g, b)


def _qkv_kernel(h_ref, w_ref, q_ref, k_ref, v_ref):
    hb = h_ref[...].astype(BF16)
    d = D_MODEL
    scale = NA_HEAD_DIM ** -0.5
    for j in range(0, d, 512):
        q_ref[:, j:j + 512] = (_dot(hb, w_ref[:, j:j + 512]) * scale).astype(BF16)
        k_ref[:, j:j + 512] = _dot(hb, w_ref[:, d + j:d + j + 512]).astype(BF16)
        v_ref[:, j:j + 512] = _dot(hb, w_ref[:, 2 * d + j:2 * d + j + 512]).astype(BF16)


def qkv_proj(h, w, tm):
    n = h.shape[0]
    row = pl.BlockSpec((tm, D_MODEL), lambda i: (i, 0))
    sds = jax.ShapeDtypeStruct((n, D_MODEL), BF16)
    return pl.pallas_call(
        _qkv_kernel,
        grid=(n // tm,),
        in_specs=[row, _const_spec(w.shape)],
        out_specs=[row, row, row],
        out_shape=[sds, sds, sds],
        compiler_params=_cparams(("parallel",)),
        name="qkv_proj",
    )(h, w)


def _natten_kernel(q_ref, k_ref, v_ref, bias_ref, o_ref, *, rows, hw):
    i = pl.program_id(2)
    r0 = jnp.clip(i - NA_WIN_H // 2, 0, rows - NA_WIN_H)
    start = pl.multiple_of(r0 * GRID_W, GRID_W)
    nk = NA_WIN_H * GRID_W
    kw = k_ref[0, pl.ds(start, nk), :]
    vw = v_ref[0, pl.ds(start, nk), :]
    lo_half = lax.broadcasted_iota(jnp.int32, (GRID_W, LANES), 1) < NA_HEAD_DIM
    for p in range(hw // LANES):
        cols = slice(p * LANES, (p + 1) * LANES)
        qp = q_ref[0, :, cols].astype(F32)
        qbd = jnp.concatenate([jnp.where(lo_half, qp, 0.0), jnp.where(lo_half, 0.0, qp)],
                              axis=0).astype(BF16)
        s = lax.dot_general(qbd, kw[:, cols], (((1,), (1,)), ((), ())), preferred_element_type=F32)
        s = s + bias_ref[0, p]
        m = jnp.max(s, axis=-1, keepdims=True)
        e = jnp.exp(s - m)
        l = jnp.sum(e, axis=-1, keepdims=True)
        o2 = _dot(e.astype(BF16), vw[:, cols]) * (1.0 / l)
        o_ref[0, :, cols] = jnp.where(lo_half, o2[:GRID_W], o2[GRID_W:]).astype(BF16)


def _natten_bias_table(rpb):
    j = np.arange(GRID_W)[:, None]
    kc = np.arange(GRID_W)[None, :]
    cstart = np.clip(j - NA_WIN_W // 2, 0, GRID_W - NA_WIN_W)
    valid = (kc >= cstart) & (kc < cstart + NA_WIN_W)
    coff = np.clip(kc - j + NA_WIN_W - 1, 0, 2 * NA_WIN_W - 2)
    dd = np.arange(NA_WIN_H)[:, None]
    a = np.arange(NA_WIN_H)[None, :]
    roff = a + (NA_WIN_H - 1) - dd
    t = rpb[:, roff[:, :, None, None], coff[None, None, :, :]]
    t = jnp.where(valid[None, None, None], t.astype(F32), NEG_BIG)
    t = jnp.transpose(t, (1, 0, 3, 2, 4))
    return t.reshape(NA_WIN_H, NA_HEADS // 2, 2 * GRID_W, NA_WIN_H * GRID_W)


def natten(q, k, v, bias, hw=512):
    bsz, t, d = q.shape
    rows = t // GRID_W
    nh = d // hw
    pp = hw // LANES

    def bias_idx(b, hh, i):
        r0 = jnp.clip(i - NA_WIN_H // 2, 0, rows - NA_WIN_H)
        return (i - r0, hh, 0, 0)

    return pl.pallas_call(
        functools.partial(_natten_kernel, rows=rows, hw=hw),
        grid=(bsz, nh, rows),
        in_specs=[pl.BlockSpec((1, GRID_W, hw), lambda b, hh, i: (b, i, hh)),
                  pl.BlockSpec((1, t, hw), lambda b, hh, i: (b, 0, hh)),
                  pl.BlockSpec((1, t, hw), lambda b, hh, i: (b, 0, hh)),
                  pl.BlockSpec((1, pp, 2 * GRID_W, NA_WIN_H * GRID_W), bias_idx)],
        out_specs=pl.BlockSpec((1, GRID_W, hw), lambda b, hh, i: (b, i, hh)),
        out_shape=jax.ShapeDtypeStruct((bsz, t, d), BF16),
        compiler_params=_cparams(("parallel", "parallel", "arbitrary")),
        name="natten",
    )(q, k, v, bias)


def _router_kernel(h_ref, w_ref, gate_ref):
    logits = jnp.dot(h_ref[...], w_ref[...], precision=lax.Precision.HIGHEST,
                     preferred_element_type=F32)
    lane = lax.broadcasted_iota(jnp.int32, logits.shape, 1)
    logits = jnp.where(lane < N_EXPERTS, logits, -jnp.inf)
    m1 = jnp.max(logits, axis=-1, keepdims=True)
    i1 = jnp.min(jnp.where(logits == m1, lane, LANES), axis=-1, keepdims=True)
    rest = jnp.where(lane == i1, -jnp.inf, logits)
    m2 = jnp.max(rest, axis=-1, keepdims=True)
    i2 = jnp.min(jnp.where(rest == m2, lane, LANES), axis=-1, keepdims=True)
    e2 = jnp.exp(m2 - m1)
    inv = 1.0 / (1.0 + e2)
    gate_ref[...] = jnp.where(lane == i1, inv, 0.0) + jnp.where(lane == i2, e2 * inv, 0.0)


def router(h, w_pad, tm):
    n = h.shape[0]
    return pl.pallas_call(
        _router_kernel,
        grid=(n // tm,),
        in_specs=[pl.BlockSpec((tm, D_MODEL), lambda i: (i, 0)), _const_spec(w_pad.shape)],
        out_specs=pl.BlockSpec((tm, LANES), lambda i: (i, 0)),
        out_shape=jax.ShapeDtypeStruct((n, LANES), F32),
        compiler_params=_cparams(("parallel",)),
        name="router",
    )(h, w_pad)


def _moe_dense_kernel(h_ref, gate_ref, wg_ref, wu_ref, wd_ref, g_ref, b_ref, o_ref, acc_ref, *, nf):
    e = pl.program_id(1)
    f = pl.program_id(2)

    @pl.when((e == 0) & (f == 0))
    def _():
        acc_ref[...] = jnp.zeros_like(acc_ref)

    hb = h_ref[...].astype(BF16)
    gate = gate_ref[...]
    lane = lax.broadcasted_iota(jnp.int32, gate.shape, 1)
    ge = jnp.sum(jnp.where(lane == e, gate, 0.0), axis=-1, keepdims=True)
    act = _silu(_dot(hb, wg_ref[0])) * _dot(hb, wu_ref[0]) * ge
    acc_ref[...] += _dot(act.astype(BF16), wd_ref[0])

    @pl.when((e == N_EXPERTS - 1) & (f == nf - 1))
    def _():
        o_ref[...] = _layer_norm(ALPHA * h_ref[...] + acc_ref[...], g_ref[...], b_ref[...])


def moe_dense_res_ln(h, gate, wg, wu, wd, g, b, tm, tf):
    n = h.shape[0]
    nf = D_FF_EXPERT // tf
    row = lambda w: pl.BlockSpec((tm, w), lambda i, e, f: (i, 0))
    return pl.pallas_call(
        functools.partial(_moe_dense_kernel, nf=nf),
        grid=(n // tm, N_EXPERTS, nf),
        in_specs=[row(D_MODEL), row(LANES),
                  pl.BlockSpec((1, D_MODEL, tf), lambda i, e, f: (e, 0, f)),
                  pl.BlockSpec((1, D_MODEL, tf), lambda i, e, f: (e, 0, f)),
                  pl.BlockSpec((1, tf, D_MODEL), lambda i, e, f: (e, f, 0)),
                  _const_spec((1, D_MODEL)), _const_spec((1, D_MODEL))],
        out_specs=row(D_MODEL),
        out_shape=jax.ShapeDtypeStruct((n, D_MODEL), F32),
        scratch_shapes=[pltpu.VMEM((tm, D_MODEL), F32)],
        compiler_params=_cparams(("parallel", "arbitrary", "arbitrary")),
        name="moe_dense_res_ln",
    )(h, gate, wg, wu, wd, g, b)


def _row(v):
    return v.reshape(1, -1).astype(F32)


def _pad_lanes(v, fill=0.0):
    v = v.astype(F32)
    pad = LANES - v.shape[-1]
    return jnp.concatenate([v, jnp.full(v.shape[:-1] + (pad,), fill, F32)], axis=-1)


def prepare_weights(p):
    w = {}
    w_in = p["ev_w_in"][0]
    o1 = D_INNER
    o2 = o1 + CONV_DIM
    o3 = o2 + 2 * SSD_HEADS
    w["wz"] = w_in[:, :o1].astype(BF16)
    w["wx"] = w_in[:, o1:o2].astype(BF16)
    w["wdt"] = _pad_lanes(w_in[:, o2:o3]).astype(BF16)
    w["wu"] = w_in[:, o3:].astype(BF16)
    w["ln_in_g"], w["ln_in_b"] = _row(p["ln_in_g"]), _row(p["ln_in_b"])
    w["conv_w"] = p["ev_conv_w"][0].astype(F32)
    w["conv_b"] = _row(p["ev_conv_b"][0])
    w["dt_bias"] = _pad_lanes(jnp.concatenate([p["ev_dt_bias_f"][0], p["ev_dt_bias_b"][0]])[None, :])
    w["a_log"] = _pad_lanes(jnp.concatenate([p["ev_a_log_f"][0], p["ev_a_log_b"][0]])[None, :])
    w["d_skip"] = jnp.repeat(p["ev_d_skip"][0].astype(F32), D_INNER // SSD_HEADS)[None, :]
    w["gnorm"] = _row(p["ev_gnorm_w"][0])
    w_out = p["ev_w_out"][0]
    w["wo_ssd"] = w_out[:D_INNER].astype(BF16)
    w["wo_fn"] = w_out[D_INNER:].astype(BF16)
    w["ev_ln1_g"], w["ev_ln1_b"] = _row(p["ev_ln1_g"][0]), _row(p["ev_ln1_b"][0])
    w["ffn_wg"] = p["ev_ffn_wg"][0].astype(BF16)
    w["ffn_wu"] = p["ev_ffn_wu"][0].astype(BF16)
    w["ffn_wd"] = p["ev_ffn_wd"][0].astype(BF16)
    w["ev_ln2_g"], w["ev_ln2_b"] = _row(p["ev_ln2_g"][0]), _row(p["ev_ln2_b"][0])
    w["w_qkv"] = p["od_w_qkv"][0].astype(BF16)
    w["na_bias"] = _natten_bias_table(p["od_rpb"][0])
    w["od_w_out"] = p["od_w_out"][0].astype(BF16)
    w["od_ln1_g"], w["od_ln1_b"] = _row(p["od_ln1_g"][0]), _row(p["od_ln1_b"][0])
    w["router"] = _pad_lanes(p["od_router"][0])
    w["moe_wg"] = p["od_wg"][0].astype(BF16)
    w["moe_wu"] = p["od_wu"][0].astype(BF16)
    w["moe_wd"] = p["od_wd"][0].astype(BF16)
    w["od_ln2_g"], w["od_ln2_b"] = _row(p["od_ln2_g"][0]), _row(p["od_ln2_b"][0])
    return w


def _pick(n, pref):
    t = pref
    while n % t:
        t //= 2
    return t


def moe_res_ln(h3, w):
    n = h3.shape[0]
    gate = router(h3, w["router"], _pick(n, 512))
    return moe_dense_res_ln(h3, gate, w["moe_wg"], w["moe_wu"], w["moe_wd"],
                            w["od_ln2_g"], w["od_ln2_b"], tm=_pick(n, 1024), tf=512)


def trunk(x, w):
    bsz, t, d = x.shape
    n = bsz * t
    tm = _pick(n, 512)
    nc = t // CHUNK
    xf = x.reshape(n, d)

    h0, z, xbc, dt_raw, u = ln_inproj(xf, w["ln_in_g"], w["ln_in_b"], w["wz"], w["wx"], w["wdt"], w["wu"], tm)
    xbc_act = conv_silu(xbc.reshape(bsz, t, CONV_DIM), w["conv_w"], w["conv_b"],
                        tt=_pick(t, 512), tc=1024)
    pcol, q = ssd_prep(dt_raw.reshape(bsz, t, LANES), w["dt_bias"], w["a_log"], cpp=_pick(nc, 4))
    cps = _pick(nc, 4)
    y_b = ssd_scan(xbc_act, pcol, q, rev=True, cps=cps)
    y_ssd = ssd_scan(xbc_act, pcol, q, rev=False, cps=cps,
                     fused=(y_b, z.reshape(bsz, t, D_INNER), w["d_skip"], w["gnorm"]))
    y_fn = fnet_mix(u.reshape(bsz, t, FNET_WIDTH), nb=_pick(FNET_N2, 4))
    h1 = proj_res_ln([y_ssd.reshape(n, D_INNER), y_fn.reshape(n, FNET_WIDTH)],
                     [w["wo_ssd"], w["wo_fn"]], h0, w["ev_ln1_g"], w["ev_ln1_b"], tm)
    h2 = ffn_res_ln(h1, w["ffn_wg"], w["ffn_wu"], w["ffn_wd"], w["ev_ln2_g"], w["ev_ln2_b"], tm)

    qh, kh, vh = qkv_proj(h2, w["w_qkv"], tm)
    att = natten(qh.reshape(bsz, t, d), kh.reshape(bsz, t, d), vh.reshape(bsz, t, d), w["na_bias"])
    h3 = proj_res_ln([att.reshape(n, d)], [w["od_w_out"]], h2, w["od_ln1_g"], w["od_ln1_b"], tm)
    return moe_res_ln(h3, w).reshape(bsz, t, d)


def kernel(x_prompt, x_sample, ln_in_g, ln_in_b, ev_w_in, ev_conv_w, ev_conv_b, ev_dt_bias_f, ev_dt_bias_b,
           ev_a_log_f, ev_a_log_b, ev_d_skip, ev_gnorm_w, ev_w_out, ev_ln1_g, ev_ln1_b, ev_ffn_wg, ev_ffn_wu,
           ev_ffn_wd, ev_ln2_g, ev_ln2_b, od_w_qkv, od_rpb, od_w_out, od_ln1_g, od_ln1_b, od_router, od_wg,
           od_wu, od_wd, od_ln2_g, od_ln2_b):
    params = dict(ln_in_g=ln_in_g, ln_in_b=ln_in_b, ev_w_in=ev_w_in, ev_conv_w=ev_conv_w, ev_conv_b=ev_conv_b,
                  ev_dt_bias_f=ev_dt_bias_f, ev_dt_bias_b=ev_dt_bias_b, ev_a_log_f=ev_a_log_f,
                  ev_a_log_b=ev_a_log_b, ev_d_skip=ev_d_skip, ev_gnorm_w=ev_gnorm_w, ev_w_out=ev_w_out,
                  ev_ln1_g=ev_ln1_g, ev_ln1_b=ev_ln1_b, ev_ffn_wg=ev_ffn_wg, ev_ffn_wu=ev_ffn_wu,
                  ev_ffn_wd=ev_ffn_wd, ev_ln2_g=ev_ln2_g, ev_ln2_b=ev_ln2_b, od_w_qkv=od_w_qkv, od_rpb=od_rpb,
                  od_w_out=od_w_out, od_ln1_g=od_ln1_g, od_ln1_b=od_ln1_b, od_router=od_router, od_wg=od_wg,
                  od_wu=od_wu, od_wd=od_wd, od_ln2_g=od_ln2_g, od_ln2_b=od_ln2_b)
    w = prepare_weights(params)
    return (trunk(x_prompt, w), trunk(x_sample, w))
```

```python
import functools
import math

import numpy as np
import jax
import jax.numpy as jnp
from jax import lax
from jax.experimental import pallas as pl
from jax.experimental.pallas import tpu as pltpu
from jax.experimental.pallas import tpu_sc as plsc

F32 = jnp.float32
BF16 = jnp.bfloat16

D_MODEL = 1024
GRID_W = 64
D_INNER = 2048
SSD_HEADS = 32
SSD_GROUPS = 4
HEADS_PER_GROUP = SSD_HEADS // SSD_GROUPS
GROUP_DIM = D_INNER // SSD_GROUPS
D_STATE = 128
D_CONV = 5
CHUNK = 128
CONV_DIM = D_INNER + 2 * SSD_GROUPS * D_STATE
FNET_GROUPS = 4
FNET_GROUP_DIM = 256
FNET_WIDTH = 1024
FNET_N2 = 128
NA_HEADS = 16
NA_HEAD_DIM = 64
NA_WIN_H = 8
NA_WIN_W = 16
D_FF_DENSE = 2816
N_EXPERTS = 8
D_FF_EXPERT = 3584
LN_EPS = 1e-5
RMS_EPS = 1e-5
DEPTH = 2
ALPHA = (2 * DEPTH) ** 0.25
LANES = 128
HALO = 16
NEG_BIG = -1e30
VMEM_LIMIT = 56 * 1024 * 1024
PACK_W = D_MODEL // 4
SC_WINDOW = 128
MOE_ROW_TILE = 1024


def _cparams(sem):
    return pltpu.CompilerParams(dimension_semantics=sem, vmem_limit_bytes=VMEM_LIMIT)


def _const_spec(shape):
    nd = len(shape)
    return pl.BlockSpec(shape, lambda *_: (0,) * nd)


def _dot(a, b):
    return jnp.dot(a, b, preferred_element_type=F32)


def _layer_norm(xf, g, b):
    mu = jnp.mean(xf, axis=-1, keepdims=True)
    xc = xf - mu
    var = jnp.mean(xc * xc, axis=-1, keepdims=True)
    return xc * lax.rsqrt(var + LN_EPS) * g + b


def _silu(x):
    return x * jax.nn.sigmoid(x)


def _ln_inproj_kernel(x_ref, g_ref, b_ref, wz_ref, wx_ref, wdt_ref, wu_ref,
                      h_ref, z_ref, xbc_ref, dt_ref, u_ref):
    h = _layer_norm(x_ref[...], g_ref[...], b_ref[...])
    h_ref[...] = h
    hb = h.astype(BF16)
    for j in range(0, D_INNER, 512):
        z_ref[:, j:j + 512] = _dot(hb, wz_ref[:, j:j + 512]).astype(BF16)
    for j in range(0, CONV_DIM, 512):
        xbc_ref[:, j:j + 512] = _dot(hb, wx_ref[:, j:j + 512]).astype(BF16)
    dt_ref[...] = _dot(hb, wdt_ref[...])
    for j in range(0, FNET_WIDTH, 512):
        u_ref[:, j:j + 512] = _dot(hb, wu_ref[:, j:j + 512]).astype(BF16)


def ln_inproj(x, g, b, wz, wx, wdt, wu, tm):
    n = x.shape[0]
    row = lambda w: pl.BlockSpec((tm, w), lambda i: (i, 0))
    return pl.pallas_call(
        _ln_inproj_kernel,
        grid=(n // tm,),
        in_specs=[row(D_MODEL), _const_spec((1, D_MODEL)), _const_spec((1, D_MODEL)),
                  _const_spec(wz.shape), _const_spec(wx.shape), _const_spec(wdt.shape),
                  _const_spec(wu.shape)],
        out_specs=[row(D_MODEL), row(D_INNER), row(CONV_DIM), row(LANES), row(FNET_WIDTH)],
        out_shape=[jax.ShapeDtypeStruct((n, D_MODEL), F32),
                   jax.ShapeDtypeStruct((n, D_INNER), BF16),
                   jax.ShapeDtypeStruct((n, CONV_DIM), BF16),
                   jax.ShapeDtypeStruct((n, LANES), F32),
                   jax.ShapeDtypeStruct((n, FNET_WIDTH), BF16)],
        compiler_params=_cparams(("parallel",)),
        name="ln_inproj",
    )(x, g, b, wz, wx, wdt, wu)


def _conv_silu_kernel(prev_ref, main_ref, next_ref, w_ref, b_ref, o_ref, scr, *, tt, nt):
    i = pl.program_id(1)
    prev = prev_ref[0].astype(F32)
    nxt = next_ref[0].astype(F32)
    scr[0:HALO, :] = jnp.where(i == 0, 0.0, prev)
    scr[HALO:HALO + tt, :] = main_ref[0].astype(F32)
    scr[HALO + tt:2 * HALO + tt, :] = jnp.where(i == nt - 1, 0.0, nxt)
    acc = b_ref[...] + w_ref[0:1, :] * scr[HALO - 2:HALO - 2 + tt, :]
    for k in range(1, D_CONV):
        acc = acc + w_ref[k:k + 1, :] * scr[HALO - 2 + k:HALO - 2 + k + tt, :]
    o_ref[0] = _silu(acc).astype(BF16)


def conv_silu(xbc, w, b, tt, tc):
    bsz, t, c = xbc.shape
    nt = t // tt
    hb = tt // HALO
    kern = functools.partial(_conv_silu_kernel, tt=tt, nt=nt)
    return pl.pallas_call(
        kern,
        grid=(bsz, nt, c // tc),
        in_specs=[
            pl.BlockSpec((1, HALO, tc), lambda bi, i, ci: (bi, jnp.maximum(i * hb - 1, 0), ci)),
            pl.BlockSpec((1, tt, tc), lambda bi, i, ci: (bi, i, ci)),
            pl.BlockSpec((1, HALO, tc), lambda bi, i, ci: (bi, jnp.minimum((i + 1) * hb, t // HALO - 1), ci)),
            pl.BlockSpec((D_CONV, tc), lambda bi, i, ci: (0, ci)),
            pl.BlockSpec((1, tc), lambda bi, i, ci: (0, ci)),
        ],
        out_specs=pl.BlockSpec((1, tt, tc), lambda bi, i, ci: (bi, i, ci)),
        out_shape=jax.ShapeDtypeStruct((bsz, t, c), BF16),
        scratch_shapes=[pltpu.VMEM((tt + 2 * HALO, tc), F32)],
        compiler_params=_cparams(("parallel", "parallel", "parallel")),
        name="conv_silu",
    )(xbc, xbc, xbc, w, b)


def _split3(x):
    hi = x.astype(BF16)
    r1 = x - hi.astype(F32)
    mid = r1.astype(BF16)
    lo = (r1 - mid.astype(F32)).astype(BF16)
    return hi, mid, lo


def _ssd_prep_kernel(dt_ref, bias_ref, alog_ref, tri_ref, trit_ref, sel_ref, pcol_ref, q_ref, *, cpp):
    lane = lax.broadcasted_iota(jnp.int32, (CHUNK, LANES), 1)
    fwd = lane < SSD_HEADS
    a_coef = -jnp.exp(alog_ref[...])
    for c in range(cpp):
        rows = slice(c * CHUNK, (c + 1) * CHUNK)
        raw = dt_ref[0, rows, :] + bias_ref[...]
        dt = jnp.maximum(raw, 0.0) + jnp.log1p(jnp.exp(-jnp.abs(raw)))
        a = dt * a_coef
        cs_f = jnp.dot(tri_ref[...], a, precision=lax.Precision.HIGHEST, preferred_element_type=F32)
        cs_b = jnp.dot(trit_ref[...], a, precision=lax.Precision.HIGHEST, preferred_element_type=F32)
        cs = jnp.where(fwd, cs_f, cs_b)
        end = jnp.where(fwd[0:1], cs[CHUNK - 1:CHUNK, :], cs[0:1, :])
        wend = jnp.exp(end - cs) * dt
        hi, mid, lo = _split3(cs)
        src = jnp.concatenate([hi, mid, lo, wend.astype(BF16)], axis=1)
        pc = _dot(src, sel_ref[...]).astype(BF16)
        pcol_ref[0, 0, rows, :] = pc[:, :SSD_GROUPS * LANES]
        pcol_ref[1, 0, rows, :] = pc[:, SSD_GROUPS * LANES:]
        cs_t = cs.T
        dt_t = dt.T
        for d in range(2):
            for g in range(SSD_GROUPS):
                r0 = d * SSD_HEADS + g * HEADS_PER_GROUP
                q_ref[d, 0, g, c, 0:HEADS_PER_GROUP, :] = cs_t[r0:r0 + HEADS_PER_GROUP, :]
                q_ref[d, 0, g, c, HEADS_PER_GROUP:2 * HEADS_PER_GROUP, :] = dt_t[r0:r0 + HEADS_PER_GROUP, :]


def _prep_select_matrix():
    sel = np.zeros((4 * LANES, 2 * SSD_GROUPS * LANES), np.float32)
    for d in range(2):
        for g in range(SSD_GROUPS):
            for q in range(4):
                for i in range(HEADS_PER_GROUP):
                    src = q * LANES + d * SSD_HEADS + g * HEADS_PER_GROUP + i
                    dst = d * SSD_GROUPS * LANES + g * LANES + q * HEADS_PER_GROUP + i
                    sel[src, dst] = 1.0
    return sel


def ssd_prep(dt_raw, bias_row, alog_row, cpp):
    bsz, t, _ = dt_raw.shape
    nc = t // CHUNK
    tri = np.tril(np.ones((CHUNK, CHUNK), np.float32))
    kern = functools.partial(_ssd_prep_kernel, cpp=cpp)
    return pl.pallas_call(
        kern,
        grid=(bsz, nc // cpp),
        in_specs=[pl.BlockSpec((1, cpp * CHUNK, LANES), lambda b, j: (b, j, 0)),
                  _const_spec((1, LANES)), _const_spec((1, LANES)),
                  _const_spec((CHUNK, CHUNK)), _const_spec((CHUNK, CHUNK)),
                  _const_spec((4 * LANES, 2 * SSD_GROUPS * LANES))],
        out_specs=[pl.BlockSpec((2, 1, cpp * CHUNK, SSD_GROUPS * LANES), lambda b, j: (0, b, j, 0)),
                   pl.BlockSpec((2, 1, SSD_GROUPS, cpp, 2 * HEADS_PER_GROUP, CHUNK),
                                lambda b, j: (0, b, 0, j, 0, 0))],
        out_shape=[jax.ShapeDtypeStruct((2, bsz, t, SSD_GROUPS * LANES), BF16),
                   jax.ShapeDtypeStruct((2, bsz, SSD_GROUPS, nc, 2 * HEADS_PER_GROUP, CHUNK), F32)],
        compiler_params=_cparams(("parallel", "parallel")),
        name="ssd_prep",
    )(dt_raw, bias_row, alog_row, jnp.asarray(tri), jnp.asarray(tri.T),
      jnp.asarray(_prep_select_matrix(), BF16))


def _expand_matrix():
    e = np.zeros((LANES, HEADS_PER_GROUP * LANES + GROUP_DIM), np.float32)
    for q in range(3):
        for h in range(HEADS_PER_GROUP):
            e[q * HEADS_PER_GROUP + h, h * LANES:(h + 1) * LANES] = 1.0
    for h in range(HEADS_PER_GROUP):
        e[3 * HEADS_PER_GROUP + h,
          HEADS_PER_GROUP * LANES + h * 64:HEADS_PER_GROUP * LANES + (h + 1) * 64] = 1.0
    return e


def _ssd_kernel(*refs, rev, cps, fuse):
    if fuse:
        xs_ref, b_ref, c_ref, p_ref, q_ref, e_ref, yb_ref, z_ref, dsk_ref, gn_ref, o_ref, st_ref = refs
    else:
        xs_ref, b_ref, c_ref, p_ref, q_ref, e_ref, o_ref, st_ref = refs

    @pl.when(pl.program_id(2) == 0)
    def _():
        st_ref[...] = jnp.zeros_like(st_ref)

    row = lax.broadcasted_iota(jnp.int32, (CHUNK, CHUNK), 0)
    col = lax.broadcasted_iota(jnp.int32, (CHUNK, CHUNK), 1)
    mask = (col >= row) if rev else (col <= row)
    lo_half = lax.broadcasted_iota(jnp.int32, (CHUNK, LANES), 1) < 64
    nb = HEADS_PER_GROUP * LANES
    end = 0 if rev else CHUNK - 1
    order = range(cps - 1, -1, -1) if rev else range(cps)
    for c in order:
        rows = slice(c * CHUNK, (c + 1) * CHUNK)
        xf = xs_ref[0, rows, :].astype(F32)
        bm = b_ref[0, rows, :]
        cm = c_ref[0, rows, :]
        ex = _dot(p_ref[0, 0, rows, :], e_ref[...])
        qv = q_ref[0, 0, 0, c]
        cb = lax.dot_general(cm, bm, (((1,), (1,)), ((), ())), preferred_element_type=F32)
        cs_parts = []
        y_parts = []
        for j in range(HEADS_PER_GROUP // 2):
            h1, h2 = 2 * j, 2 * j + 1
            c1 = ex[:, h1 * LANES:(h1 + 1) * LANES]
            c2 = ex[:, h2 * LANES:(h2 + 1) * LANES]
            cs_parts.append(jnp.where(lo_half, c1, c2))
            w1 = jnp.exp(jnp.where(mask, c1 - qv[h1:h1 + 1, :], -jnp.inf)) * (
                cb * qv[HEADS_PER_GROUP + h1:HEADS_PER_GROUP + h1 + 1, :])
            w2 = jnp.exp(jnp.where(mask, c2 - qv[h2:h2 + 1, :], -jnp.inf)) * (
                cb * qv[HEADS_PER_GROUP + h2:HEADS_PER_GROUP + h2 + 1, :])
            wp = jnp.concatenate([w1, w2], axis=1).astype(BF16)
            xp = xf[:, j * LANES:(j + 1) * LANES]
            rhs = jnp.concatenate([jnp.where(lo_half, xp, 0.0), jnp.where(lo_half, 0.0, xp)],
                                  axis=0).astype(BF16)
            y_parts.append(_dot(wp, rhs))
        ydiag = jnp.concatenate(y_parts, axis=1)
        expcs = jnp.exp(jnp.concatenate(cs_parts, axis=1))
        decay = expcs[end:end + 1, :]
        st = st_ref[...]
        yoff = _dot(cm, st.astype(BF16)) * expcs
        xsw = (xf * ex[:, nb:nb + GROUP_DIM]).astype(BF16)
        st_ref[...] = st * decay + lax.dot_general(bm, xsw, (((0,), (0,)), ((), ())),
                                                   preferred_element_type=F32)
        y = ydiag + yoff
        if fuse:
            y = y + yb_ref[0, rows, :] + dsk_ref[...] * xf
            y = y * _silu(z_ref[0, rows, :].astype(F32))
            ms = jnp.mean(y * y, axis=-1, keepdims=True)
            o_ref[0, rows, :] = (y * lax.rsqrt(ms + RMS_EPS) * gn_ref[...]).astype(BF16)
        else:
            o_ref[0, rows, :] = y


def ssd_scan(xbc_act, pcol, q, rev, cps, fused=None):
    bsz, t, _ = xbc_act.shape
    ncb = t // (cps * CHUNK)
    r = cps * CHUNK
    d = 1 if rev else 0
    cidx = (lambda j: ncb - 1 - j) if rev else (lambda j: j)
    e = jnp.asarray(_expand_matrix(), BF16)
    b0 = D_INNER // LANES
    c0 = b0 + SSD_GROUPS
    in_specs = [
        pl.BlockSpec((1, r, GROUP_DIM), lambda b, g, j: (b, cidx(j), g)),
        pl.BlockSpec((1, r, LANES), lambda b, g, j: (b, cidx(j), b0 + g)),
        pl.BlockSpec((1, r, LANES), lambda b, g, j: (b, cidx(j), c0 + g)),
        pl.BlockSpec((1, 1, r, LANES), lambda b, g, j: (d, b, cidx(j), g)),
        pl.BlockSpec((1, 1, 1, cps, 2 * HEADS_PER_GROUP, CHUNK), lambda b, g, j: (d, b, g, cidx(j), 0, 0)),
        _const_spec(e.shape),
    ]
    args = [xbc_act, xbc_act, xbc_act, pcol, q, e]
    if fused is not None:
        yb, z, dsk, gn = fused
        in_specs += [
            pl.BlockSpec((1, r, GROUP_DIM), lambda b, g, j: (b, cidx(j), g)),
            pl.BlockSpec((1, r, GROUP_DIM), lambda b, g, j: (b, cidx(j), g)),
            pl.BlockSpec((1, GROUP_DIM), lambda b, g, j: (0, g)),
            pl.BlockSpec((1, GROUP_DIM), lambda b, g, j: (0, g)),
        ]
        args += [yb, z, dsk, gn]
        out_dtype = BF16
    else:
        out_dtype = F32
    kern = functools.partial(_ssd_kernel, rev=rev, cps=cps, fuse=fused is not None)
    return pl.pallas_call(
        kern,
        grid=(bsz, SSD_GROUPS, ncb),
        in_specs=in_specs,
        out_specs=pl.BlockSpec((1, r, GROUP_DIM), lambda b, g, j: (b, cidx(j), g)),
        out_shape=jax.ShapeDtypeStruct((bsz, t, D_INNER), out_dtype),
        scratch_shapes=[pltpu.VMEM((D_STATE, GROUP_DIM), F32)],
        compiler_params=_cparams(("parallel", "parallel", "arbitrary")),
        name="ssd_bwd" if rev else "ssd_fwd",
    )(*args)


def _dft_cos_sin(n):
    k = np.arange(n, dtype=np.float64)
    ang = 2.0 * np.pi * np.outer(k, k) / n
    return np.cos(ang), np.sin(ang)


def _fnet_s1_kernel(u_ref, f1_ref, cos_ref, sin_ref, o_ref, *, n1, nb):
    c = FNET_WIDTH
    for q in range(nb):
        a = _dot(f1_ref[...], u_ref[0, :, q * c:(q + 1) * c])
        ar = a[:n1]
        ai = a[n1:]
        ct = jnp.concatenate([cos_ref[q]] * (c // LANES), axis=1)
        st = jnp.concatenate([sin_ref[q]] * (c // LANES), axis=1)
        o_ref[0, 0:n1, q * c:(q + 1) * c] = (ar * ct + ai * st).astype(BF16)
        o_ref[0, n1:2 * n1, q * c:(q + 1) * c] = (ai * ct - ar * st).astype(BF16)


def _fnet_s2_kernel(br_ref, bi_ref, f2_ref, wc_ref, o_ref):
    n2 = FNET_N2
    rhs = jnp.concatenate([br_ref[0, 0], bi_ref[0, 0]], axis=0)
    g = _dot(f2_ref[...], rhs).astype(BF16)
    for grp in range(FNET_GROUPS):
        cols = slice(grp * FNET_GROUP_DIM, (grp + 1) * FNET_GROUP_DIM)
        lhs = jnp.concatenate([g[:n2, cols], g[n2:, cols]], axis=1)
        o_ref[0, :, cols] = _dot(lhs, wc_ref[...]).astype(BF16)


def fnet_mix(u, nb=4):
    bsz, t, c = u.shape
    n2 = FNET_N2
    n1 = t // n2
    c1, s1 = _dft_cos_sin(n1)
    c2, s2 = _dft_cos_sin(n2)
    cc, sc = _dft_cos_sin(FNET_GROUP_DIM)
    f1 = jnp.asarray(np.concatenate([c1, -s1], axis=0), BF16)
    f2 = jnp.asarray(np.block([[c2, s2], [-s2, c2]]), BF16)
    scale = 1.0 / math.sqrt(t * FNET_GROUP_DIM)
    wc = jnp.asarray(np.concatenate([cc, sc], axis=0) * scale, BF16)
    ang = 2.0 * np.pi * np.outer(np.arange(n2), np.arange(n1)) / t
    tw_cos = jnp.asarray(np.repeat(np.cos(ang)[:, :, None], LANES, axis=2), F32)
    tw_sin = jnp.asarray(np.repeat(np.sin(ang)[:, :, None], LANES, axis=2), F32)

    u2 = u.reshape(bsz, n1, n2 * c)
    s1_out = pl.pallas_call(
        functools.partial(_fnet_s1_kernel, n1=n1, nb=nb),
        grid=(bsz, n2 // nb),
        in_specs=[pl.BlockSpec((1, n1, nb * c), lambda b, j: (b, 0, j)),
                  _const_spec((2 * n1, n1)),
                  pl.BlockSpec((nb, n1, LANES), lambda b, j: (j, 0, 0)),
                  pl.BlockSpec((nb, n1, LANES), lambda b, j: (j, 0, 0))],
        out_specs=pl.BlockSpec((1, 2 * n1, nb * c), lambda b, j: (b, 0, j)),
        out_shape=jax.ShapeDtypeStruct((bsz, 2 * n1, n2 * c), BF16),
        compiler_params=_cparams(("parallel", "parallel")),
        name="fnet_stage1",
    )(u2, f1, tw_cos, tw_sin)

    bv = s1_out.reshape(bsz, 2 * n1, n2, c)
    y = pl.pallas_call(
        _fnet_s2_kernel,
        grid=(bsz, n1),
        in_specs=[pl.BlockSpec((1, 1, n2, c), lambda b, k: (b, k, 0, 0)),
                  pl.BlockSpec((1, 1, n2, c), lambda b, k: (b, n1 + k, 0, 0)),
                  _const_spec((2 * n2, 2 * n2)),
                  _const_spec((2 * FNET_GROUP_DIM, FNET_GROUP_DIM))],
        out_specs=pl.BlockSpec((1, n2, c), lambda b, k: (b, 0, k)),
        out_shape=jax.ShapeDtypeStruct((bsz, n2, n1 * c), BF16),
        compiler_params=_cparams(("parallel", "parallel")),
        name="fnet_stage2",
    )(bv, bv, f2, wc)
    return y.reshape(bsz, t, c)


def _pack_bf16_pair(a, b):
    lo = lax.bitcast_convert_type(a.astype(BF16).astype(F32), jnp.uint32)
    hi = lax.bitcast_convert_type(b.astype(BF16).astype(F32), jnp.uint32)
    word = lax.shift_right_logical(lo, jnp.uint32(16)) | hi
    return lax.bitcast_convert_type(word, jnp.int32)


def _unpack_bf16_pair(word):
    w = lax.bitcast_convert_type(word, jnp.uint32)
    a = lax.bitcast_convert_type(lax.shift_left(w, jnp.uint32(16)), F32)
    b = lax.bitcast_convert_type(w & jnp.uint32(0xFFFF0000), F32)
    return a, b


def _pack_row(x, a_ref, b_ref):
    a_ref[...] = _pack_bf16_pair(x[:, 0:PACK_W], x[:, PACK_W:2 * PACK_W])
    b_ref[...] = _pack_bf16_pair(x[:, 2 * PACK_W:3 * PACK_W], x[:, 3 * PACK_W:4 * PACK_W])


def _unpack_row(a_word, b_word):
    x0, x1 = _unpack_bf16_pair(a_word)
    x2, x3 = _unpack_bf16_pair(b_word)
    return jnp.concatenate([x0, x1, x2, x3], axis=1)


def _proj_res_ln_kernel(*refs, n_in, packed):
    xs = refs[:n_in]
    ws = refs[n_in:2 * n_in]
    h_ref, g_ref, b_ref, o_ref = refs[2 * n_in:2 * n_in + 4]
    acc = _dot(xs[0][...], ws[0][...])
    for x_ref, w_ref in zip(xs[1:], ws[1:]):
        acc = acc + _dot(x_ref[...], w_ref[...])
    out = _layer_norm(ALPHA * h_ref[...] + acc, g_ref[...], b_ref[...])
    o_ref[...] = out
    if packed:
        _pack_row(out, refs[-2], refs[-1])


def proj_res_ln(xs, ws, h, g, b, tm, packed=False):
    n = h.shape[0]
    n_in = len(xs)
    in_specs = [pl.BlockSpec((tm, x.shape[1]), lambda i: (i, 0)) for x in xs]
    in_specs += [_const_spec(w.shape) for w in ws]
    in_specs += [pl.BlockSpec((tm, D_MODEL), lambda i: (i, 0)),
                 _const_spec((1, D_MODEL)), _const_spec((1, D_MODEL))]
    out_specs = [pl.BlockSpec((tm, D_MODEL), lambda i: (i, 0))]
    out_shape = [jax.ShapeDtypeStruct((n, D_MODEL), F32)]
    if packed:
        out_specs += [pl.BlockSpec((tm, PACK_W), lambda i: (i, 0))] * 2
        out_shape += [jax.ShapeDtypeStruct((n, PACK_W), jnp.int32)] * 2
    res = pl.pallas_call(
        functools.partial(_proj_res_ln_kernel, n_in=n_in, packed=packed),
        grid=(n // tm,),
        in_specs=in_specs,
        out_specs=out_specs,
        out_shape=out_shape,
        compiler_params=_cparams(("parallel",)),
        name="proj_res_ln",
    )(*xs, *ws, h, g, b)
    return res if packed else res[0]


def _ffn_res_ln_kernel(h_ref, wg_ref, wu_ref, wd_ref, g_ref, b_ref, o_ref, act_ref, *, tf):
    h = h_ref[...]
    hb = h.astype(BF16)
    for j in range(0, D_FF_DENSE, tf):
        gate = _dot(hb, wg_ref[:, j:j + tf])
        up = _dot(hb, wu_ref[:, j:j + tf])
        act_ref[:, j:j + tf] = (_silu(gate) * up).astype(BF16)
    y = _dot(act_ref[...], wd_ref[...])
    o_ref[...] = _layer_norm(ALPHA * h + y, g_ref[...], b_ref[...])


def ffn_res_ln(h, wg, wu, wd, g, b, tm, tf=256):
    n = h.shape[0]
    return pl.pallas_call(
        functools.partial(_ffn_res_ln_kernel, tf=tf),
        grid=(n // tm,),
        in_specs=[pl.BlockSpec((tm, D_MODEL), lambda i: (i, 0)),
                  _const_spec(wg.shape), _const_spec(wu.shape), _const_spec(wd.shape),
                  _const_spec((1, D_MODEL)), _const_spec((1, D_MODEL))],
        out_specs=pl.BlockSpec((tm, D_MODEL), lambda i: (i, 0)),
        out_shape=jax.ShapeDtypeStruct((n, D_MODEL), F32),
        scratch_shapes=[pltpu.VMEM((tm, D_FF_DENSE), BF16)],
        compiler_params=_cparams(("parallel",)),
        name="ffn_res_ln",
    )(h, wg, wu, wd, g, b)


def _qkv_kernel(h_ref, w_ref, q_ref, k_ref, v_ref):
    hb = h_ref[...].astype(BF16)
    d = D_MODEL
    scale = NA_HEAD_DIM ** -0.5
    for j in range(0, d, 512):
        q_ref[:, j:j + 512] = (_dot(hb, w_ref[:, j:j + 512]) * scale).astype(BF16)
        k_ref[:, j:j + 512] = _dot(hb, w_ref[:, d + j:d + j + 512]).astype(BF16)
        v_ref[:, j:j + 512] = _dot(hb, w_ref[:, 2 * d + j:2 * d + j + 512]).astype(BF16)


def qkv_proj(h, w, tm):
    n = h.shape[0]
    row = pl.BlockSpec((tm, D_MODEL), lambda i: (i, 0))
    sds = jax.ShapeDtypeStruct((n, D_MODEL), BF16)
    return pl.pallas_call(
        _qkv_kernel,
        grid=(n // tm,),
        in_specs=[row, _const_spec(w.shape)],
        out_specs=[row, row, row],
        out_shape=[sds, sds, sds],
        compiler_params=_cparams(("parallel",)),
        name="qkv_proj",
    )(h, w)


def _natten_kernel(q_ref, k_ref, v_ref, bias_ref, o_ref, *, rows, hw, rq):
    nk = NA_WIN_H * GRID_W
    lo_half = lax.broadcasted_iota(jnp.int32, (GRID_W, LANES), 1) < NA_HEAD_DIM
    for r in range(rq):
        i = pl.program_id(2) * rq + r
        r0 = jnp.clip(i - NA_WIN_H // 2, 0, rows - NA_WIN_H)
        case = i - r0
        start = pl.multiple_of(r0 * GRID_W, GRID_W)
        qrows = slice(r * GRID_W, (r + 1) * GRID_W)
        for p in range(hw // LANES):
            cols = slice(p * LANES, (p + 1) * LANES)
            qp = q_ref[0, qrows, cols].astype(F32)
            qbd = jnp.concatenate([jnp.where(lo_half, qp, 0.0), jnp.where(lo_half, 0.0, qp)],
                                  axis=0).astype(BF16)
            s = lax.dot_general(qbd, k_ref[0, pl.ds(start, nk), cols], (((1,), (1,)), ((), ())),
                                preferred_element_type=F32)
            s = s + bias_ref[case, p]
            m = jnp.max(s, axis=-1, keepdims=True)
            e = jnp.exp(s - m)
            l = jnp.sum(e, axis=-1, keepdims=True)
            o2 = _dot(e.astype(BF16), v_ref[0, pl.ds(start, nk), cols]) * (1.0 / l)
            o_ref[0, qrows, cols] = jnp.where(lo_half, o2[:GRID_W], o2[GRID_W:]).astype(BF16)


def _natten_bias_table(rpb):
    j = np.arange(GRID_W)[:, None]
    kc = np.arange(GRID_W)[None, :]
    cstart = np.clip(j - NA_WIN_W // 2, 0, GRID_W - NA_WIN_W)
    valid = (kc >= cstart) & (kc < cstart + NA_WIN_W)
    coff = np.clip(kc - j + NA_WIN_W - 1, 0, 2 * NA_WIN_W - 2)
    dd = np.arange(NA_WIN_H)[:, None]
    a = np.arange(NA_WIN_H)[None, :]
    roff = a + (NA_WIN_H - 1) - dd
    rows = rpb.astype(F32)[:, roff, :]
    onehot = (coff[None, :, :] == np.arange(2 * NA_WIN_W - 1)[:, None, None]) & valid[None]
    t = jnp.einsum("hdac,cjk->dhjak", rows, jnp.asarray(onehot, F32),
                   precision=lax.Precision.HIGHEST)
    t = jnp.where(valid[None, None, :, None, :], t, NEG_BIG)
    return t.reshape(NA_WIN_H, NA_HEADS // 2, 2 * GRID_W, NA_WIN_H * GRID_W)


def natten(q, k, v, bias, hw=512, rq=4):
    bsz, t, d = q.shape
    rows = t // GRID_W
    nh = d // hw
    pp = hw // LANES
    resident = dict(pipeline_mode=pl.Buffered(1))
    return pl.pallas_call(
        functools.partial(_natten_kernel, rows=rows, hw=hw, rq=rq),
        grid=(bsz, nh, rows // rq),
        in_specs=[pl.BlockSpec((1, rq * GRID_W, hw), lambda b, hh, i: (b, i, hh)),
                  pl.BlockSpec((1, t, hw), lambda b, hh, i: (b, 0, hh), **resident),
                  pl.BlockSpec((1, t, hw), lambda b, hh, i: (b, 0, hh), **resident),
                  pl.BlockSpec((NA_WIN_H, pp, 2 * GRID_W, NA_WIN_H * GRID_W),
                               lambda b, hh, i: (0, hh, 0, 0), **resident)],
        out_specs=pl.BlockSpec((1, rq * GRID_W, hw), lambda b, hh, i: (b, i, hh)),
        out_shape=jax.ShapeDtypeStruct((bsz, t, d), BF16),
        compiler_params=_cparams(("parallel", "parallel", "arbitrary")),
        name="natten",
    )(q, k, v, bias)


def _lane_pick(x, lane, idx):
    return jnp.sum(jnp.where(lane == idx, x, 0.0), axis=-1, keepdims=True)


def _router_kernel(h_ref, w_ref, ltri_ref, meta_ref, gate_ref, cnt_ref, run_ref):
    @pl.when(pl.program_id(0) == 0)
    def _():
        run_ref[...] = jnp.zeros_like(run_ref)

    logits = jnp.dot(h_ref[...], w_ref[...], precision=lax.Precision.HIGHEST,
                     preferred_element_type=F32)
    lane = lax.broadcasted_iota(jnp.int32, logits.shape, 1)
    logits = jnp.where(lane < N_EXPERTS, logits, -jnp.inf)
    m1 = jnp.max(logits, axis=-1, keepdims=True)
    i1 = jnp.min(jnp.where(logits == m1, lane, LANES), axis=-1, keepdims=True)
    rest = jnp.where(lane == i1, -jnp.inf, logits)
    m2 = jnp.max(rest, axis=-1, keepdims=True)
    i2 = jnp.min(jnp.where(rest == m2, lane, LANES), axis=-1, keepdims=True)
    e2 = jnp.exp(m2 - m1)
    inv = 1.0 / (1.0 + e2)
    gate_ref[...] = jnp.where(lane == 0, inv, jnp.where(lane == 1, e2 * inv, 0.0))

    sel = jnp.where(lane == i1, 1.0, jnp.where(lane == i2, 1.0, 0.0))
    before = _dot(ltri_ref[...], sel.astype(BF16)) + run_ref[...]
    rank1 = _lane_pick(before, lane, i1)
    rank2 = _lane_pick(before, lane, i2)
    run_ref[...] += jnp.sum(sel, axis=0, keepdims=True)
    cnt_ref[...] = jnp.broadcast_to(run_ref[...], cnt_ref.shape)
    packed = jnp.where(lane == 0, rank1, jnp.where(lane == 1, rank2, jnp.where(
        lane == 2, i1.astype(F32), jnp.where(lane == 3, i2.astype(F32), 0.0))))
    meta_ref[...] = packed.T[0:8, :]


def router(h, w_pad, tm):
    n = h.shape[0]
    ltri = jnp.asarray(np.tril(np.ones((tm, tm), np.float32), -1), BF16)
    return pl.pallas_call(
        _router_kernel,
        grid=(n // tm,),
        in_specs=[pl.BlockSpec((tm, D_MODEL), lambda i: (i, 0)), _const_spec(w_pad.shape),
                  _const_spec((tm, tm))],
        out_specs=[pl.BlockSpec((8, tm), lambda i: (0, i)),
                   pl.BlockSpec((tm, LANES), lambda i: (i, 0)),
                   _const_spec((8, LANES))],
        out_shape=[jax.ShapeDtypeStruct((8, n), F32),
                   jax.ShapeDtypeStruct((n, LANES), F32),
                   jax.ShapeDtypeStruct((8, LANES), F32)],
        scratch_shapes=[pltpu.VMEM((1, LANES), F32)],
        compiler_params=_cparams(("arbitrary",)),
        name="router",
    )(h, w_pad, ltri)


def _moe_pos_kernel(off_ref, meta_ref, pos_ref):
    m = meta_ref[...]
    for k in range(2):
        rank = m[k:k + 1, :].astype(jnp.int32)
        expert = m[2 + k:3 + k, :].astype(jnp.int32)
        base = jnp.zeros_like(rank)
        for e in range(N_EXPERTS):
            base = jnp.where(expert == e, off_ref[e], base)
        pos_ref[k:k + 1, :] = rank + base


def moe_positions(meta, offsets, tn):
    n = meta.shape[1]
    return pl.pallas_call(
        _moe_pos_kernel,
        grid_spec=pltpu.PrefetchScalarGridSpec(
            num_scalar_prefetch=1,
            grid=(n // tn,),
            in_specs=[pl.BlockSpec((8, tn), lambda i, off: (0, i))],
            out_specs=pl.BlockSpec((2, tn), lambda i, off: (0, i)),
        ),
        out_shape=jax.ShapeDtypeStruct((2, n), jnp.int32),
        compiler_params=_cparams(("parallel",)),
        name="moe_positions",
    )(offsets, meta)


def _sc_mesh():
    return plsc.VectorSubcoreMesh(core_axis_name="core", subcore_axis_name="subcore")


def sc_dispatch_rows(x, pos, out_rows):
    n, dim = x.shape

    @pl.kernel(out_type=jax.ShapeDtypeStruct((out_rows, dim), x.dtype), mesh=_sc_mesh(), scratch_types=[])
    def k(x_hbm, i0_hbm, i1_hbm, o_hbm):
        def body(x_v, i0_v, i1_v):
            pltpu.sync_copy(x_v, o_hbm.at[i0_v.at[0]])
            pltpu.sync_copy(x_v, o_hbm.at[i1_v.at[0]])

        pltpu.emit_pipeline(
            body, grid=(n // SC_WINDOW,),
            in_specs=[pl.BlockSpec((SC_WINDOW, dim), index_map=lambda i: (i, 0)),
                      pl.BlockSpec((1, SC_WINDOW), index_map=lambda i: (0, i)),
                      pl.BlockSpec((1, SC_WINDOW), index_map=lambda i: (0, i))],
            out_specs=[],
            core_axis_name=("core", "subcore"), dimension_semantics=(pltpu.PARALLEL,),
        )(x_hbm, i0_hbm, i1_hbm)

    return k(x, pos[0:1], pos[1:2])


def sc_gather_rows(x, idx):
    ni = idx.shape[1]
    dim = x.shape[1]

    @pl.kernel(out_type=jax.ShapeDtypeStruct((ni, dim), x.dtype), mesh=_sc_mesh(), scratch_types=[])
    def k(x_hbm, i_hbm, o_hbm):
        def body(i_v, o_v):
            pltpu.sync_copy(x_hbm.at[i_v.at[0]], o_v)

        pltpu.emit_pipeline(
            body, grid=(ni // SC_WINDOW,),
            in_specs=[pl.BlockSpec((1, SC_WINDOW), index_map=lambda i: (0, i))],
            out_specs=[pl.BlockSpec((SC_WINDOW, dim), index_map=lambda i: (i, 0))],
            core_axis_name=("core", "subcore"), dimension_semantics=(pltpu.PARALLEL,),
        )(i_hbm, o_hbm)

    return k(x, idx)


def _moe_expert_kernel(te_ref, nu_ref, xa_ref, xb_ref, wg_ref, wu_ref, wd_ref, ya_ref, yb_ref,
                       x_scr, acc_ref, *, nf):
    i = pl.program_id(0)
    f = pl.program_id(1)

    @pl.when(i < nu_ref[0])
    def _():
        @pl.when(f == 0)
        def _():
            x_scr[...] = _unpack_row(xa_ref[...], xb_ref[...]).astype(BF16)
            acc_ref[...] = jnp.zeros_like(acc_ref)

        xb = x_scr[...]
        act = _silu(_dot(xb, wg_ref[0])) * _dot(xb, wu_ref[0])
        acc_ref[...] += _dot(act.astype(BF16), wd_ref[0])

        @pl.when(f == nf - 1)
        def _():
            _pack_row(acc_ref[...], ya_ref, yb_ref)


def moe_experts(xa, xb, tile_expert, n_used, wg, wu, wd, tf):
    rows = xa.shape[0]
    nt = rows // MOE_ROW_TILE
    nf = D_FF_EXPERT // tf
    xspec = pl.BlockSpec((MOE_ROW_TILE, PACK_W), lambda i, f, te, nu: (i, 0))
    sds = jax.ShapeDtypeStruct((rows, PACK_W), jnp.int32)
    return pl.pallas_call(
        functools.partial(_moe_expert_kernel, nf=nf),
        grid_spec=pltpu.PrefetchScalarGridSpec(
            num_scalar_prefetch=2,
            grid=(nt, nf),
            in_specs=[xspec, xspec,
                      pl.BlockSpec((1, D_MODEL, tf), lambda i, f, te, nu: (te[i], 0, f)),
                      pl.BlockSpec((1, D_MODEL, tf), lambda i, f, te, nu: (te[i], 0, f)),
                      pl.BlockSpec((1, tf, D_MODEL), lambda i, f, te, nu: (te[i], f, 0))],
            out_specs=[xspec, xspec],
            scratch_shapes=[pltpu.VMEM((MOE_ROW_TILE, D_MODEL), BF16),
                            pltpu.VMEM((MOE_ROW_TILE, D_MODEL), F32)],
        ),
        out_shape=[sds, sds],
        compiler_params=_cparams(("parallel", "arbitrary")),
        name="moe_experts",
    )(tile_expert, n_used, xa, xb, wg, wu, wd)


def _moe_combine_kernel(h_ref, gate_ref, ya0_ref, yb0_ref, ya1_ref, yb1_ref, g_ref, b_ref, o_ref):
    gate = gate_ref[...]
    lane = lax.broadcasted_iota(jnp.int32, gate.shape, 1)
    g0 = _lane_pick(gate, lane, 0)
    g1 = _lane_pick(gate, lane, 1)
    y = g0 * _unpack_row(ya0_ref[...], yb0_ref[...]) + g1 * _unpack_row(ya1_ref[...], yb1_ref[...])
    o_ref[...] = _layer_norm(ALPHA * h_ref[...] + y, g_ref[...], b_ref[...])


def moe_combine_res_ln(h, gate, ya, yb, g, b, tm):
    n = h.shape[0]
    nb = n // tm
    first = pl.BlockSpec((tm, PACK_W), lambda i: (i, 0))
    second = pl.BlockSpec((tm, PACK_W), lambda i: (nb + i, 0))
    return pl.pallas_call(
        _moe_combine_kernel,
        grid=(nb,),
        in_specs=[pl.BlockSpec((tm, D_MODEL), lambda i: (i, 0)),
                  pl.BlockSpec((tm, LANES), lambda i: (i, 0)),
                  first, first, second, second,
                  _const_spec((1, D_MODEL)), _const_spec((1, D_MODEL))],
        out_specs=pl.BlockSpec((tm, D_MODEL), lambda i: (i, 0)),
        out_shape=jax.ShapeDtypeStruct((n, D_MODEL), F32),
        compiler_params=_cparams(("parallel",)),
        name="moe_combine_res_ln",
    )(h, gate, ya, yb, ya, yb, g, b)


def _row(v):
    return v.reshape(1, -1).astype(F32)


def _pad_lanes(v, fill=0.0):
    v = v.astype(F32)
    pad = LANES - v.shape[-1]
    return jnp.concatenate([v, jnp.full(v.shape[:-1] + (pad,), fill, F32)], axis=-1)


def prepare_weights(p):
    w = {}
    w_in = p["ev_w_in"][0]
    o1 = D_INNER
    o2 = o1 + CONV_DIM
    o3 = o2 + 2 * SSD_HEADS
    w["wz"] = w_in[:, :o1].astype(BF16)
    w["wx"] = w_in[:, o1:o2].astype(BF16)
    w["wdt"] = _pad_lanes(w_in[:, o2:o3]).astype(BF16)
    w["wu"] = w_in[:, o3:].astype(BF16)
    w["ln_in_g"], w["ln_in_b"] = _row(p["ln_in_g"]), _row(p["ln_in_b"])
    w["conv_w"] = p["ev_conv_w"][0].astype(F32)
    w["conv_b"] = _row(p["ev_conv_b"][0])
    w["dt_bias"] = _pad_lanes(jnp.concatenate([p["ev_dt_bias_f"][0], p["ev_dt_bias_b"][0]])[None, :])
    w["a_log"] = _pad_lanes(jnp.concatenate([p["ev_a_log_f"][0], p["ev_a_log_b"][0]])[None, :])
    w["d_skip"] = jnp.repeat(p["ev_d_skip"][0].astype(F32), D_INNER // SSD_HEADS)[None, :]
    w["gnorm"] = _row(p["ev_gnorm_w"][0])
    w_out = p["ev_w_out"][0]
    w["wo_ssd"] = w_out[:D_INNER].astype(BF16)
    w["wo_fn"] = w_out[D_INNER:].astype(BF16)
    w["ev_ln1_g"], w["ev_ln1_b"] = _row(p["ev_ln1_g"][0]), _row(p["ev_ln1_b"][0])
    w["ffn_wg"] = p["ev_ffn_wg"][0].astype(BF16)
    w["ffn_wu"] = p["ev_ffn_wu"][0].astype(BF16)
    w["ffn_wd"] = p["ev_ffn_wd"][0].astype(BF16)
    w["ev_ln2_g"], w["ev_ln2_b"] = _row(p["ev_ln2_g"][0]), _row(p["ev_ln2_b"][0])
    w["w_qkv"] = p["od_w_qkv"][0].astype(BF16)
    w["na_bias"] = _natten_bias_table(p["od_rpb"][0])
    w["od_w_out"] = p["od_w_out"][0].astype(BF16)
    w["od_ln1_g"], w["od_ln1_b"] = _row(p["od_ln1_g"][0]), _row(p["od_ln1_b"][0])
    w["router"] = _pad_lanes(p["od_router"][0])
    w["moe_wg"] = p["od_wg"][0].astype(BF16)
    w["moe_wu"] = p["od_wu"][0].astype(BF16)
    w["moe_wd"] = p["od_wd"][0].astype(BF16)
    w["od_ln2_g"], w["od_ln2_b"] = _row(p["od_ln2_g"][0]), _row(p["od_ln2_b"][0])
    return w


def _pick(n, pref):
    t = pref
    while n % t:
        t //= 2
    return t


def moe_plan(counts, n_rows):
    cnt = counts[0, :N_EXPERTS]
    padded = jnp.ceil(cnt / MOE_ROW_TILE) * MOE_ROW_TILE
    ends = jnp.cumsum(padded)
    offsets = (ends - padded).astype(jnp.int32)
    n_used = (ends[-1:] / MOE_ROW_TILE).astype(jnp.int32)
    starts = jnp.arange(n_rows // MOE_ROW_TILE, dtype=F32) * MOE_ROW_TILE
    tile_expert = jnp.minimum(jnp.sum(starts[:, None] >= ends[None, :], axis=1), N_EXPERTS - 1)
    return offsets, tile_expert.astype(jnp.int32), n_used


def moe_res_ln(h3, xa, xb, w):
    n = h3.shape[0]
    tm = _pick(n, 512)
    n_rows = 2 * n + N_EXPERTS * MOE_ROW_TILE
    meta, gate, counts = router(h3, w["router"], tm)
    offsets, tile_expert, n_used = moe_plan(counts, n_rows)
    pos = moe_positions(meta, offsets, _pick(n, 2048))
    xs_a = sc_dispatch_rows(xa, pos, n_rows)
    xs_b = sc_dispatch_rows(xb, pos, n_rows)
    ys_a, ys_b = moe_experts(xs_a, xs_b, tile_expert, n_used, w["moe_wg"], w["moe_wu"], w["moe_wd"], tf=512)
    flat = pos.reshape(1, 2 * n)
    ya = sc_gather_rows(ys_a, flat)
    yb = sc_gather_rows(ys_b, flat)
    return moe_combine_res_ln(h3, gate, ya, yb, w["od_ln2_g"], w["od_ln2_b"], tm)


def trunk(x, w):
    bsz, t, d = x.shape
    n = bsz * t
    tm = _pick(n, 512)
    nc = t // CHUNK
    xf = x.reshape(n, d)

    h0, z, xbc, dt_raw, u = ln_inproj(xf, w["ln_in_g"], w["ln_in_b"], w["wz"], w["wx"], w["wdt"], w["wu"], tm)
    xbc_act = conv_silu(xbc.reshape(bsz, t, CONV_DIM), w["conv_w"], w["conv_b"],
                        tt=_pick(t, 512), tc=1024)
    pcol, q = ssd_prep(dt_raw.reshape(bsz, t, LANES), w["dt_bias"], w["a_log"], cpp=_pick(nc, 4))
    cps = _pick(nc, 4)
    y_b = ssd_scan(xbc_act, pcol, q, rev=True, cps=cps)
    y_ssd = ssd_scan(xbc_act, pcol, q, rev=False, cps=cps,
                     fused=(y_b, z.reshape(bsz, t, D_INNER), w["d_skip"], w["gnorm"]))
    y_fn = fnet_mix(u.reshape(bsz, t, FNET_WIDTH), nb=_pick(FNET_N2, 4))
    h1 = proj_res_ln([y_ssd.reshape(n, D_INNER), y_fn.reshape(n, FNET_WIDTH)],
                     [w["wo_ssd"], w["wo_fn"]], h0, w["ev_ln1_g"], w["ev_ln1_b"], tm)
    h2 = ffn_res_ln(h1, w["ffn_wg"], w["ffn_wu"], w["ffn_wd"], w["ev_ln2_g"], w["ev_ln2_b"], tm)

    qh, kh, vh = qkv_proj(h2, w["w_qkv"], tm)
    att = natten(qh.reshape(bsz, t, d), kh.reshape(bsz, t, d), vh.reshape(bsz, t, d), w["na_bias"])
    h3, xa, xb = proj_res_ln([att.reshape(n, d)], [w["od_w_out"]], h2, w["od_ln1_g"], w["od_ln1_b"], tm,
                             packed=True)
    return moe_res_ln(h3, xa, xb, w).reshape(bsz, t, d)


def kernel(x_prompt, x_sample, ln_in_g, ln_in_b, ev_w_in, ev_conv_w, ev_conv_b, ev_dt_bias_f, ev_dt_bias_b,
           ev_a_log_f, ev_a_log_b, ev_d_skip, ev_gnorm_w, ev_w_out, ev_ln1_g, ev_ln1_b, ev_ffn_wg, ev_ffn_wu,
           ev_ffn_wd, ev_ln2_g, ev_ln2_b, od_w_qkv, od_rpb, od_w_out, od_ln1_g, od_ln1_b, od_router, od_wg,
           od_wu, od_wd, od_ln2_g, od_ln2_b):
    params = dict(ln_in_g=ln_in_g, ln_in_b=ln_in_b, ev_w_in=ev_w_in, ev_conv_w=ev_conv_w, ev_conv_b=ev_conv_b,
                  ev_dt_bias_f=ev_dt_bias_f, ev_dt_bias_b=ev_dt_bias_b, ev_a_log_f=ev_a_log_f,
                  ev_a_log_b=ev_a_log_b, ev_d_skip=ev_d_skip, ev_gnorm_w=ev_gnorm_w, ev_w_out=ev_w_out,
                  ev_ln1_g=ev_ln1_g, ev_ln1_b=ev_ln1_b, ev_ffn_wg=ev_ffn_wg, ev_ffn_wu=ev_ffn_wu,
                  ev_ffn_wd=ev_ffn_wd, ev_ln2_g=ev_ln2_g, ev_ln2_b=ev_ln2_b, od_w_qkv=od_w_qkv, od_rpb=od_rpb,
                  od_w_out=od_w_out, od_ln1_g=od_ln1_g, od_ln1_b=od_ln1_b, od_router=od_router, od_wg=od_wg,
                  od_wu=od_wu, od_wd=od_wd, od_ln2_g=od_ln2_g, od_ln2_b=od_ln2_b)
    w = prepare_weights(params)
    return (trunk(x_prompt, w), trunk(x_sample, w))
```

```python
import functools
import math

import numpy as np
import jax
import jax.numpy as jnp
from jax import lax
from jax.experimental import pallas as pl
from jax.experimental.pallas import tpu as pltpu
from jax.experimental.pallas import tpu_sc as plsc

F32 = jnp.float32
BF16 = jnp.bfloat16

D_MODEL = 1024
GRID_W = 64
D_INNER = 2048
SSD_HEADS = 32
SSD_GROUPS = 4
HEADS_PER_GROUP = SSD_HEADS // SSD_GROUPS
GROUP_DIM = D_INNER // SSD_GROUPS
D_STATE = 128
D_CONV = 5
CHUNK = 128
CONV_DIM = D_INNER + 2 * SSD_GROUPS * D_STATE
FNET_GROUPS = 4
FNET_GROUP_DIM = 256
FNET_WIDTH = 1024
FNET_N2 = 128
FNET_SUB = 8
FNET_STEP = 16
NA_HEADS = 16
NA_HEAD_DIM = 64
NA_WIN_H = 8
NA_WIN_W = 16
D_FF_DENSE = 2816
N_EXPERTS = 8
D_FF_EXPERT = 3584
LN_EPS = 1e-5
RMS_EPS = 1e-5
DEPTH = 2
ALPHA = (2 * DEPTH) ** 0.25
LANES = 128
HALO = 16
NEG_BIG = -1e30
VMEM_LIMIT = 56 * 1024 * 1024
PACK_W = D_MODEL // 4
SC_WINDOW = 128
MOE_ROW_TILE = 1024


def _cparams(sem):
    return pltpu.CompilerParams(dimension_semantics=sem, vmem_limit_bytes=VMEM_LIMIT)


def _const_spec(shape):
    nd = len(shape)
    return pl.BlockSpec(shape, lambda *_: (0,) * nd)


def _dot(a, b):
    return jnp.dot(a, b, preferred_element_type=F32)


def _layer_norm(xf, g, b):
    mu = jnp.mean(xf, axis=-1, keepdims=True)
    xc = xf - mu
    var = jnp.mean(xc * xc, axis=-1, keepdims=True)
    return xc * lax.rsqrt(var + LN_EPS) * g + b


def _silu(x):
    return x * jax.nn.sigmoid(x)


def _ln_inproj_kernel(x_ref, g_ref, b_ref, wz_ref, wx_ref, wdt_ref, wu_ref,
                      h_ref, z_ref, xbc_ref, dt_ref, u_ref):
    h = _layer_norm(x_ref[...], g_ref[...], b_ref[...])
    h_ref[...] = h
    hb = h.astype(BF16)
    for j in range(0, D_INNER, 512):
        z_ref[:, j:j + 512] = _dot(hb, wz_ref[:, j:j + 512]).astype(BF16)
    for j in range(0, CONV_DIM, 512):
        xbc_ref[:, j:j + 512] = _dot(hb, wx_ref[:, j:j + 512]).astype(BF16)
    dt_ref[...] = _dot(hb, wdt_ref[...])
    for j in range(0, FNET_WIDTH, 512):
        u_ref[:, j:j + 512] = _dot(hb, wu_ref[:, j:j + 512])


def ln_inproj(x, g, b, wz, wx, wdt, wu, tm):
    n = x.shape[0]
    row = lambda w: pl.BlockSpec((tm, w), lambda i: (i, 0))
    return pl.pallas_call(
        _ln_inproj_kernel,
        grid=(n // tm,),
        in_specs=[row(D_MODEL), _const_spec((1, D_MODEL)), _const_spec((1, D_MODEL)),
                  _const_spec(wz.shape), _const_spec(wx.shape), _const_spec(wdt.shape),
                  _const_spec(wu.shape)],
        out_specs=[row(D_MODEL), row(D_INNER), row(CONV_DIM), row(LANES), row(FNET_WIDTH)],
        out_shape=[jax.ShapeDtypeStruct((n, D_MODEL), F32),
                   jax.ShapeDtypeStruct((n, D_INNER), BF16),
                   jax.ShapeDtypeStruct((n, CONV_DIM), BF16),
                   jax.ShapeDtypeStruct((n, LANES), F32),
                   jax.ShapeDtypeStruct((n, FNET_WIDTH), F32)],
        compiler_params=_cparams(("parallel",)),
        name="ln_inproj",
    )(x, g, b, wz, wx, wdt, wu)


def _conv_silu_kernel(prev_ref, main_ref, next_ref, shift_ref, w_ref, b_ref, o_ref, scr, *, tt, nt):
    i = pl.program_id(1)
    zero = jnp.zeros((HALO, scr.shape[1]), BF16)
    scr[0:HALO, :] = prev_ref[0]
    scr[HALO:HALO + tt, :] = main_ref[0]
    scr[HALO + tt:2 * HALO + tt, :] = next_ref[0]

    @pl.when(i == 0)
    def _():
        scr[0:HALO, :] = zero

    @pl.when(i == nt - 1)
    def _():
        scr[HALO + tt:2 * HALO + tt, :] = zero

    half = D_CONV // 2
    strip = 2 * LANES
    for r0 in range(0, tt, CHUNK):
        for c0 in range(0, scr.shape[1], strip):
            cols = slice(c0, c0 + strip)
            ext = scr[r0:r0 + CHUNK + 2 * HALO, cols]
            acc = b_ref[:, cols] + w_ref[half:half + 1, cols] * ext[HALO:HALO + CHUNK].astype(F32)
            for j, k in enumerate([k for k in range(D_CONV) if k != half]):
                sh = _dot(shift_ref[j * CHUNK:(j + 1) * CHUNK, :], ext)
                acc = acc + w_ref[k:k + 1, cols] * sh
            o_ref[0, r0:r0 + CHUNK, cols] = _silu(acc).astype(BF16)


def _conv_shift_matrix():
    taps = [k for k in range(D_CONV) if k != D_CONV // 2]
    s = np.zeros((len(taps) * CHUNK, CHUNK + 2 * HALO), np.float32)
    for j, k in enumerate(taps):
        for r in range(CHUNK):
            s[j * CHUNK + r, HALO + r + k - D_CONV // 2] = 1.0
    return s


def conv_silu(xbc, w, b, tt, tc):
    bsz, t, c = xbc.shape
    nt = t // tt
    hb = tt // HALO
    kern = functools.partial(_conv_silu_kernel, tt=tt, nt=nt)
    shift = jnp.asarray(_conv_shift_matrix(), BF16)
    return pl.pallas_call(
        kern,
        grid=(bsz, nt, c // tc),
        in_specs=[
            pl.BlockSpec((1, HALO, tc), lambda bi, i, ci: (bi, jnp.maximum(i * hb - 1, 0), ci)),
            pl.BlockSpec((1, tt, tc), lambda bi, i, ci: (bi, i, ci)),
            pl.BlockSpec((1, HALO, tc), lambda bi, i, ci: (bi, jnp.minimum((i + 1) * hb, t // HALO - 1), ci)),
            _const_spec(shift.shape),
            pl.BlockSpec((D_CONV, tc), lambda bi, i, ci: (0, ci)),
            pl.BlockSpec((1, tc), lambda bi, i, ci: (0, ci)),
        ],
        out_specs=pl.BlockSpec((1, tt, tc), lambda bi, i, ci: (bi, i, ci)),
        out_shape=jax.ShapeDtypeStruct((bsz, t, c), BF16),
        scratch_shapes=[pltpu.VMEM((tt + 2 * HALO, tc), BF16)],
        compiler_params=_cparams(("parallel", "parallel", "parallel")),
        name="conv_silu",
    )(xbc, xbc, xbc, shift, w, b)


def _split3(x):
    hi = x.astype(BF16)
    r1 = x - hi.astype(F32)
    mid = r1.astype(BF16)
    lo = (r1 - mid.astype(F32)).astype(BF16)
    return hi, mid, lo


def _ssd_prep_kernel(dt_ref, bias_ref, alog_ref, tri_ref, trit_ref, sel_ref, pcol_ref, q_ref, *, cpp):
    lane = lax.broadcasted_iota(jnp.int32, (CHUNK, LANES), 1)
    fwd = lane < SSD_HEADS
    a_coef = -jnp.exp(alog_ref[...])
    for c in range(cpp):
        rows = slice(c * CHUNK, (c + 1) * CHUNK)
        raw = dt_ref[0, rows, :] + bias_ref[...]
        dt = jnp.maximum(raw, 0.0) + jnp.log1p(jnp.exp(-jnp.abs(raw)))
        a = dt * a_coef
        cs_f = jnp.dot(tri_ref[...], a, precision=lax.Precision.HIGHEST, preferred_element_type=F32)
        cs_b = jnp.dot(trit_ref[...], a, precision=lax.Precision.HIGHEST, preferred_element_type=F32)
        cs = jnp.where(fwd, cs_f, cs_b)
        end = jnp.where(fwd[0:1], cs[CHUNK - 1:CHUNK, :], cs[0:1, :])
        wend = jnp.exp(end - cs) * dt
        hi, mid, lo = _split3(cs)
        src = jnp.concatenate([hi, mid, lo, wend.astype(BF16)], axis=1)
        pc = _dot(src, sel_ref[...]).astype(BF16)
        pcol_ref[0, 0, rows, :] = pc[:, :SSD_GROUPS * LANES]
        pcol_ref[1, 0, rows, :] = pc[:, SSD_GROUPS * LANES:]
        cs_t = cs.T
        dt_t = dt.T
        for d in range(2):
            for g in range(SSD_GROUPS):
                r0 = d * SSD_HEADS + g * HEADS_PER_GROUP
                q_ref[d, 0, g, c, 0:HEADS_PER_GROUP, :] = cs_t[r0:r0 + HEADS_PER_GROUP, :]
                q_ref[d, 0, g, c, HEADS_PER_GROUP:2 * HEADS_PER_GROUP, :] = dt_t[r0:r0 + HEADS_PER_GROUP, :]


def _prep_select_matrix():
    sel = np.zeros((4 * LANES, 2 * SSD_GROUPS * LANES), np.float32)
    for d in range(2):
        for g in range(SSD_GROUPS):
            for q in range(4):
                for i in range(HEADS_PER_GROUP):
                    src = q * LANES + d * SSD_HEADS + g * HEADS_PER_GROUP + i
                    dst = d * SSD_GROUPS * LANES + g * LANES + q * HEADS_PER_GROUP + i
                    sel[src, dst] = 1.0
    return sel


def ssd_prep(dt_raw, bias_row, alog_row, cpp):
    bsz, t, _ = dt_raw.shape
    nc = t // CHUNK
    tri = np.tril(np.ones((CHUNK, CHUNK), np.float32))
    kern = functools.partial(_ssd_prep_kernel, cpp=cpp)
    return pl.pallas_call(
        kern,
        grid=(bsz, nc // cpp),
        in_specs=[pl.BlockSpec((1, cpp * CHUNK, LANES), lambda b, j: (b, j, 0)),
                  _const_spec((1, LANES)), _const_spec((1, LANES)),
                  _const_spec((CHUNK, CHUNK)), _const_spec((CHUNK, CHUNK)),
                  _const_spec((4 * LANES, 2 * SSD_GROUPS * LANES))],
        out_specs=[pl.BlockSpec((2, 1, cpp * CHUNK, SSD_GROUPS * LANES), lambda b, j: (0, b, j, 0)),
                   pl.BlockSpec((2, 1, SSD_GROUPS, cpp, 2 * HEADS_PER_GROUP, CHUNK),
                                lambda b, j: (0, b, 0, j, 0, 0))],
        out_shape=[jax.ShapeDtypeStruct((2, bsz, t, SSD_GROUPS * LANES), BF16),
                   jax.ShapeDtypeStruct((2, bsz, SSD_GROUPS, nc, 2 * HEADS_PER_GROUP, CHUNK), F32)],
        compiler_params=_cparams(("parallel", "parallel")),
        name="ssd_prep",
    )(dt_raw, bias_row, alog_row, jnp.asarray(tri), jnp.asarray(tri.T),
      jnp.asarray(_prep_select_matrix(), BF16))


def _expand_matrix():
    e = np.zeros((LANES, HEADS_PER_GROUP * LANES + GROUP_DIM), np.float32)
    for q in range(3):
        for h in range(HEADS_PER_GROUP):
            e[q * HEADS_PER_GROUP + h, h * LANES:(h + 1) * LANES] = 1.0
    for h in range(HEADS_PER_GROUP):
        e[3 * HEADS_PER_GROUP + h,
          HEADS_PER_GROUP * LANES + h * 64:HEADS_PER_GROUP * LANES + (h + 1) * 64] = 1.0
    return e


def _ssd_kernel(*refs, rev, cps, fuse):
    if fuse:
        xs_ref, b_ref, c_ref, p_ref, q_ref, e_ref, yb_ref, z_ref, dsk_ref, gn_ref, o_ref, st_ref = refs
    else:
        xs_ref, b_ref, c_ref, p_ref, q_ref, e_ref, o_ref, st_ref = refs

    @pl.when(pl.program_id(2) == 0)
    def _():
        st_ref[...] = jnp.zeros_like(st_ref)

    row = lax.broadcasted_iota(jnp.int32, (CHUNK, CHUNK), 0)
    col = lax.broadcasted_iota(jnp.int32, (CHUNK, CHUNK), 1)
    mask = (col >= row) if rev else (col <= row)
    lo_half = lax.broadcasted_iota(jnp.int32, (CHUNK, LANES), 1) < 64
    nb = HEADS_PER_GROUP * LANES
    end = 0 if rev else CHUNK - 1
    order = range(cps - 1, -1, -1) if rev else range(cps)
    for c in order:
        rows = slice(c * CHUNK, (c + 1) * CHUNK)
        xf = xs_ref[0, rows, :].astype(F32)
        bm = b_ref[0, rows, :]
        cm = c_ref[0, rows, :]
        ex = _dot(p_ref[0, 0, rows, :], e_ref[...])
        qv = q_ref[0, 0, 0, c]
        cb = lax.dot_general(cm, bm, (((1,), (1,)), ((), ())), preferred_element_type=F32)
        cs_parts = []
        y_parts = []
        for j in range(HEADS_PER_GROUP // 2):
            h1, h2 = 2 * j, 2 * j + 1
            c1 = ex[:, h1 * LANES:(h1 + 1) * LANES]
            c2 = ex[:, h2 * LANES:(h2 + 1) * LANES]
            cs_parts.append(jnp.where(lo_half, c1, c2))
            w1 = jnp.exp(jnp.where(mask, c1 - qv[h1:h1 + 1, :], -jnp.inf)) * (
                cb * qv[HEADS_PER_GROUP + h1:HEADS_PER_GROUP + h1 + 1, :])
            w2 = jnp.exp(jnp.where(mask, c2 - qv[h2:h2 + 1, :], -jnp.inf)) * (
                cb * qv[HEADS_PER_GROUP + h2:HEADS_PER_GROUP + h2 + 1, :])
            wp = jnp.concatenate([w1, w2], axis=1).astype(BF16)
            xp = xf[:, j * LANES:(j + 1) * LANES]
            rhs = jnp.concatenate([jnp.where(lo_half, xp, 0.0), jnp.where(lo_half, 0.0, xp)],
                                  axis=0).astype(BF16)
            y_parts.append(_dot(wp, rhs))
        ydiag = jnp.concatenate(y_parts, axis=1)
        expcs = jnp.exp(jnp.concatenate(cs_parts, axis=1))
        decay = expcs[end:end + 1, :]
        st = st_ref[...]
        yoff = _dot(cm, st.astype(BF16)) * expcs
        xsw = (xf * ex[:, nb:nb + GROUP_DIM]).astype(BF16)
        st_ref[...] = st * decay + lax.dot_general(bm, xsw, (((0,), (0,)), ((), ())),
                                                   preferred_element_type=F32)
        y = ydiag + yoff
        if fuse:
            y = y + yb_ref[0, rows, :].astype(F32) + dsk_ref[...] * xf
            y = y * _silu(z_ref[0, rows, :].astype(F32))
            ms = jnp.mean(y * y, axis=-1, keepdims=True)
            o_ref[0, rows, :] = (y * lax.rsqrt(ms + RMS_EPS) * gn_ref[...]).astype(BF16)
        else:
            o_ref[0, rows, :] = y.astype(BF16)


def ssd_scan(xbc_act, pcol, q, rev, cps, fused=None):
    bsz, t, _ = xbc_act.shape
    ncb = t // (cps * CHUNK)
    r = cps * CHUNK
    d = 1 if rev else 0
    cidx = (lambda j: ncb - 1 - j) if rev else (lambda j: j)
    e = jnp.asarray(_expand_matrix(), BF16)
    b0 = D_INNER // LANES
    c0 = b0 + SSD_GROUPS
    in_specs = [
        pl.BlockSpec((1, r, GROUP_DIM), lambda b, g, j: (b, cidx(j), g)),
        pl.BlockSpec((1, r, LANES), lambda b, g, j: (b, cidx(j), b0 + g)),
        pl.BlockSpec((1, r, LANES), lambda b, g, j: (b, cidx(j), c0 + g)),
        pl.BlockSpec((1, 1, r, LANES), lambda b, g, j: (d, b, cidx(j), g)),
        pl.BlockSpec((1, 1, 1, cps, 2 * HEADS_PER_GROUP, CHUNK), lambda b, g, j: (d, b, g, cidx(j), 0, 0)),
        _const_spec(e.shape),
    ]
    args = [xbc_act, xbc_act, xbc_act, pcol, q, e]
    if fused is not None:
        yb, z, dsk, gn = fused
        in_specs += [
            pl.BlockSpec((1, r, GROUP_DIM), lambda b, g, j: (b, cidx(j), g)),
            pl.BlockSpec((1, r, GROUP_DIM), lambda b, g, j: (b, cidx(j), g)),
            pl.BlockSpec((1, GROUP_DIM), lambda b, g, j: (0, g)),
            pl.BlockSpec((1, GROUP_DIM), lambda b, g, j: (0, g)),
        ]
        args += [yb, z, dsk, gn]
    kern = functools.partial(_ssd_kernel, rev=rev, cps=cps, fuse=fused is not None)
    return pl.pallas_call(
        kern,
        grid=(bsz, SSD_GROUPS, ncb),
        in_specs=in_specs,
        out_specs=pl.BlockSpec((1, r, GROUP_DIM), lambda b, g, j: (b, cidx(j), g)),
        out_shape=jax.ShapeDtypeStruct((bsz, t, D_INNER), BF16),
        scratch_shapes=[pltpu.VMEM((D_STATE, GROUP_DIM), F32)],
        compiler_params=_cparams(("parallel", "parallel", "arbitrary")),
        name="ssd_bwd" if rev else "ssd_fwd",
    )(*args)


def _dft_cos_sin(n):
    k = np.arange(n, dtype=np.float64)
    ang = 2.0 * np.pi * np.outer(k, k) / n
    return np.cos(ang), np.sin(ang)


def _fnet_s1_kernel(u_ref, g_ref, cos_ref, sin_ref, o_ref, *, n1):
    c = FNET_WIDTH
    rows = n1 * FNET_SUB
    re_parts, im_parts = [], []
    for h in range(FNET_STEP // FNET_SUB):
        x = u_ref[0, :, h * FNET_SUB:(h + 1) * FNET_SUB, :].reshape(rows, c).astype(BF16)
        a = _dot(g_ref[...], x)
        ar = a[:rows]
        ai = a[rows:]
        ct = jnp.concatenate([cos_ref[h]] * (c // LANES), axis=1)
        st = jnp.concatenate([sin_ref[h]] * (c // LANES), axis=1)
        re_parts.append((ar * ct + ai * st).reshape(n1, FNET_SUB, c))
        im_parts.append((ai * ct - ar * st).reshape(n1, FNET_SUB, c))
    o_ref[0, 0] = jnp.concatenate(re_parts, axis=1).astype(BF16)
    o_ref[0, 1] = jnp.concatenate(im_parts, axis=1).astype(BF16)


def _fnet_s2_kernel(br_ref, bi_ref, f2_ref, wc_ref, o_ref):
    n2 = FNET_N2
    rhs = jnp.concatenate([br_ref[0, 0, 0], bi_ref[0, 0, 0]], axis=0)
    g = _dot(f2_ref[...], rhs).astype(BF16)
    for grp in range(FNET_GROUPS):
        cols = slice(grp * FNET_GROUP_DIM, (grp + 1) * FNET_GROUP_DIM)
        lhs = jnp.concatenate([g[:n2, cols], g[n2:, cols]], axis=1)
        o_ref[0, :, cols] = _dot(lhs, wc_ref[...]).astype(BF16)


def fnet_mix(u):
    bsz, t, c = u.shape
    n2 = FNET_N2
    n1 = t // n2
    sub, step = FNET_SUB, FNET_STEP
    c1, s1 = _dft_cos_sin(n1)
    c2, s2 = _dft_cos_sin(n2)
    cc, sc = _dft_cos_sin(FNET_GROUP_DIM)
    g1 = jnp.asarray(np.kron(np.concatenate([c1, -s1], axis=0), np.eye(sub)), BF16)
    f2 = jnp.asarray(np.block([[c2, s2], [-s2, c2]]), BF16)
    scale = 1.0 / math.sqrt(t * FNET_GROUP_DIM)
    wc = jnp.asarray(np.concatenate([cc, sc], axis=0) * scale, BF16)
    k1 = np.arange(n1)[None, :, None]
    pos = (np.arange(n2 // sub)[:, None, None] * sub + np.arange(sub)[None, None, :])
    ang = (2.0 * np.pi * k1 * pos / t).reshape(n2 // sub, n1 * sub)
    tw_cos = jnp.asarray(np.repeat(np.cos(ang)[:, :, None], LANES, axis=2), F32)
    tw_sin = jnp.asarray(np.repeat(np.sin(ang)[:, :, None], LANES, axis=2), F32)

    hs = step // sub
    bv = pl.pallas_call(
        functools.partial(_fnet_s1_kernel, n1=n1),
        grid=(bsz, n2 // step),
        in_specs=[pl.BlockSpec((1, n1, step, c), lambda b, j: (b, 0, j, 0)),
                  _const_spec(g1.shape),
                  pl.BlockSpec((hs, n1 * sub, LANES), lambda b, j: (j, 0, 0)),
                  pl.BlockSpec((hs, n1 * sub, LANES), lambda b, j: (j, 0, 0))],
        out_specs=pl.BlockSpec((1, 2, n1, step, c), lambda b, j: (b, 0, 0, j, 0)),
        out_shape=jax.ShapeDtypeStruct((bsz, 2, n1, n2, c), BF16),
        compiler_params=_cparams(("parallel", "parallel")),
        name="fnet_stage1",
    )(u.reshape(bsz, n1, n2, c), g1, tw_cos, tw_sin)

    y = pl.pallas_call(
        _fnet_s2_kernel,
        grid=(bsz, n1),
        in_specs=[pl.BlockSpec((1, 1, 1, n2, c), lambda b, k: (b, 0, k, 0, 0)),
                  pl.BlockSpec((1, 1, 1, n2, c), lambda b, k: (b, 1, k, 0, 0)),
                  _const_spec((2 * n2, 2 * n2)),
                  _const_spec((2 * FNET_GROUP_DIM, FNET_GROUP_DIM))],
        out_specs=pl.BlockSpec((1, n2, c), lambda b, k: (b, 0, k)),
        out_shape=jax.ShapeDtypeStruct((bsz, n2, n1 * c), BF16),
        compiler_params=_cparams(("parallel", "parallel")),
        name="fnet_stage2",
    )(bv, bv, f2, wc)
    return y.reshape(bsz, t, c)


def _pack_bf16_pair(a, b):
    lo = lax.bitcast_convert_type(a.astype(BF16).astype(F32), jnp.uint32)
    hi = lax.bitcast_convert_type(b.astype(BF16).astype(F32), jnp.uint32)
    word = lax.shift_right_logical(lo, jnp.uint32(16)) | hi
    return lax.bitcast_convert_type(word, jnp.int32)


def _unpack_bf16_pair(word):
    w = lax.bitcast_convert_type(word, jnp.uint32)
    a = lax.bitcast_convert_type(lax.shift_left(w, jnp.uint32(16)), F32)
    b = lax.bitcast_convert_type(w & jnp.uint32(0xFFFF0000), F32)
    return a, b


def _pack_row(x, a_ref, b_ref):
    a_ref[...] = _pack_bf16_pair(x[:, 0:PACK_W], x[:, PACK_W:2 * PACK_W])
    b_ref[...] = _pack_bf16_pair(x[:, 2 * PACK_W:3 * PACK_W], x[:, 3 * PACK_W:4 * PACK_W])


def _unpack_row(a_word, b_word):
    x0, x1 = _unpack_bf16_pair(a_word)
    x2, x3 = _unpack_bf16_pair(b_word)
    return jnp.concatenate([x0, x1, x2, x3], axis=1)


def _proj_res_ln_kernel(*refs, n_in, packed):
    xs = refs[:n_in]
    ws = refs[n_in:2 * n_in]
    h_ref, g_ref, b_ref, o_ref = refs[2 * n_in:2 * n_in + 4]
    acc = _dot(xs[0][...], ws[0][...])
    for x_ref, w_ref in zip(xs[1:], ws[1:]):
        acc = acc + _dot(x_ref[...], w_ref[...])
    out = _layer_norm(ALPHA * h_ref[...] + acc, g_ref[...], b_ref[...])
    o_ref[...] = out
    if packed:
        _pack_row(out, refs[-2], refs[-1])


def proj_res_ln(xs, ws, h, g, b, tm, packed=False):
    n = h.shape[0]
    n_in = len(xs)
    in_specs = [pl.BlockSpec((tm, x.shape[1]), lambda i: (i, 0)) for x in xs]
    in_specs += [_const_spec(w.shape) for w in ws]
    in_specs += [pl.BlockSpec((tm, D_MODEL), lambda i: (i, 0)),
                 _const_spec((1, D_MODEL)), _const_spec((1, D_MODEL))]
    out_specs = [pl.BlockSpec((tm, D_MODEL), lambda i: (i, 0))]
    out_shape = [jax.ShapeDtypeStruct((n, D_MODEL), F32)]
    if packed:
        out_specs += [pl.BlockSpec((tm, PACK_W), lambda i: (i, 0))] * 2
        out_shape += [jax.ShapeDtypeStruct((n, PACK_W), jnp.int32)] * 2
    res = pl.pallas_call(
        functools.partial(_proj_res_ln_kernel, n_in=n_in, packed=packed),
        grid=(n // tm,),
        in_specs=in_specs,
        out_specs=out_specs,
        out_shape=out_shape,
        compiler_params=_cparams(("parallel",)),
        name="proj_res_ln",
    )(*xs, *ws, h, g, b)
    return res if packed else res[0]


def _ffn_res_ln_kernel(h_ref, wg_ref, wu_ref, wd_ref, g_ref, b_ref, o_ref, act_ref, *, tf):
    h = h_ref[...]
    hb = h.astype(BF16)
    for j in range(0, D_FF_DENSE, tf):
        gate = _dot(hb, wg_ref[:, j:j + tf])
        up = _dot(hb, wu_ref[:, j:j + tf])
        act_ref[:, j:j + tf] = (_silu(gate) * up).astype(BF16)
    y = _dot(act_ref[...], wd_ref[...])
    o_ref[...] = _layer_norm(ALPHA * h + y, g_ref[...], b_ref[...])


def ffn_res_ln(h, wg, wu, wd, g, b, tm, tf=256):
    n = h.shape[0]
    return pl.pallas_call(
        functools.partial(_ffn_res_ln_kernel, tf=tf),
        grid=(n // tm,),
        in_specs=[pl.BlockSpec((tm, D_MODEL), lambda i: (i, 0)),
                  _const_spec(wg.shape), _const_spec(wu.shape), _const_spec(wd.shape),
                  _const_spec((1, D_MODEL)), _const_spec((1, D_MODEL))],
        out_specs=pl.BlockSpec((tm, D_MODEL), lambda i: (i, 0)),
        out_shape=jax.ShapeDtypeStruct((n, D_MODEL), F32),
        scratch_shapes=[pltpu.VMEM((tm, D_FF_DENSE), BF16)],
        compiler_params=_cparams(("parallel",)),
        name="ffn_res_ln",
    )(h, wg, wu, wd, g, b)


def _qkv_kernel(h_ref, w_ref, q_ref, k_ref, v_ref):
    hb = h_ref[...].astype(BF16)
    d = D_MODEL
    scale = NA_HEAD_DIM ** -0.5
    for j in range(0, d, 512):
        q_ref[:, j:j + 512] = (_dot(hb, w_ref[:, j:j + 512]) * scale).astype(BF16)
        k_ref[:, j:j + 512] = _dot(hb, w_ref[:, d + j:d + j + 512]).astype(BF16)
        v_ref[:, j:j + 512] = _dot(hb, w_ref[:, 2 * d + j:2 * d + j + 512]).astype(BF16)


def qkv_proj(h, w, tm):
    n = h.shape[0]
    row = pl.BlockSpec((tm, D_MODEL), lambda i: (i, 0))
    sds = jax.ShapeDtypeStruct((n, D_MODEL), BF16)
    return pl.pallas_call(
        _qkv_kernel,
        grid=(n // tm,),
        in_specs=[row, _const_spec(w.shape)],
        out_specs=[row, row, row],
        out_shape=[sds, sds, sds],
        compiler_params=_cparams(("parallel",)),
        name="qkv_proj",
    )(h, w)


def _natten_kernel(q_ref, k_ref, v_ref, bias_ref, o_ref, *, rows, hw, rq):
    nk = NA_WIN_H * GRID_W
    npair = hw // LANES
    lo_half = lax.broadcasted_iota(jnp.int32, (GRID_W, LANES), 1) < NA_HEAD_DIM
    starts = []
    scores = []
    for r in range(rq):
        i = pl.program_id(2) * rq + r
        r0 = jnp.clip(i - NA_WIN_H // 2, 0, rows - NA_WIN_H)
        case = i - r0
        starts.append(pl.multiple_of(r0 * GRID_W, GRID_W))
        for p in range(npair):
            cols = slice(p * LANES, (p + 1) * LANES)
            qp = q_ref[0, r * GRID_W:(r + 1) * GRID_W, cols].astype(F32)
            qbd = jnp.concatenate([jnp.where(lo_half, qp, 0.0), jnp.where(lo_half, 0.0, qp)],
                                  axis=0).astype(BF16)
            s = lax.dot_general(qbd, k_ref[0, pl.ds(starts[r], nk), cols], (((1,), (1,)), ((), ())),
                                preferred_element_type=F32)
            scores.append(s + bias_ref[case, p])
    s_all = jnp.concatenate(scores, axis=0)
    e_f32 = jnp.exp(s_all - jnp.max(s_all, axis=-1, keepdims=True))
    inv_all = 1.0 / jnp.sum(e_f32, axis=-1, keepdims=True)
    e_all = e_f32.astype(BF16)
    for r in range(rq):
        for p in range(npair):
            cols = slice(p * LANES, (p + 1) * LANES)
            c0 = (r * npair + p) * 2 * GRID_W
            e = e_all[c0:c0 + 2 * GRID_W]
            o2 = _dot(e, v_ref[0, pl.ds(starts[r], nk), cols]) * inv_all[c0:c0 + 2 * GRID_W]
            o_ref[0, r * GRID_W:(r + 1) * GRID_W, cols] = jnp.where(
                lo_half, o2[:GRID_W], o2[GRID_W:]).astype(BF16)


def _natten_bias_table(rpb):
    j = np.arange(GRID_W)[:, None]
    kc = np.arange(GRID_W)[None, :]
    cstart = np.clip(j - NA_WIN_W // 2, 0, GRID_W - NA_WIN_W)
    valid = (kc >= cstart) & (kc < cstart + NA_WIN_W)
    coff = np.clip(kc - j + NA_WIN_W - 1, 0, 2 * NA_WIN_W - 2)
    dd = np.arange(NA_WIN_H)[:, None]
    a = np.arange(NA_WIN_H)[None, :]
    roff = a + (NA_WIN_H - 1) - dd
    rows = rpb.astype(F32)[:, roff, :]
    onehot = (coff[None, :, :] == np.arange(2 * NA_WIN_W - 1)[:, None, None]) & valid[None]
    t = jnp.einsum("hdac,cjk->dhjak", rows, jnp.asarray(onehot, F32),
                   precision=lax.Precision.HIGHEST)
    t = jnp.where(valid[None, None, :, None, :], t, NEG_BIG)
    return t.reshape(NA_WIN_H, NA_HEADS // 2, 2 * GRID_W, NA_WIN_H * GRID_W)


def natten(q, k, v, bias, hw=512, rq=4):
    bsz, t, d = q.shape
    rows = t // GRID_W
    nh = d // hw
    pp = hw // LANES
    resident = dict(pipeline_mode=pl.Buffered(1))
    return pl.pallas_call(
        functools.partial(_natten_kernel, rows=rows, hw=hw, rq=rq),
        grid=(bsz, nh, rows // rq),
        in_specs=[pl.BlockSpec((1, rq * GRID_W, hw), lambda b, hh, i: (b, i, hh)),
                  pl.BlockSpec((1, t, hw), lambda b, hh, i: (b, 0, hh), **resident),
                  pl.BlockSpec((1, t, hw), lambda b, hh, i: (b, 0, hh), **resident),
                  pl.BlockSpec((NA_WIN_H, pp, 2 * GRID_W, NA_WIN_H * GRID_W),
                               lambda b, hh, i: (0, hh, 0, 0), **resident)],
        out_specs=pl.BlockSpec((1, rq * GRID_W, hw), lambda b, hh, i: (b, i, hh)),
        out_shape=jax.ShapeDtypeStruct((bsz, t, d), BF16),
        compiler_params=_cparams(("parallel", "parallel", "arbitrary")),
        name="natten",
    )(q, k, v, bias)


def _lane_pick(x, lane, idx):
    return jnp.sum(jnp.where(lane == idx, x, 0.0), axis=-1, keepdims=True)


def _router_kernel(h_ref, w_ref, ltri_ref, meta_ref, gate_ref, cnt_ref, run_ref):
    @pl.when(pl.program_id(0) == 0)
    def _():
        run_ref[...] = jnp.zeros_like(run_ref)

    logits = jnp.dot(h_ref[...], w_ref[...], precision=lax.Precision.HIGHEST,
                     preferred_element_type=F32)
    lane = lax.broadcasted_iota(jnp.int32, logits.shape, 1)
    logits = jnp.where(lane < N_EXPERTS, logits, -jnp.inf)
    m1 = jnp.max(logits, axis=-1, keepdims=True)
    i1 = jnp.min(jnp.where(logits == m1, lane, LANES), axis=-1, keepdims=True)
    rest = jnp.where(lane == i1, -jnp.inf, logits)
    m2 = jnp.max(rest, axis=-1, keepdims=True)
    i2 = jnp.min(jnp.where(rest == m2, lane, LANES), axis=-1, keepdims=True)
    e2 = jnp.exp(m2 - m1)
    inv = 1.0 / (1.0 + e2)
    gate_ref[...] = jnp.where(lane == 0, inv, jnp.where(lane == 1, e2 * inv, 0.0))

    sel = jnp.where(lane == i1, 1.0, jnp.where(lane == i2, 1.0, 0.0))
    before = _dot(ltri_ref[...], sel.astype(BF16)) + run_ref[...]
    rank1 = _lane_pick(before, lane, i1)
    rank2 = _lane_pick(before, lane, i2)
    run_ref[...] += jnp.sum(sel, axis=0, keepdims=True)
    cnt_ref[...] = jnp.broadcast_to(run_ref[...], cnt_ref.shape)
    packed = jnp.where(lane == 0, rank1, jnp.where(lane == 1, rank2, jnp.where(
        lane == 2, i1.astype(F32), jnp.where(lane == 3, i2.astype(F32), 0.0))))
    meta_ref[...] = packed.T[0:8, :]


def router(h, w_pad, tm):
    n = h.shape[0]
    ltri = jnp.asarray(np.tril(np.ones((tm, tm), np.float32), -1), BF16)
    return pl.pallas_call(
        _router_kernel,
        grid=(n // tm,),
        in_specs=[pl.BlockSpec((tm, D_MODEL), lambda i: (i, 0)), _const_spec(w_pad.shape),
                  _const_spec((tm, tm))],
        out_specs=[pl.BlockSpec((8, tm), lambda i: (0, i)),
                   pl.BlockSpec((tm, LANES), lambda i: (i, 0)),
                   _const_spec((8, LANES))],
        out_shape=[jax.ShapeDtypeStruct((8, n), F32),
                   jax.ShapeDtypeStruct((n, LANES), F32),
                   jax.ShapeDtypeStruct((8, LANES), F32)],
        scratch_shapes=[pltpu.VMEM((1, LANES), F32)],
        compiler_params=_cparams(("arbitrary",)),
        name="router",
    )(h, w_pad, ltri)


def _moe_pos_kernel(off_ref, meta_ref, pos_ref):
    m = meta_ref[...]
    for k in range(2):
        rank = m[k:k + 1, :].astype(jnp.int32)
        expert = m[2 + k:3 + k, :].astype(jnp.int32)
        base = jnp.zeros_like(rank)
        for e in range(N_EXPERTS):
            base = jnp.where(expert == e, off_ref[e], base)
        pos_ref[k:k + 1, :] = rank + base


def moe_positions(meta, offsets, tn):
    n = meta.shape[1]
    return pl.pallas_call(
        _moe_pos_kernel,
        grid_spec=pltpu.PrefetchScalarGridSpec(
            num_scalar_prefetch=1,
            grid=(n // tn,),
            in_specs=[pl.BlockSpec((8, tn), lambda i, off: (0, i))],
            out_specs=pl.BlockSpec((2, tn), lambda i, off: (0, i)),
        ),
        out_shape=jax.ShapeDtypeStruct((2, n), jnp.int32),
        compiler_params=_cparams(("parallel",)),
        name="moe_positions",
    )(offsets, meta)


def _sc_mesh():
    return plsc.VectorSubcoreMesh(core_axis_name="core", subcore_axis_name="subcore")


def sc_dispatch_rows(x, pos, out_rows):
    n, dim = x.shape

    @pl.kernel(out_type=jax.ShapeDtypeStruct((out_rows, dim), x.dtype), mesh=_sc_mesh(), scratch_types=[])
    def k(x_hbm, i0_hbm, i1_hbm, o_hbm):
        def body(x_v, i0_v, i1_v):
            pltpu.sync_copy(x_v, o_hbm.at[i0_v.at[0]])
            pltpu.sync_copy(x_v, o_hbm.at[i1_v.at[0]])

        pltpu.emit_pipeline(
            body, grid=(n // SC_WINDOW,),
            in_specs=[pl.BlockSpec((SC_WINDOW, dim), index_map=lambda i: (i, 0)),
                      pl.BlockSpec((1, SC_WINDOW), index_map=lambda i: (0, i)),
                      pl.BlockSpec((1, SC_WINDOW), index_map=lambda i: (0, i))],
            out_specs=[],
            core_axis_name=("core", "subcore"), dimension_semantics=(pltpu.PARALLEL,),
        )(x_hbm, i0_hbm, i1_hbm)

    return k(x, pos[0:1], pos[1:2])


def sc_gather_rows(x, idx):
    ni = idx.shape[1]
    dim = x.shape[1]

    @pl.kernel(out_type=jax.ShapeDtypeStruct((ni, dim), x.dtype), mesh=_sc_mesh(), scratch_types=[])
    def k(x_hbm, i_hbm, o_hbm):
        def body(i_v, o_v):
            pltpu.sync_copy(x_hbm.at[i_v.at[0]], o_v)

        pltpu.emit_pipeline(
            body, grid=(ni // SC_WINDOW,),
            in_specs=[pl.BlockSpec((1, SC_WINDOW), index_map=lambda i: (0, i))],
            out_specs=[pl.BlockSpec((SC_WINDOW, dim), index_map=lambda i: (i, 0))],
            core_axis_name=("core", "subcore"), dimension_semantics=(pltpu.PARALLEL,),
        )(i_hbm, o_hbm)

    return k(x, idx)


def _moe_expert_kernel(te_ref, nu_ref, xa_ref, xb_ref, wg_ref, wu_ref, wd_ref, ya_ref, yb_ref,
                       x_scr, acc_ref, *, nf):
    i = pl.program_id(0)
    f = pl.program_id(1)

    @pl.when(i < nu_ref[0])
    def _():
        @pl.when(f == 0)
        def _():
            x_scr[...] = _unpack_row(xa_ref[...], xb_ref[...]).astype(BF16)
            acc_ref[...] = jnp.zeros_like(acc_ref)

        xb = x_scr[...]
        act = _silu(_dot(xb, wg_ref[0])) * _dot(xb, wu_ref[0])
        acc_ref[...] += _dot(act.astype(BF16), wd_ref[0])

        @pl.when(f == nf - 1)
        def _():
            _pack_row(acc_ref[...], ya_ref, yb_ref)


def moe_experts(xa, xb, tile_expert, n_used, wg, wu, wd, tf):
    rows = xa.shape[0]
    nt = rows // MOE_ROW_TILE
    nf = D_FF_EXPERT // tf
    xspec = pl.BlockSpec((MOE_ROW_TILE, PACK_W), lambda i, f, te, nu: (i, 0))
    sds = jax.ShapeDtypeStruct((rows, PACK_W), jnp.int32)
    return pl.pallas_call(
        functools.partial(_moe_expert_kernel, nf=nf),
        grid_spec=pltpu.PrefetchScalarGridSpec(
            num_scalar_prefetch=2,
            grid=(nt, nf),
            in_specs=[xspec, xspec,
                      pl.BlockSpec((1, D_MODEL, tf), lambda i, f, te, nu: (te[i], 0, f)),
                      pl.BlockSpec((1, D_MODEL, tf), lambda i, f, te, nu: (te[i], 0, f)),
                      pl.BlockSpec((1, tf, D_MODEL), lambda i, f, te, nu: (te[i], f, 0))],
            out_specs=[xspec, xspec],
            scratch_shapes=[pltpu.VMEM((MOE_ROW_TILE, D_MODEL), BF16),
                            pltpu.VMEM((MOE_ROW_TILE, D_MODEL), F32)],
        ),
        out_shape=[sds, sds],
        compiler_params=_cparams(("parallel", "arbitrary")),
        name="moe_experts",
    )(tile_expert, n_used, xa, xb, wg, wu, wd)


def _moe_combine_kernel(h_ref, gate_ref, ya0_ref, yb0_ref, ya1_ref, yb1_ref, g_ref, b_ref, o_ref):
    gate = gate_ref[...]
    lane = lax.broadcasted_iota(jnp.int32, gate.shape, 1)
    g0 = _lane_pick(gate, lane, 0)
    g1 = _lane_pick(gate, lane, 1)
    y = g0 * _unpack_row(ya0_ref[...], yb0_ref[...]) + g1 * _unpack_row(ya1_ref[...], yb1_ref[...])
    o_ref[...] = _layer_norm(ALPHA * h_ref[...] + y, g_ref[...], b_ref[...])


def moe_combine_res_ln(h, gate, ya, yb, g, b, tm):
    n = h.shape[0]
    nb = n // tm
    first = pl.BlockSpec((tm, PACK_W), lambda i: (i, 0))
    second = pl.BlockSpec((tm, PACK_W), lambda i: (nb + i, 0))
    return pl.pallas_call(
        _moe_combine_kernel,
        grid=(nb,),
        in_specs=[pl.BlockSpec((tm, D_MODEL), lambda i: (i, 0)),
                  pl.BlockSpec((tm, LANES), lambda i: (i, 0)),
                  first, first, second, second,
                  _const_spec((1, D_MODEL)), _const_spec((1, D_MODEL))],
        out_specs=pl.BlockSpec((tm, D_MODEL), lambda i: (i, 0)),
        out_shape=jax.ShapeDtypeStruct((n, D_MODEL), F32),
        compiler_params=_cparams(("parallel",)),
        name="moe_combine_res_ln",
    )(h, gate, ya, yb, ya, yb, g, b)


def _row(v):
    return v.reshape(1, -1).astype(F32)


def _pad_lanes(v, fill=0.0):
    v = v.astype(F32)
    pad = LANES - v.shape[-1]
    return jnp.concatenate([v, jnp.full(v.shape[:-1] + (pad,), fill, F32)], axis=-1)


def prepare_weights(p):
    w = {}
    w_in = p["ev_w_in"][0]
    o1 = D_INNER
    o2 = o1 + CONV_DIM
    o3 = o2 + 2 * SSD_HEADS
    w["wz"] = w_in[:, :o1].astype(BF16)
    w["wx"] = w_in[:, o1:o2].astype(BF16)
    w["wdt"] = _pad_lanes(w_in[:, o2:o3]).astype(BF16)
    w["wu"] = w_in[:, o3:].astype(BF16)
    w["ln_in_g"], w["ln_in_b"] = _row(p["ln_in_g"]), _row(p["ln_in_b"])
    w["conv_w"] = p["ev_conv_w"][0].astype(F32)
    w["conv_b"] = _row(p["ev_conv_b"][0])
    w["dt_bias"] = _pad_lanes(jnp.concatenate([p["ev_dt_bias_f"][0], p["ev_dt_bias_b"][0]])[None, :])
    w["a_log"] = _pad_lanes(jnp.concatenate([p["ev_a_log_f"][0], p["ev_a_log_b"][0]])[None, :])
    w["d_skip"] = jnp.repeat(p["ev_d_skip"][0].astype(F32), D_INNER // SSD_HEADS)[None, :]
    w["gnorm"] = _row(p["ev_gnorm_w"][0])
    w_out = p["ev_w_out"][0]
    w["wo_ssd"] = w_out[:D_INNER].astype(BF16)
    w["wo_fn"] = w_out[D_INNER:].astype(BF16)
    w["ev_ln1_g"], w["ev_ln1_b"] = _row(p["ev_ln1_g"][0]), _row(p["ev_ln1_b"][0])
    w["ffn_wg"] = p["ev_ffn_wg"][0].astype(BF16)
    w["ffn_wu"] = p["ev_ffn_wu"][0].astype(BF16)
    w["ffn_wd"] = p["ev_ffn_wd"][0].astype(BF16)
    w["ev_ln2_g"], w["ev_ln2_b"] = _row(p["ev_ln2_g"][0]), _row(p["ev_ln2_b"][0])
    w["w_qkv"] = p["od_w_qkv"][0].astype(BF16)
    w["na_bias"] = _natten_bias_table(p["od_rpb"][0])
    w["od_w_out"] = p["od_w_out"][0].astype(BF16)
    w["od_ln1_g"], w["od_ln1_b"] = _row(p["od_ln1_g"][0]), _row(p["od_ln1_b"][0])
    w["router"] = _pad_lanes(p["od_router"][0])
    w["moe_wg"] = p["od_wg"][0].astype(BF16)
    w["moe_wu"] = p["od_wu"][0].astype(BF16)
    w["moe_wd"] = p["od_wd"][0].astype(BF16)
    w["od_ln2_g"], w["od_ln2_b"] = _row(p["od_ln2_g"][0]), _row(p["od_ln2_b"][0])
    return w


def _pick(n, pref):
    t = pref
    while n % t:
        t //= 2
    return t


def moe_plan(counts, n_rows):
    cnt = counts[0, :N_EXPERTS]
    padded = jnp.ceil(cnt / MOE_ROW_TILE) * MOE_ROW_TILE
    ends = jnp.cumsum(padded)
    offsets = (ends - padded).astype(jnp.int32)
    n_used = (ends[-1:] / MOE_ROW_TILE).astype(jnp.int32)
    starts = jnp.arange(n_rows // MOE_ROW_TILE, dtype=F32) * MOE_ROW_TILE
    tile_expert = jnp.minimum(jnp.sum(starts[:, None] >= ends[None, :], axis=1), N_EXPERTS - 1)
    return offsets, tile_expert.astype(jnp.int32), n_used


def moe_res_ln(h3, xa, xb, w):
    n = h3.shape[0]
    tm = _pick(n, 512)
    n_rows = 2 * n + N_EXPERTS * MOE_ROW_TILE
    meta, gate, counts = router(h3, w["router"], tm)
    offsets, tile_expert, n_used = moe_plan(counts, n_rows)
    pos = moe_positions(meta, offsets, _pick(n, 2048))
    xs_a = sc_dispatch_rows(xa, pos, n_rows)
    xs_b = sc_dispatch_rows(xb, pos, n_rows)
    ys_a, ys_b = moe_experts(xs_a, xs_b, tile_expert, n_used, w["moe_wg"], w["moe_wu"], w["moe_wd"], tf=512)
    flat = pos.reshape(1, 2 * n)
    ya = sc_gather_rows(ys_a, flat)
    yb = sc_gather_rows(ys_b, flat)
    return moe_combine_res_ln(h3, gate, ya, yb, w["od_ln2_g"], w["od_ln2_b"], tm)


def trunk(x, w):
    bsz, t, d = x.shape
    n = bsz * t
    tm = _pick(n, 512)
    nc = t // CHUNK
    xf = x.reshape(n, d)

    h0, z, xbc, dt_raw, u = ln_inproj(xf, w["ln_in_g"], w["ln_in_b"], w["wz"], w["wx"], w["wdt"], w["wu"], tm)
    xbc_act = conv_silu(xbc.reshape(bsz, t, CONV_DIM), w["conv_w"], w["conv_b"],
                        tt=_pick(t, 512), tc=1024)
    pcol, q = ssd_prep(dt_raw.reshape(bsz, t, LANES), w["dt_bias"], w["a_log"], cpp=_pick(nc, 4))
    cps = _pick(nc, 4)
    y_b = ssd_scan(xbc_act, pcol, q, rev=True, cps=cps)
    y_ssd = ssd_scan(xbc_act, pcol, q, rev=False, cps=cps,
                     fused=(y_b, z.reshape(bsz, t, D_INNER), w["d_skip"], w["gnorm"]))
    y_fn = fnet_mix(u.reshape(bsz, t, FNET_WIDTH))
    h1 = proj_res_ln([y_ssd.reshape(n, D_INNER), y_fn.reshape(n, FNET_WIDTH)],
                     [w["wo_ssd"], w["wo_fn"]], h0, w["ev_ln1_g"], w["ev_ln1_b"], tm)
    h2 = ffn_res_ln(h1, w["ffn_wg"], w["ffn_wu"], w["ffn_wd"], w["ev_ln2_g"], w["ev_ln2_b"], tm)

    qh, kh, vh = qkv_proj(h2, w["w_qkv"], tm)
    att = natten(qh.reshape(bsz, t, d), kh.reshape(bsz, t, d), vh.reshape(bsz, t, d), w["na_bias"])
    h3, xa, xb = proj_res_ln([att.reshape(n, d)], [w["od_w_out"]], h2, w["od_ln1_g"], w["od_ln1_b"], tm,
                             packed=True)
    return moe_res_ln(h3, xa, xb, w).reshape(bsz, t, d)


def kernel(x_prompt, x_sample, ln_in_g, ln_in_b, ev_w_in, ev_conv_w, ev_conv_b, ev_dt_bias_f, ev_dt_bias_b,
           ev_a_log_f, ev_a_log_b, ev_d_skip, ev_gnorm_w, ev_w_out, ev_ln1_g, ev_ln1_b, ev_ffn_wg, ev_ffn_wu,
           ev_ffn_wd, ev_ln2_g, ev_ln2_b, od_w_qkv, od_rpb, od_w_out, od_ln1_g, od_ln1_b, od_router, od_wg,
           od_wu, od_wd, od_ln2_g, od_ln2_b):
    params = dict(ln_in_g=ln_in_g, ln_in_b=ln_in_b, ev_w_in=ev_w_in, ev_conv_w=ev_conv_w, ev_conv_b=ev_conv_b,
                  ev_dt_bias_f=ev_dt_bias_f, ev_dt_bias_b=ev_dt_bias_b, ev_a_log_f=ev_a_log_f,
                  ev_a_log_b=ev_a_log_b, ev_d_skip=ev_d_skip, ev_gnorm_w=ev_gnorm_w, ev_w_out=ev_w_out,
                  ev_ln1_g=ev_ln1_g, ev_ln1_b=ev_ln1_b, ev_ffn_wg=ev_ffn_wg, ev_ffn_wu=ev_ffn_wu,
                  ev_ffn_wd=ev_ffn_wd, ev_ln2_g=ev_ln2_g, ev_ln2_b=ev_ln2_b, od_w_qkv=od_w_qkv, od_rpb=od_rpb,
                  od_w_out=od_w_out, od_ln1_g=od_ln1_g, od_ln1_b=od_ln1_b, od_router=od_router, od_wg=od_wg,
                  od_wu=od_wu, od_wd=od_wd, od_ln2_g=od_ln2_g, od_ln2_b=od_ln2_b)
    w = prepare_weights(params)
    return (trunk(x_prompt, w), trunk(x_sample, w))
```

```python
import functools
import math

import numpy as np
import jax
import jax.numpy as jnp
from jax import lax
from jax.experimental import pallas as pl
from jax.experimental.pallas import tpu as pltpu
from jax.experimental.pallas import tpu_sc as plsc

F32 = jnp.float32
BF16 = jnp.bfloat16

D_MODEL = 1024
GRID_W = 64
D_INNER = 2048
SSD_HEADS = 32
SSD_GROUPS = 4
HEADS_PER_GROUP = SSD_HEADS // SSD_GROUPS
GROUP_DIM = D_INNER // SSD_GROUPS
D_STATE = 128
D_CONV = 5
CHUNK = 128
CONV_DIM = D_INNER + 2 * SSD_GROUPS * D_STATE
FNET_GROUPS = 4
FNET_GROUP_DIM = 256
FNET_WIDTH = 1024
FNET_N2 = 128
FNET_SUB = 8
FNET_STEP = 16
NA_HEADS = 16
NA_HEAD_DIM = 64
NA_WIN_H = 8
NA_WIN_W = 16
D_FF_DENSE = 2816
N_EXPERTS = 8
D_FF_EXPERT = 3584
LN_EPS = 1e-5
RMS_EPS = 1e-5
DEPTH = 2
ALPHA = (2 * DEPTH) ** 0.25
LANES = 128
HALO = 16
NEG_BIG = -1e30
VMEM_LIMIT = 56 * 1024 * 1024
PACK_W = D_MODEL // 4
SC_WINDOW = 128
MOE_ROW_TILE = 512


def _cparams(sem):
    return pltpu.CompilerParams(dimension_semantics=sem, vmem_limit_bytes=VMEM_LIMIT)


def _const_spec(shape):
    nd = len(shape)
    return pl.BlockSpec(shape, lambda *_: (0,) * nd)


def _dot(a, b):
    return jnp.dot(a, b, preferred_element_type=F32)


def _layer_norm(xf, g, b):
    mu = jnp.mean(xf, axis=-1, keepdims=True)
    xc = xf - mu
    var = jnp.mean(xc * xc, axis=-1, keepdims=True)
    return xc * lax.rsqrt(var + LN_EPS) * g + b


def _silu(x):
    return x * jax.nn.sigmoid(x)


def _ln_inproj_kernel(x_ref, g_ref, b_ref, wz_ref, wx_ref, wdt_ref, wu_ref,
                      h_ref, z_ref, xbc_ref, dt_ref, u_ref):
    h = _layer_norm(x_ref[...], g_ref[...], b_ref[...])
    h_ref[...] = h
    hb = h.astype(BF16)
    for j in range(0, D_INNER, 512):
        z_ref[:, j:j + 512] = _dot(hb, wz_ref[:, j:j + 512]).astype(BF16)
    for j in range(0, CONV_DIM, 512):
        xbc_ref[:, j:j + 512] = _dot(hb, wx_ref[:, j:j + 512]).astype(BF16)
    dt_ref[...] = _dot(hb, wdt_ref[...])
    for j in range(0, FNET_WIDTH, 512):
        u_ref[:, j:j + 512] = _dot(hb, wu_ref[:, j:j + 512])


def ln_inproj(x, g, b, wz, wx, wdt, wu, tm):
    n = x.shape[0]
    row = lambda w: pl.BlockSpec((tm, w), lambda i: (i, 0))
    return pl.pallas_call(
        _ln_inproj_kernel,
        grid=(n // tm,),
        in_specs=[row(D_MODEL), _const_spec((1, D_MODEL)), _const_spec((1, D_MODEL)),
                  _const_spec(wz.shape), _const_spec(wx.shape), _const_spec(wdt.shape),
                  _const_spec(wu.shape)],
        out_specs=[row(D_MODEL), row(D_INNER), row(CONV_DIM), row(LANES), row(FNET_WIDTH)],
        out_shape=[jax.ShapeDtypeStruct((n, D_MODEL), F32),
                   jax.ShapeDtypeStruct((n, D_INNER), BF16),
                   jax.ShapeDtypeStruct((n, CONV_DIM), BF16),
                   jax.ShapeDtypeStruct((n, LANES), F32),
                   jax.ShapeDtypeStruct((n, FNET_WIDTH), F32)],
        compiler_params=_cparams(("parallel",)),
        name="ln_inproj",
    )(x, g, b, wz, wx, wdt, wu)


def _conv_silu_kernel(prev_ref, main_ref, next_ref, shift_ref, w_ref, b_ref, o_ref, scr, *, tt, nt):
    i = pl.program_id(1)
    zero = jnp.zeros((HALO, scr.shape[1]), BF16)
    scr[0:HALO, :] = prev_ref[0]
    scr[HALO:HALO + tt, :] = main_ref[0]
    scr[HALO + tt:2 * HALO + tt, :] = next_ref[0]

    @pl.when(i == 0)
    def _():
        scr[0:HALO, :] = zero

    @pl.when(i == nt - 1)
    def _():
        scr[HALO + tt:2 * HALO + tt, :] = zero

    half = D_CONV // 2
    strip = 2 * LANES
    for r0 in range(0, tt, CHUNK):
        for c0 in range(0, scr.shape[1], strip):
            cols = slice(c0, c0 + strip)
            ext = scr[r0:r0 + CHUNK + 2 * HALO, cols]
            acc = b_ref[:, cols] + w_ref[half:half + 1, cols] * ext[HALO:HALO + CHUNK].astype(F32)
            for j, k in enumerate([k for k in range(D_CONV) if k != half]):
                sh = _dot(shift_ref[j * CHUNK:(j + 1) * CHUNK, :], ext)
                acc = acc + w_ref[k:k + 1, cols] * sh
            o_ref[0, r0:r0 + CHUNK, cols] = _silu(acc).astype(BF16)


def _conv_shift_matrix():
    taps = [k for k in range(D_CONV) if k != D_CONV // 2]
    s = np.zeros((len(taps) * CHUNK, CHUNK + 2 * HALO), np.float32)
    for j, k in enumerate(taps):
        for r in range(CHUNK):
            s[j * CHUNK + r, HALO + r + k - D_CONV // 2] = 1.0
    return s


def conv_silu(xbc, w, b, tt, tc):
    bsz, t, c = xbc.shape
    nt = t // tt
    hb = tt // HALO
    kern = functools.partial(_conv_silu_kernel, tt=tt, nt=nt)
    shift = jnp.asarray(_conv_shift_matrix(), BF16)
    return pl.pallas_call(
        kern,
        grid=(bsz, nt, c // tc),
        in_specs=[
            pl.BlockSpec((1, HALO, tc), lambda bi, i, ci: (bi, jnp.maximum(i * hb - 1, 0), ci)),
            pl.BlockSpec((1, tt, tc), lambda bi, i, ci: (bi, i, ci)),
            pl.BlockSpec((1, HALO, tc), lambda bi, i, ci: (bi, jnp.minimum((i + 1) * hb, t // HALO - 1), ci)),
            _const_spec(shift.shape),
            pl.BlockSpec((D_CONV, tc), lambda bi, i, ci: (0, ci)),
            pl.BlockSpec((1, tc), lambda bi, i, ci: (0, ci)),
        ],
        out_specs=pl.BlockSpec((1, tt, tc), lambda bi, i, ci: (bi, i, ci)),
        out_shape=jax.ShapeDtypeStruct((bsz, t, c), BF16),
        scratch_shapes=[pltpu.VMEM((tt + 2 * HALO, tc), BF16)],
        compiler_params=_cparams(("parallel", "parallel", "parallel")),
        name="conv_silu",
    )(xbc, xbc, xbc, shift, w, b)


def _split3(x):
    hi = x.astype(BF16)
    r1 = x - hi.astype(F32)
    mid = r1.astype(BF16)
    lo = (r1 - mid.astype(F32)).astype(BF16)
    return hi, mid, lo


def _ssd_prep_kernel(dt_ref, bias_ref, alog_ref, tri_ref, trit_ref, sel_ref, pcol_ref, q_ref, *, cpp):
    lane = lax.broadcasted_iota(jnp.int32, (CHUNK, LANES), 1)
    fwd = lane < SSD_HEADS
    a_coef = -jnp.exp(alog_ref[...])
    for c in range(cpp):
        rows = slice(c * CHUNK, (c + 1) * CHUNK)
        raw = dt_ref[0, rows, :] + bias_ref[...]
        dt = jnp.maximum(raw, 0.0) + jnp.log1p(jnp.exp(-jnp.abs(raw)))
        a = dt * a_coef
        cs_f = jnp.dot(tri_ref[...], a, precision=lax.Precision.HIGHEST, preferred_element_type=F32)
        cs_b = jnp.dot(trit_ref[...], a, precision=lax.Precision.HIGHEST, preferred_element_type=F32)
        cs = jnp.where(fwd, cs_f, cs_b)
        end = jnp.where(fwd[0:1], cs[CHUNK - 1:CHUNK, :], cs[0:1, :])
        wend = jnp.exp(end - cs) * dt
        hi, mid, lo = _split3(cs)
        src = jnp.concatenate([hi, mid, lo, wend.astype(BF16)], axis=1)
        pc = _dot(src, sel_ref[...]).astype(BF16)
        pcol_ref[0, 0, rows, :] = pc[:, :SSD_GROUPS * LANES]
        pcol_ref[1, 0, rows, :] = pc[:, SSD_GROUPS * LANES:]
        cs_t = cs.T
        dt_t = dt.T
        for d in range(2):
            for g in range(SSD_GROUPS):
                r0 = d * SSD_HEADS + g * HEADS_PER_GROUP
                q_ref[d, 0, g, c, 0:HEADS_PER_GROUP, :] = cs_t[r0:r0 + HEADS_PER_GROUP, :]
                q_ref[d, 0, g, c, HEADS_PER_GROUP:2 * HEADS_PER_GROUP, :] = dt_t[r0:r0 + HEADS_PER_GROUP, :]


def _prep_select_matrix():
    sel = np.zeros((4 * LANES, 2 * SSD_GROUPS * LANES), np.float32)
    for d in range(2):
        for g in range(SSD_GROUPS):
            for q in range(4):
                for i in range(HEADS_PER_GROUP):
                    src = q * LANES + d * SSD_HEADS + g * HEADS_PER_GROUP + i
                    dst = d * SSD_GROUPS * LANES + g * LANES + q * HEADS_PER_GROUP + i
                    sel[src, dst] = 1.0
    return sel


def ssd_prep(dt_raw, bias_row, alog_row, cpp):
    bsz, t, _ = dt_raw.shape
    nc = t // CHUNK
    tri = np.tril(np.ones((CHUNK, CHUNK), np.float32))
    kern = functools.partial(_ssd_prep_kernel, cpp=cpp)
    return pl.pallas_call(
        kern,
        grid=(bsz, nc // cpp),
        in_specs=[pl.BlockSpec((1, cpp * CHUNK, LANES), lambda b, j: (b, j, 0)),
                  _const_spec((1, LANES)), _const_spec((1, LANES)),
                  _const_spec((CHUNK, CHUNK)), _const_spec((CHUNK, CHUNK)),
                  _const_spec((4 * LANES, 2 * SSD_GROUPS * LANES))],
        out_specs=[pl.BlockSpec((2, 1, cpp * CHUNK, SSD_GROUPS * LANES), lambda b, j: (0, b, j, 0)),
                   pl.BlockSpec((2, 1, SSD_GROUPS, cpp, 2 * HEADS_PER_GROUP, CHUNK),
                                lambda b, j: (0, b, 0, j, 0, 0))],
        out_shape=[jax.ShapeDtypeStruct((2, bsz, t, SSD_GROUPS * LANES), BF16),
                   jax.ShapeDtypeStruct((2, bsz, SSD_GROUPS, nc, 2 * HEADS_PER_GROUP, CHUNK), F32)],
        compiler_params=_cparams(("parallel", "parallel")),
        name="ssd_prep",
    )(dt_raw, bias_row, alog_row, jnp.asarray(tri), jnp.asarray(tri.T),
      jnp.asarray(_prep_select_matrix(), BF16))


def _expand_matrix():
    e = np.zeros((LANES, HEADS_PER_GROUP * LANES + GROUP_DIM), np.float32)
    for q in range(3):
        for h in range(HEADS_PER_GROUP):
            e[q * HEADS_PER_GROUP + h, h * LANES:(h + 1) * LANES] = 1.0
    for h in range(HEADS_PER_GROUP):
        e[3 * HEADS_PER_GROUP + h,
          HEADS_PER_GROUP * LANES + h * 64:HEADS_PER_GROUP * LANES + (h + 1) * 64] = 1.0
    return e


def _ssd_kernel(*refs, rev, cps, fuse):
    if fuse:
        xs_ref, b_ref, c_ref, p_ref, q_ref, e_ref, yb_ref, z_ref, dsk_ref, gn_ref, o_ref, st_ref = refs
    else:
        xs_ref, b_ref, c_ref, p_ref, q_ref, e_ref, o_ref, st_ref = refs

    @pl.when(pl.program_id(2) == 0)
    def _():
        st_ref[...] = jnp.zeros_like(st_ref)

    row = lax.broadcasted_iota(jnp.int32, (CHUNK, CHUNK), 0)
    col = lax.broadcasted_iota(jnp.int32, (CHUNK, CHUNK), 1)
    mask = (col >= row) if rev else (col <= row)
    lo_half = lax.broadcasted_iota(jnp.int32, (CHUNK, LANES), 1) < 64
    nb = HEADS_PER_GROUP * LANES
    end = 0 if rev else CHUNK - 1
    order = range(cps - 1, -1, -1) if rev else range(cps)
    for c in order:
        rows = slice(c * CHUNK, (c + 1) * CHUNK)
        xf = xs_ref[0, rows, :].astype(F32)
        bm = b_ref[0, rows, :]
        cm = c_ref[0, rows, :]
        ex = _dot(p_ref[0, 0, rows, :], e_ref[...])
        qv = q_ref[0, 0, 0, c]
        cb = lax.dot_general(cm, bm, (((1,), (1,)), ((), ())), preferred_element_type=F32)
        cs_parts = []
        y_parts = []
        for j in range(HEADS_PER_GROUP // 2):
            h1, h2 = 2 * j, 2 * j + 1
            c1 = ex[:, h1 * LANES:(h1 + 1) * LANES]
            c2 = ex[:, h2 * LANES:(h2 + 1) * LANES]
            cs_parts.append(jnp.where(lo_half, c1, c2))
            w1 = jnp.exp(jnp.where(mask, c1 - qv[h1:h1 + 1, :], -jnp.inf)) * (
                cb * qv[HEADS_PER_GROUP + h1:HEADS_PER_GROUP + h1 + 1, :])
            w2 = jnp.exp(jnp.where(mask, c2 - qv[h2:h2 + 1, :], -jnp.inf)) * (
                cb * qv[HEADS_PER_GROUP + h2:HEADS_PER_GROUP + h2 + 1, :])
            wp = jnp.concatenate([w1, w2], axis=1).astype(BF16)
            xp = xf[:, j * LANES:(j + 1) * LANES]
            rhs = jnp.concatenate([jnp.where(lo_half, xp, 0.0), jnp.where(lo_half, 0.0, xp)],
                                  axis=0).astype(BF16)
            y_parts.append(_dot(wp, rhs))
        ydiag = jnp.concatenate(y_parts, axis=1)
        expcs = jnp.exp(jnp.concatenate(cs_parts, axis=1))
        decay = expcs[end:end + 1, :]
        st = st_ref[...]
        yoff = _dot(cm, st.astype(BF16)) * expcs
        xsw = (xf * ex[:, nb:nb + GROUP_DIM]).astype(BF16)
        st_ref[...] = st * decay + lax.dot_general(bm, xsw, (((0,), (0,)), ((), ())),
                                                   preferred_element_type=F32)
        y = ydiag + yoff
        if fuse:
            y = y + yb_ref[0, rows, :].astype(F32) + dsk_ref[...] * xf
            y = y * _silu(z_ref[0, rows, :].astype(F32))
            ms = jnp.mean(y * y, axis=-1, keepdims=True)
            o_ref[0, rows, :] = (y * lax.rsqrt(ms + RMS_EPS) * gn_ref[...]).astype(BF16)
        else:
            o_ref[0, rows, :] = y.astype(BF16)


def ssd_scan(xbc_act, pcol, q, rev, cps, fused=None):
    bsz, t, _ = xbc_act.shape
    ncb = t // (cps * CHUNK)
    r = cps * CHUNK
    d = 1 if rev else 0
    cidx = (lambda j: ncb - 1 - j) if rev else (lambda j: j)
    e = jnp.asarray(_expand_matrix(), BF16)
    b0 = D_INNER // LANES
    c0 = b0 + SSD_GROUPS
    in_specs = [
        pl.BlockSpec((1, r, GROUP_DIM), lambda b, g, j: (b, cidx(j), g)),
        pl.BlockSpec((1, r, LANES), lambda b, g, j: (b, cidx(j), b0 + g)),
        pl.BlockSpec((1, r, LANES), lambda b, g, j: (b, cidx(j), c0 + g)),
        pl.BlockSpec((1, 1, r, LANES), lambda b, g, j: (d, b, cidx(j), g)),
        pl.BlockSpec((1, 1, 1, cps, 2 * HEADS_PER_GROUP, CHUNK), lambda b, g, j: (d, b, g, cidx(j), 0, 0)),
        _const_spec(e.shape),
    ]
    args = [xbc_act, xbc_act, xbc_act, pcol, q, e]
    if fused is not None:
        yb, z, dsk, gn = fused
        in_specs += [
            pl.BlockSpec((1, r, GROUP_DIM), lambda b, g, j: (b, cidx(j), g)),
            pl.BlockSpec((1, r, GROUP_DIM), lambda b, g, j: (b, cidx(j), g)),
            pl.BlockSpec((1, GROUP_DIM), lambda b, g, j: (0, g)),
            pl.BlockSpec((1, GROUP_DIM), lambda b, g, j: (0, g)),
        ]
        args += [yb, z, dsk, gn]
    kern = functools.partial(_ssd_kernel, rev=rev, cps=cps, fuse=fused is not None)
    return pl.pallas_call(
        kern,
        grid=(bsz, SSD_GROUPS, ncb),
        in_specs=in_specs,
        out_specs=pl.BlockSpec((1, r, GROUP_DIM), lambda b, g, j: (b, cidx(j), g)),
        out_shape=jax.ShapeDtypeStruct((bsz, t, D_INNER), BF16),
        scratch_shapes=[pltpu.VMEM((D_STATE, GROUP_DIM), F32)],
        compiler_params=_cparams(("parallel", "parallel", "arbitrary")),
        name="ssd_bwd" if rev else "ssd_fwd",
    )(*args)


def _dft_cos_sin(n):
    k = np.arange(n, dtype=np.float64)
    ang = 2.0 * np.pi * np.outer(k, k) / n
    return np.cos(ang), np.sin(ang)


def _fnet_s1_kernel(u_ref, g_ref, cos_ref, sin_ref, o_ref, *, n1):
    c = FNET_WIDTH
    rows = n1 * FNET_SUB
    re_parts, im_parts = [], []
    for h in range(FNET_STEP // FNET_SUB):
        x = u_ref[0, :, h * FNET_SUB:(h + 1) * FNET_SUB, :].reshape(rows, c).astype(BF16)
        a = _dot(g_ref[...], x)
        ar = a[:rows]
        ai = a[rows:]
        ct = jnp.concatenate([cos_ref[h]] * (c // LANES), axis=1)
        st = jnp.concatenate([sin_ref[h]] * (c // LANES), axis=1)
        re_parts.append((ar * ct + ai * st).reshape(n1, FNET_SUB, c))
        im_parts.append((ai * ct - ar * st).reshape(n1, FNET_SUB, c))
    o_ref[0, 0] = jnp.concatenate(re_parts, axis=1).astype(BF16)
    o_ref[0, 1] = jnp.concatenate(im_parts, axis=1).astype(BF16)


def _fnet_s2_kernel(br_ref, bi_ref, f2_ref, wc_ref, o_ref, *, ks):
    n2 = FNET_N2
    c = FNET_WIDTH
    gs = []
    for kk in range(ks):
        rhs = jnp.concatenate([br_ref[0, 0, kk], bi_ref[0, 0, kk]], axis=0)
        gs.append(_dot(f2_ref[...], rhs).astype(BF16))
    for grp in range(FNET_GROUPS):
        cols = slice(grp * FNET_GROUP_DIM, (grp + 1) * FNET_GROUP_DIM)
        lhs = jnp.concatenate([jnp.concatenate([g[:n2, cols], g[n2:, cols]], axis=1) for g in gs], axis=0)
        res = _dot(lhs, wc_ref[...]).astype(BF16)
        for kk in range(ks):
            o_ref[0, :, kk * c + grp * FNET_GROUP_DIM:kk * c + (grp + 1) * FNET_GROUP_DIM] = (
                res[kk * n2:(kk + 1) * n2])


def fnet_mix(u):
    bsz, t, c = u.shape
    n2 = FNET_N2
    n1 = t // n2
    sub, step = FNET_SUB, FNET_STEP
    c1, s1 = _dft_cos_sin(n1)
    c2, s2 = _dft_cos_sin(n2)
    cc, sc = _dft_cos_sin(FNET_GROUP_DIM)
    g1 = jnp.asarray(np.kron(np.concatenate([c1, -s1], axis=0), np.eye(sub)), BF16)
    f2 = jnp.asarray(np.block([[c2, s2], [-s2, c2]]), BF16)
    scale = 1.0 / math.sqrt(t * FNET_GROUP_DIM)
    wc = jnp.asarray(np.concatenate([cc, sc], axis=0) * scale, BF16)
    k1 = np.arange(n1)[None, :, None]
    pos = (np.arange(n2 // sub)[:, None, None] * sub + np.arange(sub)[None, None, :])
    ang = (2.0 * np.pi * k1 * pos / t).reshape(n2 // sub, n1 * sub)
    tw_cos = jnp.asarray(np.repeat(np.cos(ang)[:, :, None], LANES, axis=2), F32)
    tw_sin = jnp.asarray(np.repeat(np.sin(ang)[:, :, None], LANES, axis=2), F32)

    hs = step // sub
    bv = pl.pallas_call(
        functools.partial(_fnet_s1_kernel, n1=n1),
        grid=(bsz, n2 // step),
        in_specs=[pl.BlockSpec((1, n1, step, c), lambda b, j: (b, 0, j, 0)),
                  _const_spec(g1.shape),
                  pl.BlockSpec((hs, n1 * sub, LANES), lambda b, j: (j, 0, 0)),
                  pl.BlockSpec((hs, n1 * sub, LANES), lambda b, j: (j, 0, 0))],
        out_specs=pl.BlockSpec((1, 2, n1, step, c), lambda b, j: (b, 0, 0, j, 0)),
        out_shape=jax.ShapeDtypeStruct((bsz, 2, n1, n2, c), BF16),
        compiler_params=_cparams(("parallel", "parallel")),
        name="fnet_stage1",
    )(u.reshape(bsz, n1, n2, c), g1, tw_cos, tw_sin)

    ks = _pick(n1, 4)
    y = pl.pallas_call(
        functools.partial(_fnet_s2_kernel, ks=ks),
        grid=(bsz, n1 // ks),
        in_specs=[pl.BlockSpec((1, 1, ks, n2, c), lambda b, k: (b, 0, k, 0, 0)),
                  pl.BlockSpec((1, 1, ks, n2, c), lambda b, k: (b, 1, k, 0, 0)),
                  _const_spec((2 * n2, 2 * n2)),
                  _const_spec((2 * FNET_GROUP_DIM, FNET_GROUP_DIM))],
        out_specs=pl.BlockSpec((1, n2, ks * c), lambda b, k: (b, 0, k)),
        out_shape=jax.ShapeDtypeStruct((bsz, n2, n1 * c), BF16),
        compiler_params=_cparams(("parallel", "parallel")),
        name="fnet_stage2",
    )(bv, bv, f2, wc)
    return y.reshape(bsz, t, c)


def _pack_bf16_pair(a, b):
    lo = lax.bitcast_convert_type(a.astype(BF16).astype(F32), jnp.uint32)
    hi = lax.bitcast_convert_type(b.astype(BF16).astype(F32), jnp.uint32)
    word = lax.shift_right_logical(lo, jnp.uint32(16)) | hi
    return lax.bitcast_convert_type(word, jnp.int32)


def _unpack_bf16_pair(word):
    w = lax.bitcast_convert_type(word, jnp.uint32)
    a = lax.bitcast_convert_type(lax.shift_left(w, jnp.uint32(16)), F32)
    b = lax.bitcast_convert_type(w & jnp.uint32(0xFFFF0000), F32)
    return a, b


def _pack_row(x, a_ref, b_ref):
    a_ref[...] = _pack_bf16_pair(x[:, 0:PACK_W], x[:, PACK_W:2 * PACK_W])
    b_ref[...] = _pack_bf16_pair(x[:, 2 * PACK_W:3 * PACK_W], x[:, 3 * PACK_W:4 * PACK_W])


def _unpack_row(a_word, b_word):
    x0, x1 = _unpack_bf16_pair(a_word)
    x2, x3 = _unpack_bf16_pair(b_word)
    return jnp.concatenate([x0, x1, x2, x3], axis=1)


def _proj_res_ln_kernel(*refs, n_in, packed):
    xs = refs[:n_in]
    ws = refs[n_in:2 * n_in]
    h_ref, g_ref, b_ref, o_ref = refs[2 * n_in:2 * n_in + 4]
    acc = _dot(xs[0][...], ws[0][...])
    for x_ref, w_ref in zip(xs[1:], ws[1:]):
        acc = acc + _dot(x_ref[...], w_ref[...])
    out = _layer_norm(ALPHA * h_ref[...] + acc, g_ref[...], b_ref[...])
    o_ref[...] = out
    if packed:
        _pack_row(out, refs[-2], refs[-1])


def proj_res_ln(xs, ws, h, g, b, tm, packed=False):
    n = h.shape[0]
    n_in = len(xs)
    in_specs = [pl.BlockSpec((tm, x.shape[1]), lambda i: (i, 0)) for x in xs]
    in_specs += [_const_spec(w.shape) for w in ws]
    in_specs += [pl.BlockSpec((tm, D_MODEL), lambda i: (i, 0)),
                 _const_spec((1, D_MODEL)), _const_spec((1, D_MODEL))]
    out_specs = [pl.BlockSpec((tm, D_MODEL), lambda i: (i, 0))]
    out_shape = [jax.ShapeDtypeStruct((n, D_MODEL), F32)]
    if packed:
        out_specs += [pl.BlockSpec((tm, PACK_W), lambda i: (i, 0))] * 2
        out_shape += [jax.ShapeDtypeStruct((n, PACK_W), jnp.int32)] * 2
    res = pl.pallas_call(
        functools.partial(_proj_res_ln_kernel, n_in=n_in, packed=packed),
        grid=(n // tm,),
        in_specs=in_specs,
        out_specs=out_specs,
        out_shape=out_shape,
        compiler_params=_cparams(("parallel",)),
        name="proj_res_ln",
    )(*xs, *ws, h, g, b)
    return res if packed else res[0]


def _ffn_res_ln_kernel(h_ref, wg_ref, wu_ref, wd_ref, g_ref, b_ref, o_ref, act_ref, *, tf):
    h = h_ref[...]
    hb = h.astype(BF16)
    for j in range(0, D_FF_DENSE, tf):
        gate = _dot(hb, wg_ref[:, j:j + tf])
        up = _dot(hb, wu_ref[:, j:j + tf])
        act_ref[:, j:j + tf] = (_silu(gate) * up).astype(BF16)
    y = _dot(act_ref[...], wd_ref[...])
    o_ref[...] = _layer_norm(ALPHA * h + y, g_ref[...], b_ref[...])


def ffn_res_ln(h, wg, wu, wd, g, b, tm, tf=256):
    n = h.shape[0]
    return pl.pallas_call(
        functools.partial(_ffn_res_ln_kernel, tf=tf),
        grid=(n // tm,),
        in_specs=[pl.BlockSpec((tm, D_MODEL), lambda i: (i, 0)),
                  _const_spec(wg.shape), _const_spec(wu.shape), _const_spec(wd.shape),
                  _const_spec((1, D_MODEL)), _const_spec((1, D_MODEL))],
        out_specs=pl.BlockSpec((tm, D_MODEL), lambda i: (i, 0)),
        out_shape=jax.ShapeDtypeStruct((n, D_MODEL), F32),
        scratch_shapes=[pltpu.VMEM((tm, D_FF_DENSE), BF16)],
        compiler_params=_cparams(("parallel",)),
        name="ffn_res_ln",
    )(h, wg, wu, wd, g, b)


def _qkv_kernel(h_ref, w_ref, q_ref, k_ref, v_ref):
    hb = h_ref[...].astype(BF16)
    d = D_MODEL
    scale = NA_HEAD_DIM ** -0.5
    for j in range(0, d, 512):
        q_ref[:, j:j + 512] = (_dot(hb, w_ref[:, j:j + 512]) * scale).astype(BF16)
        k_ref[:, j:j + 512] = _dot(hb, w_ref[:, d + j:d + j + 512]).astype(BF16)
        v_ref[:, j:j + 512] = _dot(hb, w_ref[:, 2 * d + j:2 * d + j + 512]).astype(BF16)


def qkv_proj(h, w, tm):
    n = h.shape[0]
    row = pl.BlockSpec((tm, D_MODEL), lambda i: (i, 0))
    sds = jax.ShapeDtypeStruct((n, D_MODEL), BF16)
    return pl.pallas_call(
        _qkv_kernel,
        grid=(n // tm,),
        in_specs=[row, _const_spec(w.shape)],
        out_specs=[row, row, row],
        out_shape=[sds, sds, sds],
        compiler_params=_cparams(("parallel",)),
        name="qkv_proj",
    )(h, w)


def _natten_kernel(q_ref, k_ref, v_ref, bias_ref, o_ref, *, rows, hw, rq):
    nk = NA_WIN_H * GRID_W
    npair = hw // LANES
    lo_half = lax.broadcasted_iota(jnp.int32, (GRID_W, LANES), 1) < NA_HEAD_DIM
    starts = []
    scores = []
    for r in range(rq):
        i = pl.program_id(2) * rq + r
        r0 = jnp.clip(i - NA_WIN_H // 2, 0, rows - NA_WIN_H)
        case = i - r0
        starts.append(pl.multiple_of(r0 * GRID_W, GRID_W))
        for p in range(npair):
            cols = slice(p * LANES, (p + 1) * LANES)
            qp = q_ref[0, r * GRID_W:(r + 1) * GRID_W, cols].astype(F32)
            qbd = jnp.concatenate([jnp.where(lo_half, qp, 0.0), jnp.where(lo_half, 0.0, qp)],
                                  axis=0).astype(BF16)
            s = lax.dot_general(qbd, k_ref[0, pl.ds(starts[r], nk), cols], (((1,), (1,)), ((), ())),
                                preferred_element_type=F32)
            scores.append(s + bias_ref[case, p])
    s_all = jnp.concatenate(scores, axis=0)
    e_f32 = jnp.exp(s_all - jnp.max(s_all, axis=-1, keepdims=True))
    inv_all = 1.0 / jnp.sum(e_f32, axis=-1, keepdims=True)
    e_all = e_f32.astype(BF16)
    for r in range(rq):
        for p in range(npair):
            cols = slice(p * LANES, (p + 1) * LANES)
            c0 = (r * npair + p) * 2 * GRID_W
            e = e_all[c0:c0 + 2 * GRID_W]
            o2 = _dot(e, v_ref[0, pl.ds(starts[r], nk), cols]) * inv_all[c0:c0 + 2 * GRID_W]
            o_ref[0, r * GRID_W:(r + 1) * GRID_W, cols] = jnp.where(
                lo_half, o2[:GRID_W], o2[GRID_W:]).astype(BF16)


def _natten_bias_table(rpb):
    j = np.arange(GRID_W)[:, None]
    kc = np.arange(GRID_W)[None, :]
    cstart = np.clip(j - NA_WIN_W // 2, 0, GRID_W - NA_WIN_W)
    valid = (kc >= cstart) & (kc < cstart + NA_WIN_W)
    coff = np.clip(kc - j + NA_WIN_W - 1, 0, 2 * NA_WIN_W - 2)
    dd = np.arange(NA_WIN_H)[:, None]
    a = np.arange(NA_WIN_H)[None, :]
    roff = a + (NA_WIN_H - 1) - dd
    rows = rpb.astype(F32)[:, roff, :]
    onehot = (coff[None, :, :] == np.arange(2 * NA_WIN_W - 1)[:, None, None]) & valid[None]
    t = jnp.einsum("hdac,cjk->dhjak", rows, jnp.asarray(onehot, F32),
                   precision=lax.Precision.HIGHEST)
    t = jnp.where(valid[None, None, :, None, :], t, NEG_BIG)
    return t.reshape(NA_WIN_H, NA_HEADS // 2, 2 * GRID_W, NA_WIN_H * GRID_W)


def natten(q, k, v, bias, hw=512, rq=4):
    bsz, t, d = q.shape
    rows = t // GRID_W
    nh = d // hw
    pp = hw // LANES
    resident = dict(pipeline_mode=pl.Buffered(1))
    return pl.pallas_call(
        functools.partial(_natten_kernel, rows=rows, hw=hw, rq=rq),
        grid=(bsz, nh, rows // rq),
        in_specs=[pl.BlockSpec((1, rq * GRID_W, hw), lambda b, hh, i: (b, i, hh)),
                  pl.BlockSpec((1, t, hw), lambda b, hh, i: (b, 0, hh), **resident),
                  pl.BlockSpec((1, t, hw), lambda b, hh, i: (b, 0, hh), **resident),
                  pl.BlockSpec((NA_WIN_H, pp, 2 * GRID_W, NA_WIN_H * GRID_W),
                               lambda b, hh, i: (0, hh, 0, 0), **resident)],
        out_specs=pl.BlockSpec((1, rq * GRID_W, hw), lambda b, hh, i: (b, i, hh)),
        out_shape=jax.ShapeDtypeStruct((bsz, t, d), BF16),
        compiler_params=_cparams(("parallel", "parallel", "arbitrary")),
        name="natten",
    )(q, k, v, bias)


def _lane_pick(x, lane, idx):
    return jnp.sum(jnp.where(lane == idx, x, 0.0), axis=-1, keepdims=True)


def _router_kernel(h_ref, w_ref, ltri_ref, meta_ref, gate_ref, cnt_ref, run_ref):
    @pl.when(pl.program_id(0) == 0)
    def _():
        run_ref[...] = jnp.zeros_like(run_ref)

    logits = _dot(h_ref[...].astype(BF16), w_ref[...])
    lane = lax.broadcasted_iota(jnp.int32, logits.shape, 1)
    logits = jnp.where(lane < N_EXPERTS, logits, -jnp.inf)
    m1 = jnp.max(logits, axis=-1, keepdims=True)
    i1 = jnp.min(jnp.where(logits == m1, lane, LANES), axis=-1, keepdims=True)
    rest = jnp.where(lane == i1, -jnp.inf, logits)
    m2 = jnp.max(rest, axis=-1, keepdims=True)
    i2 = jnp.min(jnp.where(rest == m2, lane, LANES), axis=-1, keepdims=True)
    e2 = jnp.exp(m2 - m1)
    inv = 1.0 / (1.0 + e2)
    gate_ref[...] = jnp.where(lane == 0, inv, jnp.where(lane == 1, e2 * inv, 0.0))

    sel = jnp.where(lane == i1, 1.0, jnp.where(lane == i2, 1.0, 0.0))
    before = _dot(ltri_ref[...], sel.astype(BF16)) + run_ref[...]
    rank1 = _lane_pick(before, lane, i1)
    rank2 = _lane_pick(before, lane, i2)
    run_ref[...] += jnp.sum(sel, axis=0, keepdims=True)
    cnt_ref[...] = jnp.broadcast_to(run_ref[...], cnt_ref.shape)
    packed = jnp.where(lane == 0, rank1, jnp.where(lane == 1, rank2, jnp.where(
        lane == 2, i1.astype(F32), jnp.where(lane == 3, i2.astype(F32), 0.0))))
    meta_ref[...] = packed.T[0:8, :]


def router(h, w_pad, tm):
    n = h.shape[0]
    ltri = jnp.asarray(np.tril(np.ones((tm, tm), np.float32), -1), BF16)
    return pl.pallas_call(
        _router_kernel,
        grid=(n // tm,),
        in_specs=[pl.BlockSpec((tm, D_MODEL), lambda i: (i, 0)), _const_spec(w_pad.shape),
                  _const_spec((tm, tm))],
        out_specs=[pl.BlockSpec((8, tm), lambda i: (0, i)),
                   pl.BlockSpec((tm, LANES), lambda i: (i, 0)),
                   _const_spec((8, LANES))],
        out_shape=[jax.ShapeDtypeStruct((8, n), F32),
                   jax.ShapeDtypeStruct((n, LANES), F32),
                   jax.ShapeDtypeStruct((8, LANES), F32)],
        scratch_shapes=[pltpu.VMEM((1, LANES), F32)],
        compiler_params=_cparams(("arbitrary",)),
        name="router",
    )(h, w_pad, ltri)


def _moe_pos_kernel(off_ref, meta_ref, pos_ref):
    m = meta_ref[...]
    for k in range(2):
        rank = m[k:k + 1, :].astype(jnp.int32)
        expert = m[2 + k:3 + k, :].astype(jnp.int32)
        base = jnp.zeros_like(rank)
        for e in range(N_EXPERTS):
            base = jnp.where(expert == e, off_ref[e], base)
        pos_ref[k:k + 1, :] = rank + base


def moe_positions(meta, offsets, tn):
    n = meta.shape[1]
    return pl.pallas_call(
        _moe_pos_kernel,
        grid_spec=pltpu.PrefetchScalarGridSpec(
            num_scalar_prefetch=1,
            grid=(n // tn,),
            in_specs=[pl.BlockSpec((8, tn), lambda i, off: (0, i))],
            out_specs=pl.BlockSpec((2, tn), lambda i, off: (0, i)),
        ),
        out_shape=jax.ShapeDtypeStruct((2, n), jnp.int32),
        compiler_params=_cparams(("parallel",)),
        name="moe_positions",
    )(offsets, meta)


def _sc_mesh():
    return plsc.VectorSubcoreMesh(core_axis_name="core", subcore_axis_name="subcore")


def sc_dispatch_rows(x, pos, out_rows):
    n, dim = x.shape

    @pl.kernel(out_type=jax.ShapeDtypeStruct((out_rows, dim), x.dtype), mesh=_sc_mesh(), scratch_types=[])
    def k(x_hbm, i0_hbm, i1_hbm, o_hbm):
        def body(x_v, i0_v, i1_v):
            pltpu.sync_copy(x_v, o_hbm.at[i0_v.at[0]])
            pltpu.sync_copy(x_v, o_hbm.at[i1_v.at[0]])

        pltpu.emit_pipeline(
            body, grid=(n // SC_WINDOW,),
            in_specs=[pl.BlockSpec((SC_WINDOW, dim), index_map=lambda i: (i, 0)),
                      pl.BlockSpec((1, SC_WINDOW), index_map=lambda i: (0, i)),
                      pl.BlockSpec((1, SC_WINDOW), index_map=lambda i: (0, i))],
            out_specs=[],
            core_axis_name=("core", "subcore"), dimension_semantics=(pltpu.PARALLEL,),
        )(x_hbm, i0_hbm, i1_hbm)

    return k(x, pos[0:1], pos[1:2])


def sc_gather_rows(x, idx):
    ni = idx.shape[1]
    dim = x.shape[1]

    @pl.kernel(out_type=jax.ShapeDtypeStruct((ni, dim), x.dtype), mesh=_sc_mesh(), scratch_types=[])
    def k(x_hbm, i_hbm, o_hbm):
        def body(i_v, o_v):
            pltpu.sync_copy(x_hbm.at[i_v.at[0]], o_v)

        pltpu.emit_pipeline(
            body, grid=(ni // SC_WINDOW,),
            in_specs=[pl.BlockSpec((1, SC_WINDOW), index_map=lambda i: (0, i))],
            out_specs=[pl.BlockSpec((SC_WINDOW, dim), index_map=lambda i: (i, 0))],
            core_axis_name=("core", "subcore"), dimension_semantics=(pltpu.PARALLEL,),
        )(i_hbm, o_hbm)

    return k(x, idx)


def _moe_expert_kernel(te_ref, nu_ref, xa_ref, xb_ref, wg_ref, wu_ref, wd_ref, ya_ref, yb_ref, act_ref, *, tf):
    @pl.when(pl.program_id(0) < nu_ref[0])
    def _():
        xb = _unpack_row(xa_ref[...], xb_ref[...]).astype(BF16)
        for j in range(0, D_FF_EXPERT, tf):
            gate = _dot(xb, wg_ref[0, :, j:j + tf])
            up = _dot(xb, wu_ref[0, :, j:j + tf])
            act_ref[:, j:j + tf] = (_silu(gate) * up).astype(BF16)
        _pack_row(_dot(act_ref[...], wd_ref[0]), ya_ref, yb_ref)


def moe_experts(xa, xb, tile_expert, n_used, wg, wu, wd, tf):
    rows = xa.shape[0]
    nt = rows // MOE_ROW_TILE
    xspec = pl.BlockSpec((MOE_ROW_TILE, PACK_W), lambda i, te, nu: (i, 0))
    sds = jax.ShapeDtypeStruct((rows, PACK_W), jnp.int32)
    return pl.pallas_call(
        functools.partial(_moe_expert_kernel, tf=tf),
        grid_spec=pltpu.PrefetchScalarGridSpec(
            num_scalar_prefetch=2,
            grid=(nt,),
            in_specs=[xspec, xspec,
                      pl.BlockSpec((1, D_MODEL, D_FF_EXPERT), lambda i, te, nu: (te[i], 0, 0)),
                      pl.BlockSpec((1, D_MODEL, D_FF_EXPERT), lambda i, te, nu: (te[i], 0, 0)),
                      pl.BlockSpec((1, D_FF_EXPERT, D_MODEL), lambda i, te, nu: (te[i], 0, 0))],
            out_specs=[xspec, xspec],
            scratch_shapes=[pltpu.VMEM((MOE_ROW_TILE, D_FF_EXPERT), BF16)],
        ),
        out_shape=[sds, sds],
        compiler_params=_cparams(("arbitrary",)),
        name="moe_experts",
    )(tile_expert, n_used, xa, xb, wg, wu, wd)


def _moe_combine_kernel(h_ref, gate_ref, ya0_ref, yb0_ref, ya1_ref, yb1_ref, g_ref, b_ref, o_ref):
    gate = gate_ref[...]
    lane = lax.broadcasted_iota(jnp.int32, gate.shape, 1)
    g0 = _lane_pick(gate, lane, 0)
    g1 = _lane_pick(gate, lane, 1)
    y = g0 * _unpack_row(ya0_ref[...], yb0_ref[...]) + g1 * _unpack_row(ya1_ref[...], yb1_ref[...])
    o_ref[...] = _layer_norm(ALPHA * h_ref[...] + y, g_ref[...], b_ref[...])


def moe_combine_res_ln(h, gate, ya, yb, g, b, tm):
    n = h.shape[0]
    nb = n // tm
    first = pl.BlockSpec((tm, PACK_W), lambda i: (i, 0))
    second = pl.BlockSpec((tm, PACK_W), lambda i: (nb + i, 0))
    return pl.pallas_call(
        _moe_combine_kernel,
        grid=(nb,),
        in_specs=[pl.BlockSpec((tm, D_MODEL), lambda i: (i, 0)),
                  pl.BlockSpec((tm, LANES), lambda i: (i, 0)),
                  first, first, second, second,
                  _const_spec((1, D_MODEL)), _const_spec((1, D_MODEL))],
        out_specs=pl.BlockSpec((tm, D_MODEL), lambda i: (i, 0)),
        out_shape=jax.ShapeDtypeStruct((n, D_MODEL), F32),
        compiler_params=_cparams(("parallel",)),
        name="moe_combine_res_ln",
    )(h, gate, ya, yb, ya, yb, g, b)


def _row(v):
    return v.reshape(1, -1).astype(F32)


def _pad_lanes(v, fill=0.0):
    v = v.astype(F32)
    pad = LANES - v.shape[-1]
    return jnp.concatenate([v, jnp.full(v.shape[:-1] + (pad,), fill, F32)], axis=-1)


def prepare_weights(p):
    w = {}
    w_in = p["ev_w_in"][0]
    o1 = D_INNER
    o2 = o1 + CONV_DIM
    o3 = o2 + 2 * SSD_HEADS
    w["wz"] = w_in[:, :o1].astype(BF16)
    w["wx"] = w_in[:, o1:o2].astype(BF16)
    w["wdt"] = _pad_lanes(w_in[:, o2:o3]).astype(BF16)
    w["wu"] = w_in[:, o3:].astype(BF16)
    w["ln_in_g"], w["ln_in_b"] = _row(p["ln_in_g"]), _row(p["ln_in_b"])
    w["conv_w"] = p["ev_conv_w"][0].astype(F32)
    w["conv_b"] = _row(p["ev_conv_b"][0])
    w["dt_bias"] = _pad_lanes(jnp.concatenate([p["ev_dt_bias_f"][0], p["ev_dt_bias_b"][0]])[None, :])
    w["a_log"] = _pad_lanes(jnp.concatenate([p["ev_a_log_f"][0], p["ev_a_log_b"][0]])[None, :])
    w["d_skip"] = jnp.repeat(p["ev_d_skip"][0].astype(F32), D_INNER // SSD_HEADS)[None, :]
    w["gnorm"] = _row(p["ev_gnorm_w"][0])
    w_out = p["ev_w_out"][0]
    w["wo_ssd"] = w_out[:D_INNER].astype(BF16)
    w["wo_fn"] = w_out[D_INNER:].astype(BF16)
    w["ev_ln1_g"], w["ev_ln1_b"] = _row(p["ev_ln1_g"][0]), _row(p["ev_ln1_b"][0])
    w["ffn_wg"] = p["ev_ffn_wg"][0].astype(BF16)
    w["ffn_wu"] = p["ev_ffn_wu"][0].astype(BF16)
    w["ffn_wd"] = p["ev_ffn_wd"][0].astype(BF16)
    w["ev_ln2_g"], w["ev_ln2_b"] = _row(p["ev_ln2_g"][0]), _row(p["ev_ln2_b"][0])
    w["w_qkv"] = p["od_w_qkv"][0].astype(BF16)
    w["na_bias"] = _natten_bias_table(p["od_rpb"][0])
    w["od_w_out"] = p["od_w_out"][0].astype(BF16)
    w["od_ln1_g"], w["od_ln1_b"] = _row(p["od_ln1_g"][0]), _row(p["od_ln1_b"][0])
    w["router"] = _pad_lanes(p["od_router"][0]).astype(BF16)
    w["moe_wg"] = p["od_wg"][0].astype(BF16)
    w["moe_wu"] = p["od_wu"][0].astype(BF16)
    w["moe_wd"] = p["od_wd"][0].astype(BF16)
    w["od_ln2_g"], w["od_ln2_b"] = _row(p["od_ln2_g"][0]), _row(p["od_ln2_b"][0])
    return w


def _pick(n, pref):
    t = pref
    while n % t:
        t //= 2
    return t


def moe_plan(counts, n_rows):
    cnt = counts[0, :N_EXPERTS]
    padded = jnp.ceil(cnt / MOE_ROW_TILE) * MOE_ROW_TILE
    ends = jnp.cumsum(padded)
    offsets = (ends - padded).astype(jnp.int32)
    n_used = (ends[-1:] / MOE_ROW_TILE).astype(jnp.int32)
    starts = jnp.arange(n_rows // MOE_ROW_TILE, dtype=F32) * MOE_ROW_TILE
    tile_expert = jnp.minimum(jnp.sum(starts[:, None] >= ends[None, :], axis=1), N_EXPERTS - 1)
    return offsets, tile_expert.astype(jnp.int32), n_used


def moe_res_ln(h3, xa, xb, w):
    n = h3.shape[0]
    tm = _pick(n, 512)
    n_rows = 2 * n + N_EXPERTS * MOE_ROW_TILE
    meta, gate, counts = router(h3, w["router"], tm)
    offsets, tile_expert, n_used = moe_plan(counts, n_rows)
    pos = moe_positions(meta, offsets, _pick(n, 2048))
    xs_a = sc_dispatch_rows(xa, pos, n_rows)
    xs_b = sc_dispatch_rows(xb, pos, n_rows)
    ys_a, ys_b = moe_experts(xs_a, xs_b, tile_expert, n_used, w["moe_wg"], w["moe_wu"], w["moe_wd"], tf=512)
    flat = pos.reshape(1, 2 * n)
    ya = sc_gather_rows(ys_a, flat)
    yb = sc_gather_rows(ys_b, flat)
    return moe_combine_res_ln(h3, gate, ya, yb, w["od_ln2_g"], w["od_ln2_b"], tm)


def trunk(x, w):
    bsz, t, d = x.shape
    n = bsz * t
    tm = _pick(n, 512)
    nc = t // CHUNK
    xf = x.reshape(n, d)

    h0, z, xbc, dt_raw, u = ln_inproj(xf, w["ln_in_g"], w["ln_in_b"], w["wz"], w["wx"], w["wdt"], w["wu"], tm)
    xbc_act = conv_silu(xbc.reshape(bsz, t, CONV_DIM), w["conv_w"], w["conv_b"],
                        tt=_pick(t, 512), tc=1024)
    pcol, q = ssd_prep(dt_raw.reshape(bsz, t, LANES), w["dt_bias"], w["a_log"], cpp=_pick(nc, 4))
    cps = _pick(nc, 4)
    y_b = ssd_scan(xbc_act, pcol, q, rev=True, cps=cps)
    y_ssd = ssd_scan(xbc_act, pcol, q, rev=False, cps=cps,
                     fused=(y_b, z.reshape(bsz, t, D_INNER), w["d_skip"], w["gnorm"]))
    y_fn = fnet_mix(u.reshape(bsz, t, FNET_WIDTH))
    h1 = proj_res_ln([y_ssd.reshape(n, D_INNER), y_fn.reshape(n, FNET_WIDTH)],
                     [w["wo_ssd"], w["wo_fn"]], h0, w["ev_ln1_g"], w["ev_ln1_b"], tm)
    h2 = ffn_res_ln(h1, w["ffn_wg"], w["ffn_wu"], w["ffn_wd"], w["ev_ln2_g"], w["ev_ln2_b"], tm)

    qh, kh, vh = qkv_proj(h2, w["w_qkv"], tm)
    att = natten(qh.reshape(bsz, t, d), kh.reshape(bsz, t, d), vh.reshape(bsz, t, d), w["na_bias"])
    h3, xa, xb = proj_res_ln([att.reshape(n, d)], [w["od_w_out"]], h2, w["od_ln1_g"], w["od_ln1_b"], tm,
                             packed=True)
    return moe_res_ln(h3, xa, xb, w).reshape(bsz, t, d)


def kernel(x_prompt, x_sample, ln_in_g, ln_in_b, ev_w_in, ev_conv_w, ev_conv_b, ev_dt_bias_f, ev_dt_bias_b,
           ev_a_log_f, ev_a_log_b, ev_d_skip, ev_gnorm_w, ev_w_out, ev_ln1_g, ev_ln1_b, ev_ffn_wg, ev_ffn_wu,
           ev_ffn_wd, ev_ln2_g, ev_ln2_b, od_w_qkv, od_rpb, od_w_out, od_ln1_g, od_ln1_b, od_router, od_wg,
           od_wu, od_wd, od_ln2_g, od_ln2_b):
    params = dict(ln_in_g=ln_in_g, ln_in_b=ln_in_b, ev_w_in=ev_w_in, ev_conv_w=ev_conv_w, ev_conv_b=ev_conv_b,
                  ev_dt_bias_f=ev_dt_bias_f, ev_dt_bias_b=ev_dt_bias_b, ev_a_log_f=ev_a_log_f,
                  ev_a_log_b=ev_a_log_b, ev_d_skip=ev_d_skip, ev_gnorm_w=ev_gnorm_w, ev_w_out=ev_w_out,
                  ev_ln1_g=ev_ln1_g, ev_ln1_b=ev_ln1_b, ev_ffn_wg=ev_ffn_wg, ev_ffn_wu=ev_ffn_wu,
                  ev_ffn_wd=ev_ffn_wd, ev_ln2_g=ev_ln2_g, ev_ln2_b=ev_ln2_b, od_w_qkv=od_w_qkv, od_rpb=od_rpb,
                  od_w_out=od_w_out, od_ln1_g=od_ln1_g, od_ln1_b=od_ln1_b, od_router=od_router, od_wg=od_wg,
                  od_wu=od_wu, od_wd=od_wd, od_ln2_g=od_ln2_g, od_ln2_b=od_ln2_b)
    w = prepare_weights(params)
    return (trunk(x_prompt, w), trunk(x_sample, w))
```

```python
import functools
import math

import numpy as np
import jax
import jax.numpy as jnp
from jax import lax
from jax.experimental import pallas as pl
from jax.experimental.pallas import tpu as pltpu
from jax.experimental.pallas import tpu_sc as plsc

F32 = jnp.float32
BF16 = jnp.bfloat16

D_MODEL = 1024
GRID_W = 64
D_INNER = 2048
SSD_HEADS = 32
SSD_GROUPS = 4
HEADS_PER_GROUP = SSD_HEADS // SSD_GROUPS
GROUP_DIM = D_INNER // SSD_GROUPS
D_STATE = 128
D_CONV = 5
CHUNK = 128
CONV_DIM = D_INNER + 2 * SSD_GROUPS * D_STATE
FNET_GROUPS = 4
FNET_GROUP_DIM = 256
FNET_WIDTH = 1024
FNET_N2 = 128
FNET_SUB = 8
FNET_STEP = 16
NA_HEADS = 16
NA_HEAD_DIM = 64
NA_WIN_H = 8
NA_WIN_W = 16
D_FF_DENSE = 2816
N_EXPERTS = 8
D_FF_EXPERT = 3584
LN_EPS = 1e-5
RMS_EPS = 1e-5
DEPTH = 2
ALPHA = (2 * DEPTH) ** 0.25
LANES = 128
HALO = 16
NEG_BIG = -1e30
LOG2E = 1.4426950408889634
VMEM_LIMIT = 56 * 1024 * 1024
PACK_W = D_MODEL // 4
SC_WINDOW = 128
MOE_ROW_TILE = 512


def _cparams(sem):
    return pltpu.CompilerParams(dimension_semantics=sem, vmem_limit_bytes=VMEM_LIMIT)


def _const_spec(shape):
    nd = len(shape)
    return pl.BlockSpec(shape, lambda *_: (0,) * nd)


def _dot(a, b):
    return jnp.dot(a, b, preferred_element_type=F32)


def _layer_norm(xf, g, b):
    mu = jnp.mean(xf, axis=-1, keepdims=True)
    xc = xf - mu
    var = jnp.mean(xc * xc, axis=-1, keepdims=True)
    return xc * lax.rsqrt(var + LN_EPS) * g + b


def _silu(x):
    return x * jax.nn.sigmoid(x)


def _ln_inproj_kernel(x_ref, g_ref, b_ref, wz_ref, wx_ref, wdt_ref, wu_ref,
                      h_ref, z_ref, xbc_ref, dt_ref, u_ref):
    h = _layer_norm(x_ref[...], g_ref[...], b_ref[...])
    h_ref[...] = h
    hb = h.astype(BF16)
    for j in range(0, D_INNER, 512):
        z_ref[:, j:j + 512] = _dot(hb, wz_ref[:, j:j + 512]).astype(BF16)
    for j in range(0, CONV_DIM, 512):
        xbc_ref[:, j:j + 512] = _dot(hb, wx_ref[:, j:j + 512]).astype(BF16)
    dt_ref[...] = _dot(hb, wdt_ref[...])
    for j in range(0, FNET_WIDTH, 512):
        u_ref[:, j:j + 512] = _dot(hb, wu_ref[:, j:j + 512])


def ln_inproj(x, g, b, wz, wx, wdt, wu, tm):
    n = x.shape[0]
    row = lambda w: pl.BlockSpec((tm, w), lambda i: (i, 0))
    return pl.pallas_call(
        _ln_inproj_kernel,
        grid=(n // tm,),
        in_specs=[row(D_MODEL), _const_spec((1, D_MODEL)), _const_spec((1, D_MODEL)),
                  _const_spec(wz.shape), _const_spec(wx.shape), _const_spec(wdt.shape),
                  _const_spec(wu.shape)],
        out_specs=[row(D_MODEL), row(D_INNER), row(CONV_DIM), row(LANES), row(FNET_WIDTH)],
        out_shape=[jax.ShapeDtypeStruct((n, D_MODEL), F32),
                   jax.ShapeDtypeStruct((n, D_INNER), BF16),
                   jax.ShapeDtypeStruct((n, CONV_DIM), BF16),
                   jax.ShapeDtypeStruct((n, LANES), F32),
                   jax.ShapeDtypeStruct((n, FNET_WIDTH), F32)],
        compiler_params=_cparams(("parallel",)),
        name="ln_inproj",
    )(x, g, b, wz, wx, wdt, wu)


def _conv_silu_kernel(prev_ref, main_ref, next_ref, shift_ref, w_ref, b_ref, o_ref, scr, *, tt, nt):
    i = pl.program_id(1)
    zero = jnp.zeros((HALO, scr.shape[1]), BF16)
    scr[0:HALO, :] = prev_ref[0]
    scr[HALO:HALO + tt, :] = main_ref[0]
    scr[HALO + tt:2 * HALO + tt, :] = next_ref[0]

    @pl.when(i == 0)
    def _():
        scr[0:HALO, :] = zero

    @pl.when(i == nt - 1)
    def _():
        scr[HALO + tt:2 * HALO + tt, :] = zero

    half = D_CONV // 2
    strip = 2 * LANES
    for r0 in range(0, tt, CHUNK):
        for c0 in range(0, scr.shape[1], strip):
            cols = slice(c0, c0 + strip)
            ext = scr[r0:r0 + CHUNK + 2 * HALO, cols]
            acc = b_ref[:, cols] + w_ref[half:half + 1, cols] * ext[HALO:HALO + CHUNK].astype(F32)
            for j, k in enumerate([k for k in range(D_CONV) if k != half]):
                sh = _dot(shift_ref[j * CHUNK:(j + 1) * CHUNK, :], ext)
                acc = acc + w_ref[k:k + 1, cols] * sh
            o_ref[0, r0:r0 + CHUNK, cols] = _silu(acc).astype(BF16)


def _conv_shift_matrix():
    taps = [k for k in range(D_CONV) if k != D_CONV // 2]
    s = np.zeros((len(taps) * CHUNK, CHUNK + 2 * HALO), np.float32)
    for j, k in enumerate(taps):
        for r in range(CHUNK):
            s[j * CHUNK + r, HALO + r + k - D_CONV // 2] = 1.0
    return s


def conv_silu(xbc, w, b, tt, tc):
    bsz, t, c = xbc.shape
    nt = t // tt
    hb = tt // HALO
    kern = functools.partial(_conv_silu_kernel, tt=tt, nt=nt)
    shift = jnp.asarray(_conv_shift_matrix(), BF16)
    return pl.pallas_call(
        kern,
        grid=(bsz, nt, c // tc),
        in_specs=[
            pl.BlockSpec((1, HALO, tc), lambda bi, i, ci: (bi, jnp.maximum(i * hb - 1, 0), ci)),
            pl.BlockSpec((1, tt, tc), lambda bi, i, ci: (bi, i, ci)),
            pl.BlockSpec((1, HALO, tc), lambda bi, i, ci: (bi, jnp.minimum((i + 1) * hb, t // HALO - 1), ci)),
            _const_spec(shift.shape),
            pl.BlockSpec((D_CONV, tc), lambda bi, i, ci: (0, ci)),
            pl.BlockSpec((1, tc), lambda bi, i, ci: (0, ci)),
        ],
        out_specs=pl.BlockSpec((1, tt, tc), lambda bi, i, ci: (bi, i, ci)),
        out_shape=jax.ShapeDtypeStruct((bsz, t, c), BF16),
        scratch_shapes=[pltpu.VMEM((tt + 2 * HALO, tc), BF16)],
        compiler_params=_cparams(("parallel", "parallel", "parallel")),
        name="conv_silu",
    )(xbc, xbc, xbc, shift, w, b)


def _split3(x):
    hi = x.astype(BF16)
    r1 = x - hi.astype(F32)
    mid = r1.astype(BF16)
    lo = (r1 - mid.astype(F32)).astype(BF16)
    return hi, mid, lo


def _ssd_prep_kernel(dt_ref, bias_ref, alog_ref, tri_ref, trit_ref, sel_ref, pcol_ref, q_ref, *, cpp):
    lane = lax.broadcasted_iota(jnp.int32, (CHUNK, LANES), 1)
    fwd = lane < SSD_HEADS
    a_coef = -jnp.exp(alog_ref[...])
    for c in range(cpp):
        rows = slice(c * CHUNK, (c + 1) * CHUNK)
        raw = dt_ref[0, rows, :] + bias_ref[...]
        dt = jnp.maximum(raw, 0.0) + jnp.log1p(jnp.exp(-jnp.abs(raw)))
        a = dt * a_coef
        cs_f = jnp.dot(tri_ref[...], a, precision=lax.Precision.HIGHEST, preferred_element_type=F32)
        cs_b = jnp.dot(trit_ref[...], a, precision=lax.Precision.HIGHEST, preferred_element_type=F32)
        cs = jnp.where(fwd, cs_f, cs_b)
        end = jnp.where(fwd[0:1], cs[CHUNK - 1:CHUNK, :], cs[0:1, :])
        wend = jnp.exp(end - cs) * dt
        hi, mid, lo = _split3(cs * LOG2E)
        src = jnp.concatenate([hi, mid, lo, wend.astype(BF16)], axis=1)
        pc = _dot(src, sel_ref[...]).astype(BF16)
        pcol_ref[0, 0, rows, :] = pc[:, :SSD_GROUPS * LANES]
        pcol_ref[1, 0, rows, :] = pc[:, SSD_GROUPS * LANES:]
        src_t = ((cs - jnp.log(dt)) * LOG2E).T
        for d in range(2):
            for g in range(SSD_GROUPS):
                r0 = d * SSD_HEADS + g * HEADS_PER_GROUP
                q_ref[d, 0, g, c] = src_t[r0:r0 + HEADS_PER_GROUP, :]


def _prep_select_matrix():
    sel = np.zeros((4 * LANES, 2 * SSD_GROUPS * LANES), np.float32)
    for d in range(2):
        for g in range(SSD_GROUPS):
            for q in range(4):
                for i in range(HEADS_PER_GROUP):
                    src = q * LANES + d * SSD_HEADS + g * HEADS_PER_GROUP + i
                    dst = d * SSD_GROUPS * LANES + g * LANES + q * HEADS_PER_GROUP + i
                    sel[src, dst] = 1.0
    return sel


def ssd_prep(dt_raw, bias_row, alog_row, cpp):
    bsz, t, _ = dt_raw.shape
    nc = t // CHUNK
    tri = np.tril(np.ones((CHUNK, CHUNK), np.float32))
    kern = functools.partial(_ssd_prep_kernel, cpp=cpp)
    return pl.pallas_call(
        kern,
        grid=(bsz, nc // cpp),
        in_specs=[pl.BlockSpec((1, cpp * CHUNK, LANES), lambda b, j: (b, j, 0)),
                  _const_spec((1, LANES)), _const_spec((1, LANES)),
                  _const_spec((CHUNK, CHUNK)), _const_spec((CHUNK, CHUNK)),
                  _const_spec((4 * LANES, 2 * SSD_GROUPS * LANES))],
        out_specs=[pl.BlockSpec((2, 1, cpp * CHUNK, SSD_GROUPS * LANES), lambda b, j: (0, b, j, 0)),
                   pl.BlockSpec((2, 1, SSD_GROUPS, cpp, HEADS_PER_GROUP, CHUNK),
                                lambda b, j: (0, b, 0, j, 0, 0))],
        out_shape=[jax.ShapeDtypeStruct((2, bsz, t, SSD_GROUPS * LANES), BF16),
                   jax.ShapeDtypeStruct((2, bsz, SSD_GROUPS, nc, HEADS_PER_GROUP, CHUNK), F32)],
        compiler_params=_cparams(("parallel", "parallel")),
        name="ssd_prep",
    )(dt_raw, bias_row, alog_row, jnp.asarray(tri), jnp.asarray(tri.T),
      jnp.asarray(_prep_select_matrix(), BF16))


def _expand_matrix():
    e = np.zeros((LANES, HEADS_PER_GROUP * LANES + GROUP_DIM), np.float32)
    for q in range(3):
        for h in range(HEADS_PER_GROUP):
            e[q * HEADS_PER_GROUP + h, h * LANES:(h + 1) * LANES] = 1.0
    for h in range(HEADS_PER_GROUP):
        e[3 * HEADS_PER_GROUP + h,
          HEADS_PER_GROUP * LANES + h * 64:HEADS_PER_GROUP * LANES + (h + 1) * 64] = 1.0
    return e


def _ssd_kernel(*refs, rev, cps, fuse):
    if fuse:
        xs_ref, b_ref, c_ref, p_ref, q_ref, e_ref, yb_ref, z_ref, dsk_ref, gn_ref, o_ref, st_ref = refs
    else:
        xs_ref, b_ref, c_ref, p_ref, q_ref, e_ref, o_ref, st_ref = refs

    @pl.when(pl.program_id(2) == 0)
    def _():
        st_ref[...] = jnp.zeros_like(st_ref)

    row = lax.broadcasted_iota(jnp.int32, (CHUNK, CHUNK), 0)
    col = lax.broadcasted_iota(jnp.int32, (CHUNK, CHUNK), 1)
    mask = (col >= row) if rev else (col <= row)
    lo_half = lax.broadcasted_iota(jnp.int32, (CHUNK, LANES), 1) < 64
    nb = HEADS_PER_GROUP * LANES
    end = 0 if rev else CHUNK - 1
    order = range(cps - 1, -1, -1) if rev else range(cps)
    for c in order:
        rows = slice(c * CHUNK, (c + 1) * CHUNK)
        xf = xs_ref[0, rows, :].astype(F32)
        bm = b_ref[0, rows, :]
        cm = c_ref[0, rows, :]
        ex = _dot(p_ref[0, 0, rows, :], e_ref[...])
        qv = q_ref[0, 0, 0, c]
        cb = lax.dot_general(cm, bm, (((1,), (1,)), ((), ())), preferred_element_type=F32)
        cs_parts = []
        y_parts = []
        for j in range(HEADS_PER_GROUP // 2):
            h1, h2 = 2 * j, 2 * j + 1
            c1 = ex[:, h1 * LANES:(h1 + 1) * LANES]
            c2 = ex[:, h2 * LANES:(h2 + 1) * LANES]
            cs_parts.append(jnp.where(lo_half, c1, c2))
            w1 = jnp.exp2(jnp.where(mask, c1 - qv[h1:h1 + 1, :], -jnp.inf)) * cb
            w2 = jnp.exp2(jnp.where(mask, c2 - qv[h2:h2 + 1, :], -jnp.inf)) * cb
            wp = jnp.concatenate([w1, w2], axis=1).astype(BF16)
            xp = xf[:, j * LANES:(j + 1) * LANES]
            rhs = jnp.concatenate([jnp.where(lo_half, xp, 0.0), jnp.where(lo_half, 0.0, xp)],
                                  axis=0).astype(BF16)
            y_parts.append(_dot(wp, rhs))
        ydiag = jnp.concatenate(y_parts, axis=1)
        expcs = jnp.exp2(jnp.concatenate(cs_parts, axis=1))
        decay = expcs[end:end + 1, :]
        st = st_ref[...]
        yoff = _dot(cm, st.astype(BF16)) * expcs
        xsw = (xf * ex[:, nb:nb + GROUP_DIM]).astype(BF16)
        st_ref[...] = st * decay + lax.dot_general(bm, xsw, (((0,), (0,)), ((), ())),
                                                   preferred_element_type=F32)
        y = ydiag + yoff
        if fuse:
            y = y + yb_ref[0, rows, :].astype(F32) + dsk_ref[...] * xf
            y = y * _silu(z_ref[0, rows, :].astype(F32))
            ms = jnp.mean(y * y, axis=-1, keepdims=True)
            o_ref[0, rows, :] = (y * lax.rsqrt(ms + RMS_EPS) * gn_ref[...]).astype(BF16)
        else:
            o_ref[0, rows, :] = y.astype(BF16)


def ssd_scan(xbc_act, pcol, q, rev, cps, fused=None):
    bsz, t, _ = xbc_act.shape
    ncb = t // (cps * CHUNK)
    r = cps * CHUNK
    d = 1 if rev else 0
    cidx = (lambda j: ncb - 1 - j) if rev else (lambda j: j)
    e = jnp.asarray(_expand_matrix(), BF16)
    b0 = D_INNER // LANES
    c0 = b0 + SSD_GROUPS
    in_specs = [
        pl.BlockSpec((1, r, GROUP_DIM), lambda b, g, j: (b, cidx(j), g)),
        pl.BlockSpec((1, r, LANES), lambda b, g, j: (b, cidx(j), b0 + g)),
        pl.BlockSpec((1, r, LANES), lambda b, g, j: (b, cidx(j), c0 + g)),
        pl.BlockSpec((1, 1, r, LANES), lambda b, g, j: (d, b, cidx(j), g)),
        pl.BlockSpec((1, 1, 1, cps, HEADS_PER_GROUP, CHUNK), lambda b, g, j: (d, b, g, cidx(j), 0, 0)),
        _const_spec(e.shape),
    ]
    args = [xbc_act, xbc_act, xbc_act, pcol, q, e]
    if fused is not None:
        yb, z, dsk, gn = fused
        in_specs += [
            pl.BlockSpec((1, r, GROUP_DIM), lambda b, g, j: (b, cidx(j), g)),
            pl.BlockSpec((1, r, GROUP_DIM), lambda b, g, j: (b, cidx(j), g)),
            pl.BlockSpec((1, GROUP_DIM), lambda b, g, j: (0, g)),
            pl.BlockSpec((1, GROUP_DIM), lambda b, g, j: (0, g)),
        ]
        args += [yb, z, dsk, gn]
    kern = functools.partial(_ssd_kernel, rev=rev, cps=cps, fuse=fused is not None)
    return pl.pallas_call(
        kern,
        grid=(bsz, SSD_GROUPS, ncb),
        in_specs=in_specs,
        out_specs=pl.BlockSpec((1, r, GROUP_DIM), lambda b, g, j: (b, cidx(j), g)),
        out_shape=jax.ShapeDtypeStruct((bsz, t, D_INNER), BF16),
        scratch_shapes=[pltpu.VMEM((D_STATE, GROUP_DIM), F32)],
        compiler_params=_cparams(("parallel", "parallel", "arbitrary")),
        name="ssd_bwd" if rev else "ssd_fwd",
    )(*args)


def _dft_cos_sin(n):
    k = np.arange(n, dtype=np.float64)
    ang = 2.0 * np.pi * np.outer(k, k) / n
    return np.cos(ang), np.sin(ang)


def _fnet_s1_kernel(u_ref, g_ref, cos_ref, sin_ref, o_ref, *, n1):
    c = FNET_WIDTH
    rows = n1 * FNET_SUB
    re_parts, im_parts = [], []
    for h in range(FNET_STEP // FNET_SUB):
        x = u_ref[0, :, h * FNET_SUB:(h + 1) * FNET_SUB, :].reshape(rows, c).astype(BF16)
        a = _dot(g_ref[...], x)
        ar = a[:rows]
        ai = a[rows:]
        ct = jnp.concatenate([cos_ref[h]] * (c // LANES), axis=1)
        st = jnp.concatenate([sin_ref[h]] * (c // LANES), axis=1)
        re_parts.append((ar * ct + ai * st).reshape(n1, FNET_SUB, c))
        im_parts.append((ai * ct - ar * st).reshape(n1, FNET_SUB, c))
    o_ref[0, 0] = jnp.concatenate(re_parts, axis=1).astype(BF16)
    o_ref[0, 1] = jnp.concatenate(im_parts, axis=1).astype(BF16)


def _fnet_s2_kernel(br_ref, bi_ref, f2_ref, wc_ref, o_ref, *, ks):
    n2 = FNET_N2
    c = FNET_WIDTH
    gs = []
    for kk in range(ks):
        rhs = jnp.concatenate([br_ref[0, 0, kk], bi_ref[0, 0, kk]], axis=0)
        gs.append(_dot(f2_ref[...], rhs).astype(BF16))
    for grp in range(FNET_GROUPS):
        cols = slice(grp * FNET_GROUP_DIM, (grp + 1) * FNET_GROUP_DIM)
        lhs = jnp.concatenate([jnp.concatenate([g[:n2, cols], g[n2:, cols]], axis=1) for g in gs], axis=0)
        res = _dot(lhs, wc_ref[...]).astype(BF16)
        for kk in range(ks):
            o_ref[0, :, kk * c + grp * FNET_GROUP_DIM:kk * c + (grp + 1) * FNET_GROUP_DIM] = (
                res[kk * n2:(kk + 1) * n2])


def fnet_mix(u):
    bsz, t, c = u.shape
    n2 = FNET_N2
    n1 = t // n2
    sub, step = FNET_SUB, FNET_STEP
    c1, s1 = _dft_cos_sin(n1)
    c2, s2 = _dft_cos_sin(n2)
    cc, sc = _dft_cos_sin(FNET_GROUP_DIM)
    g1 = jnp.asarray(np.kron(np.concatenate([c1, -s1], axis=0), np.eye(sub)), BF16)
    f2 = jnp.asarray(np.block([[c2, s2], [-s2, c2]]), BF16)
    scale = 1.0 / math.sqrt(t * FNET_GROUP_DIM)
    wc = jnp.asarray(np.concatenate([cc, sc], axis=0) * scale, BF16)
    k1 = np.arange(n1)[None, :, None]
    pos = (np.arange(n2 // sub)[:, None, None] * sub + np.arange(sub)[None, None, :])
    ang = (2.0 * np.pi * k1 * pos / t).reshape(n2 // sub, n1 * sub)
    tw_cos = jnp.asarray(np.repeat(np.cos(ang)[:, :, None], LANES, axis=2), F32)
    tw_sin = jnp.asarray(np.repeat(np.sin(ang)[:, :, None], LANES, axis=2), F32)

    hs = step // sub
    bv = pl.pallas_call(
        functools.partial(_fnet_s1_kernel, n1=n1),
        grid=(bsz, n2 // step),
        in_specs=[pl.BlockSpec((1, n1, step, c), lambda b, j: (b, 0, j, 0)),
                  _const_spec(g1.shape),
                  pl.BlockSpec((hs, n1 * sub, LANES), lambda b, j: (j, 0, 0)),
                  pl.BlockSpec((hs, n1 * sub, LANES), lambda b, j: (j, 0, 0))],
        out_specs=pl.BlockSpec((1, 2, n1, step, c), lambda b, j: (b, 0, 0, j, 0)),
        out_shape=jax.ShapeDtypeStruct((bsz, 2, n1, n2, c), BF16),
        compiler_params=_cparams(("parallel", "parallel")),
        name="fnet_stage1",
    )(u.reshape(bsz, n1, n2, c), g1, tw_cos, tw_sin)

    ks = _pick(n1, 4)
    y = pl.pallas_call(
        functools.partial(_fnet_s2_kernel, ks=ks),
        grid=(bsz, n1 // ks),
        in_specs=[pl.BlockSpec((1, 1, ks, n2, c), lambda b, k: (b, 0, k, 0, 0)),
                  pl.BlockSpec((1, 1, ks, n2, c), lambda b, k: (b, 1, k, 0, 0)),
                  _const_spec((2 * n2, 2 * n2)),
                  _const_spec((2 * FNET_GROUP_DIM, FNET_GROUP_DIM))],
        out_specs=pl.BlockSpec((1, n2, ks * c), lambda b, k: (b, 0, k)),
        out_shape=jax.ShapeDtypeStruct((bsz, n2, n1 * c), BF16),
        compiler_params=_cparams(("parallel", "parallel")),
        name="fnet_stage2",
    )(bv, bv, f2, wc)
    return y.reshape(bsz, t, c)


def _pack_bf16_pair(a, b):
    lo = lax.bitcast_convert_type(a.astype(BF16).astype(F32), jnp.uint32)
    hi = lax.bitcast_convert_type(b.astype(BF16).astype(F32), jnp.uint32)
    word = lax.shift_right_logical(lo, jnp.uint32(16)) | hi
    return lax.bitcast_convert_type(word, jnp.int32)


def _unpack_bf16_pair(word):
    w = lax.bitcast_convert_type(word, jnp.uint32)
    a = lax.bitcast_convert_type(lax.shift_left(w, jnp.uint32(16)), F32)
    b = lax.bitcast_convert_type(w & jnp.uint32(0xFFFF0000), F32)
    return a, b


def _pack_row(x, a_ref, b_ref):
    a_ref[...] = _pack_bf16_pair(x[:, 0:PACK_W], x[:, PACK_W:2 * PACK_W])
    b_ref[...] = _pack_bf16_pair(x[:, 2 * PACK_W:3 * PACK_W], x[:, 3 * PACK_W:4 * PACK_W])


def _unpack_row(a_word, b_word):
    x0, x1 = _unpack_bf16_pair(a_word)
    x2, x3 = _unpack_bf16_pair(b_word)
    return jnp.concatenate([x0, x1, x2, x3], axis=1)


def _proj_res_ln_kernel(*refs, n_in, packed):
    xs = refs[:n_in]
    ws = refs[n_in:2 * n_in]
    h_ref, g_ref, b_ref, o_ref = refs[2 * n_in:2 * n_in + 4]
    acc = _dot(xs[0][...], ws[0][...])
    for x_ref, w_ref in zip(xs[1:], ws[1:]):
        acc = acc + _dot(x_ref[...], w_ref[...])
    out = _layer_norm(ALPHA * h_ref[...] + acc, g_ref[...], b_ref[...])
    o_ref[...] = out
    if packed:
        _pack_row(out, refs[-2], refs[-1])


def proj_res_ln(xs, ws, h, g, b, tm, packed=False):
    n = h.shape[0]
    n_in = len(xs)
    in_specs = [pl.BlockSpec((tm, x.shape[1]), lambda i: (i, 0)) for x in xs]
    in_specs += [_const_spec(w.shape) for w in ws]
    in_specs += [pl.BlockSpec((tm, D_MODEL), lambda i: (i, 0)),
                 _const_spec((1, D_MODEL)), _const_spec((1, D_MODEL))]
    out_specs = [pl.BlockSpec((tm, D_MODEL), lambda i: (i, 0))]
    out_shape = [jax.ShapeDtypeStruct((n, D_MODEL), F32)]
    if packed:
        out_specs += [pl.BlockSpec((tm, PACK_W), lambda i: (i, 0))] * 2
        out_shape += [jax.ShapeDtypeStruct((n, PACK_W), jnp.int32)] * 2
    res = pl.pallas_call(
        functools.partial(_proj_res_ln_kernel, n_in=n_in, packed=packed),
        grid=(n // tm,),
        in_specs=in_specs,
        out_specs=out_specs,
        out_shape=out_shape,
        compiler_params=_cparams(("parallel",)),
        name="proj_res_ln",
    )(*xs, *ws, h, g, b)
    return res if packed else res[0]


def _ffn_res_ln_kernel(h_ref, wg_ref, wu_ref, wd_ref, g_ref, b_ref, o_ref, act_ref, *, tf):
    h = h_ref[...]
    hb = h.astype(BF16)
    for j in range(0, D_FF_DENSE, tf):
        gate = _dot(hb, wg_ref[:, j:j + tf])
        up = _dot(hb, wu_ref[:, j:j + tf])
        act_ref[:, j:j + tf] = (_silu(gate) * up).astype(BF16)
    y = _dot(act_ref[...], wd_ref[...])
    o_ref[...] = _layer_norm(ALPHA * h + y, g_ref[...], b_ref[...])


def ffn_res_ln(h, wg, wu, wd, g, b, tm, tf=256):
    n = h.shape[0]
    return pl.pallas_call(
        functools.partial(_ffn_res_ln_kernel, tf=tf),
        grid=(n // tm,),
        in_specs=[pl.BlockSpec((tm, D_MODEL), lambda i: (i, 0)),
                  _const_spec(wg.shape), _const_spec(wu.shape), _const_spec(wd.shape),
                  _const_spec((1, D_MODEL)), _const_spec((1, D_MODEL))],
        out_specs=pl.BlockSpec((tm, D_MODEL), lambda i: (i, 0)),
        out_shape=jax.ShapeDtypeStruct((n, D_MODEL), F32),
        scratch_shapes=[pltpu.VMEM((tm, D_FF_DENSE), BF16)],
        compiler_params=_cparams(("parallel",)),
        name="ffn_res_ln",
    )(h, wg, wu, wd, g, b)


def _qkv_kernel(h_ref, w_ref, q_ref, k_ref, v_ref):
    hb = h_ref[...].astype(BF16)
    d = D_MODEL
    scale = NA_HEAD_DIM ** -0.5
    for j in range(0, d, 512):
        q_ref[:, j:j + 512] = (_dot(hb, w_ref[:, j:j + 512]) * scale).astype(BF16)
        k_ref[:, j:j + 512] = _dot(hb, w_ref[:, d + j:d + j + 512]).astype(BF16)
        v_ref[:, j:j + 512] = _dot(hb, w_ref[:, 2 * d + j:2 * d + j + 512]).astype(BF16)


def qkv_proj(h, w, tm):
    n = h.shape[0]
    row = pl.BlockSpec((tm, D_MODEL), lambda i: (i, 0))
    sds = jax.ShapeDtypeStruct((n, D_MODEL), BF16)
    return pl.pallas_call(
        _qkv_kernel,
        grid=(n // tm,),
        in_specs=[row, _const_spec(w.shape)],
        out_specs=[row, row, row],
        out_shape=[sds, sds, sds],
        compiler_params=_cparams(("parallel",)),
        name="qkv_proj",
    )(h, w)


def _natten_kernel(q_ref, k_ref, v_ref, bias_ref, o_ref, *, rows, hw, rq):
    nk = NA_WIN_H * GRID_W
    npair = hw // LANES
    lo_half = lax.broadcasted_iota(jnp.int32, (GRID_W, LANES), 1) < NA_HEAD_DIM
    starts = []
    scores = []
    for r in range(rq):
        i = pl.program_id(2) * rq + r
        r0 = jnp.clip(i - NA_WIN_H // 2, 0, rows - NA_WIN_H)
        case = i - r0
        starts.append(pl.multiple_of(r0 * GRID_W, GRID_W))
        for p in range(npair):
            cols = slice(p * LANES, (p + 1) * LANES)
            qp = q_ref[0, r * GRID_W:(r + 1) * GRID_W, cols].astype(F32)
            qbd = jnp.concatenate([jnp.where(lo_half, qp, 0.0), jnp.where(lo_half, 0.0, qp)],
                                  axis=0).astype(BF16)
            s = lax.dot_general(qbd, k_ref[0, pl.ds(starts[r], nk), cols], (((1,), (1,)), ((), ())),
                                preferred_element_type=F32)
            scores.append(s + bias_ref[case, p])
    s_all = jnp.concatenate(scores, axis=0)
    e_f32 = jnp.exp(s_all - jnp.max(s_all, axis=-1, keepdims=True))
    inv_all = 1.0 / jnp.sum(e_f32, axis=-1, keepdims=True)
    e_all = e_f32.astype(BF16)
    for r in range(rq):
        for p in range(npair):
            cols = slice(p * LANES, (p + 1) * LANES)
            c0 = (r * npair + p) * 2 * GRID_W
            e = e_all[c0:c0 + 2 * GRID_W]
            o2 = _dot(e, v_ref[0, pl.ds(starts[r], nk), cols]) * inv_all[c0:c0 + 2 * GRID_W]
            o_ref[0, r * GRID_W:(r + 1) * GRID_W, cols] = jnp.where(
                lo_half, o2[:GRID_W], o2[GRID_W:]).astype(BF16)


def _natten_bias_table(rpb):
    j = np.arange(GRID_W)[:, None]
    kc = np.arange(GRID_W)[None, :]
    cstart = np.clip(j - NA_WIN_W // 2, 0, GRID_W - NA_WIN_W)
    valid = (kc >= cstart) & (kc < cstart + NA_WIN_W)
    coff = np.clip(kc - j + NA_WIN_W - 1, 0, 2 * NA_WIN_W - 2)
    dd = np.arange(NA_WIN_H)[:, None]
    a = np.arange(NA_WIN_H)[None, :]
    roff = a + (NA_WIN_H - 1) - dd
    rows = rpb.astype(F32)[:, roff, :]
    onehot = (coff[None, :, :] == np.arange(2 * NA_WIN_W - 1)[:, None, None]) & valid[None]
    t = jnp.einsum("hdac,cjk->dhjak", rows, jnp.asarray(onehot, F32),
                   precision=lax.Precision.HIGHEST)
    t = jnp.where(valid[None, None, :, None, :], t, NEG_BIG)
    return t.reshape(NA_WIN_H, NA_HEADS // 2, 2 * GRID_W, NA_WIN_H * GRID_W)


def natten(q, k, v, bias, hw=512, rq=8):
    bsz, t, d = q.shape
    rows = t // GRID_W
    nh = d // hw
    pp = hw // LANES
    resident = dict(pipeline_mode=pl.Buffered(1))
    return pl.pallas_call(
        functools.partial(_natten_kernel, rows=rows, hw=hw, rq=rq),
        grid=(bsz, nh, rows // rq),
        in_specs=[pl.BlockSpec((1, rq * GRID_W, hw), lambda b, hh, i: (b, i, hh)),
                  pl.BlockSpec((1, t, hw), lambda b, hh, i: (b, 0, hh), **resident),
                  pl.BlockSpec((1, t, hw), lambda b, hh, i: (b, 0, hh), **resident),
                  pl.BlockSpec((NA_WIN_H, pp, 2 * GRID_W, NA_WIN_H * GRID_W),
                               lambda b, hh, i: (0, hh, 0, 0), **resident)],
        out_specs=pl.BlockSpec((1, rq * GRID_W, hw), lambda b, hh, i: (b, i, hh)),
        out_shape=jax.ShapeDtypeStruct((bsz, t, d), BF16),
        compiler_params=_cparams(("parallel", "parallel", "arbitrary")),
        name="natten",
    )(q, k, v, bias)


def _lane_pick(x, lane, idx):
    return jnp.sum(jnp.where(lane == idx, x, 0.0), axis=-1, keepdims=True)


def _router_kernel(h_ref, w_ref, ltri_ref, meta_ref, gate_ref, cnt_ref, run_ref):
    @pl.when(pl.program_id(0) == 0)
    def _():
        run_ref[...] = jnp.zeros_like(run_ref)

    logits = _dot(h_ref[...].astype(BF16), w_ref[...])
    lane = lax.broadcasted_iota(jnp.int32, logits.shape, 1)
    logits = jnp.where(lane < N_EXPERTS, logits, -jnp.inf)
    m1 = jnp.max(logits, axis=-1, keepdims=True)
    i1 = jnp.min(jnp.where(logits == m1, lane, LANES), axis=-1, keepdims=True)
    rest = jnp.where(lane == i1, -jnp.inf, logits)
    m2 = jnp.max(rest, axis=-1, keepdims=True)
    i2 = jnp.min(jnp.where(rest == m2, lane, LANES), axis=-1, keepdims=True)
    e2 = jnp.exp(m2 - m1)
    inv = 1.0 / (1.0 + e2)
    gate_ref[...] = jnp.where(lane == 0, inv, jnp.where(lane == 1, e2 * inv, 0.0))

    sel = jnp.where(lane == i1, 1.0, jnp.where(lane == i2, 1.0, 0.0))
    before = _dot(ltri_ref[...], sel.astype(BF16)) + run_ref[...]
    rank1 = _lane_pick(before, lane, i1)
    rank2 = _lane_pick(before, lane, i2)
    run_ref[...] += jnp.sum(sel, axis=0, keepdims=True)
    cnt_ref[...] = jnp.broadcast_to(run_ref[...], cnt_ref.shape)
    packed = jnp.where(lane == 0, rank1, jnp.where(lane == 1, rank2, jnp.where(
        lane == 2, i1.astype(F32), jnp.where(lane == 3, i2.astype(F32), 0.0))))
    meta_ref[...] = packed.T[0:8, :]


def router(h, w_pad, tm):
    n = h.shape[0]
    ltri = jnp.asarray(np.tril(np.ones((tm, tm), np.float32), -1), BF16)
    return pl.pallas_call(
        _router_kernel,
        grid=(n // tm,),
        in_specs=[pl.BlockSpec((tm, D_MODEL), lambda i: (i, 0)), _const_spec(w_pad.shape),
                  _const_spec((tm, tm))],
        out_specs=[pl.BlockSpec((8, tm), lambda i: (0, i)),
                   pl.BlockSpec((tm, LANES), lambda i: (i, 0)),
                   _const_spec((8, LANES))],
        out_shape=[jax.ShapeDtypeStruct((8, n), F32),
                   jax.ShapeDtypeStruct((n, LANES), F32),
                   jax.ShapeDtypeStruct((8, LANES), F32)],
        scratch_shapes=[pltpu.VMEM((1, LANES), F32)],
        compiler_params=_cparams(("arbitrary",)),
        name="router",
    )(h, w_pad, ltri)


def _moe_pos_kernel(off_ref, meta_ref, pos_ref):
    m = meta_ref[...]
    for k in range(2):
        rank = m[k:k + 1, :].astype(jnp.int32)
        expert = m[2 + k:3 + k, :].astype(jnp.int32)
        base = jnp.zeros_like(rank)
        for e in range(N_EXPERTS):
            base = jnp.where(expert == e, off_ref[e], base)
        pos_ref[k:k + 1, :] = rank + base


def moe_positions(meta, offsets, tn):
    n = meta.shape[1]
    return pl.pallas_call(
        _moe_pos_kernel,
        grid_spec=pltpu.PrefetchScalarGridSpec(
            num_scalar_prefetch=1,
            grid=(n // tn,),
            in_specs=[pl.BlockSpec((8, tn), lambda i, off: (0, i))],
            out_specs=pl.BlockSpec((2, tn), lambda i, off: (0, i)),
        ),
        out_shape=jax.ShapeDtypeStruct((2, n), jnp.int32),
        compiler_params=_cparams(("parallel",)),
        name="moe_positions",
    )(offsets, meta)


def _sc_mesh():
    return plsc.VectorSubcoreMesh(core_axis_name="core", subcore_axis_name="subcore")


def sc_dispatch_rows(x, pos, out_rows):
    n, dim = x.shape

    @pl.kernel(out_type=jax.ShapeDtypeStruct((out_rows, dim), x.dtype), mesh=_sc_mesh(), scratch_types=[])
    def k(x_hbm, i0_hbm, i1_hbm, o_hbm):
        def body(x_v, i0_v, i1_v):
            pltpu.sync_copy(x_v, o_hbm.at[i0_v.at[0]])
            pltpu.sync_copy(x_v, o_hbm.at[i1_v.at[0]])

        pltpu.emit_pipeline(
            body, grid=(n // SC_WINDOW,),
            in_specs=[pl.BlockSpec((SC_WINDOW, dim), index_map=lambda i: (i, 0)),
                      pl.BlockSpec((1, SC_WINDOW), index_map=lambda i: (0, i)),
                      pl.BlockSpec((1, SC_WINDOW), index_map=lambda i: (0, i))],
            out_specs=[],
            core_axis_name=("core", "subcore"), dimension_semantics=(pltpu.PARALLEL,),
        )(x_hbm, i0_hbm, i1_hbm)

    return k(x, pos[0:1], pos[1:2])


def sc_gather_rows(x, idx):
    ni = idx.shape[1]
    dim = x.shape[1]

    @pl.kernel(out_type=jax.ShapeDtypeStruct((ni, dim), x.dtype), mesh=_sc_mesh(), scratch_types=[])
    def k(x_hbm, i_hbm, o_hbm):
        def body(i_v, o_v):
            pltpu.sync_copy(x_hbm.at[i_v.at[0]], o_v)

        pltpu.emit_pipeline(
            body, grid=(ni // SC_WINDOW,),
            in_specs=[pl.BlockSpec((1, SC_WINDOW), index_map=lambda i: (0, i))],
            out_specs=[pl.BlockSpec((SC_WINDOW, dim), index_map=lambda i: (i, 0))],
            core_axis_name=("core", "subcore"), dimension_semantics=(pltpu.PARALLEL,),
        )(i_hbm, o_hbm)

    return k(x, idx)


def _moe_expert_kernel(te_ref, nu_ref, xa_ref, xb_ref, wg_ref, wu_ref, wd_ref, ya_ref, yb_ref, act_ref, *, tf):
    @pl.when(pl.program_id(0) < nu_ref[0])
    def _():
        xb = _unpack_row(xa_ref[...], xb_ref[...]).astype(BF16)
        for j in range(0, D_FF_EXPERT, tf):
            gate = _dot(xb, wg_ref[0, :, j:j + tf])
            up = _dot(xb, wu_ref[0, :, j:j + tf])
            act_ref[:, j:j + tf] = (_silu(gate) * up).astype(BF16)
        _pack_row(_dot(act_ref[...], wd_ref[0]), ya_ref, yb_ref)


def moe_experts(xa, xb, tile_expert, n_used, wg, wu, wd, tf):
    rows = xa.shape[0]
    nt = rows // MOE_ROW_TILE
    xspec = pl.BlockSpec((MOE_ROW_TILE, PACK_W), lambda i, te, nu: (i, 0))
    sds = jax.ShapeDtypeStruct((rows, PACK_W), jnp.int32)
    return pl.pallas_call(
        functools.partial(_moe_expert_kernel, tf=tf),
        grid_spec=pltpu.PrefetchScalarGridSpec(
            num_scalar_prefetch=2,
            grid=(nt,),
            in_specs=[xspec, xspec,
                      pl.BlockSpec((1, D_MODEL, D_FF_EXPERT), lambda i, te, nu: (te[i], 0, 0)),
                      pl.BlockSpec((1, D_MODEL, D_FF_EXPERT), lambda i, te, nu: (te[i], 0, 0)),
                      pl.BlockSpec((1, D_FF_EXPERT, D_MODEL), lambda i, te, nu: (te[i], 0, 0))],
            out_specs=[xspec, xspec],
            scratch_shapes=[pltpu.VMEM((MOE_ROW_TILE, D_FF_EXPERT), BF16)],
        ),
        out_shape=[sds, sds],
        compiler_params=_cparams(("arbitrary",)),
        name="moe_experts",
    )(tile_expert, n_used, xa, xb, wg, wu, wd)


def _moe_combine_kernel(h_ref, gate_ref, ya0_ref, yb0_ref, ya1_ref, yb1_ref, g_ref, b_ref, o_ref):
    gate = gate_ref[...]
    lane = lax.broadcasted_iota(jnp.int32, gate.shape, 1)
    g0 = _lane_pick(gate, lane, 0)
    g1 = _lane_pick(gate, lane, 1)
    y = g0 * _unpack_row(ya0_ref[...], yb0_ref[...]) + g1 * _unpack_row(ya1_ref[...], yb1_ref[...])
    o_ref[...] = _layer_norm(ALPHA * h_ref[...] + y, g_ref[...], b_ref[...])


def moe_combine_res_ln(h, gate, ya, yb, g, b, tm):
    n = h.shape[0]
    nb = n // tm
    first = pl.BlockSpec((tm, PACK_W), lambda i: (i, 0))
    second = pl.BlockSpec((tm, PACK_W), lambda i: (nb + i, 0))
    return pl.pallas_call(
        _moe_combine_kernel,
        grid=(nb,),
        in_specs=[pl.BlockSpec((tm, D_MODEL), lambda i: (i, 0)),
                  pl.BlockSpec((tm, LANES), lambda i: (i, 0)),
                  first, first, second, second,
                  _const_spec((1, D_MODEL)), _const_spec((1, D_MODEL))],
        out_specs=pl.BlockSpec((tm, D_MODEL), lambda i: (i, 0)),
        out_shape=jax.ShapeDtypeStruct((n, D_MODEL), F32),
        compiler_params=_cparams(("parallel",)),
        name="moe_combine_res_ln",
    )(h, gate, ya, yb, ya, yb, g, b)


def _row(v):
    return v.reshape(1, -1).astype(F32)


def _pad_lanes(v, fill=0.0):
    v = v.astype(F32)
    pad = LANES - v.shape[-1]
    return jnp.concatenate([v, jnp.full(v.shape[:-1] + (pad,), fill, F32)], axis=-1)


def prepare_weights(p):
    w = {}
    w_in = p["ev_w_in"][0]
    o1 = D_INNER
    o2 = o1 + CONV_DIM
    o3 = o2 + 2 * SSD_HEADS
    w["wz"] = w_in[:, :o1].astype(BF16)
    w["wx"] = w_in[:, o1:o2].astype(BF16)
    w["wdt"] = _pad_lanes(w_in[:, o2:o3]).astype(BF16)
    w["wu"] = w_in[:, o3:].astype(BF16)
    w["ln_in_g"], w["ln_in_b"] = _row(p["ln_in_g"]), _row(p["ln_in_b"])
    w["conv_w"] = p["ev_conv_w"][0].astype(F32)
    w["conv_b"] = _row(p["ev_conv_b"][0])
    w["dt_bias"] = _pad_lanes(jnp.concatenate([p["ev_dt_bias_f"][0], p["ev_dt_bias_b"][0]])[None, :])
    w["a_log"] = _pad_lanes(jnp.concatenate([p["ev_a_log_f"][0], p["ev_a_log_b"][0]])[None, :])
    w["d_skip"] = jnp.repeat(p["ev_d_skip"][0].astype(F32), D_INNER // SSD_HEADS)[None, :]
    w["gnorm"] = _row(p["ev_gnorm_w"][0])
    w_out = p["ev_w_out"][0]
    w["wo_ssd"] = w_out[:D_INNER].astype(BF16)
    w["wo_fn"] = w_out[D_INNER:].astype(BF16)
    w["ev_ln1_g"], w["ev_ln1_b"] = _row(p["ev_ln1_g"][0]), _row(p["ev_ln1_b"][0])
    w["ffn_wg"] = p["ev_ffn_wg"][0].astype(BF16)
    w["ffn_wu"] = p["ev_ffn_wu"][0].astype(BF16)
    w["ffn_wd"] = p["ev_ffn_wd"][0].astype(BF16)
    w["ev_ln2_g"], w["ev_ln2_b"] = _row(p["ev_ln2_g"][0]), _row(p["ev_ln2_b"][0])
    w["w_qkv"] = p["od_w_qkv"][0].astype(BF16)
    w["na_bias"] = _natten_bias_table(p["od_rpb"][0])
    w["od_w_out"] = p["od_w_out"][0].astype(BF16)
    w["od_ln1_g"], w["od_ln1_b"] = _row(p["od_ln1_g"][0]), _row(p["od_ln1_b"][0])
    w["router"] = _pad_lanes(p["od_router"][0]).astype(BF16)
    w["moe_wg"] = p["od_wg"][0].astype(BF16)
    w["moe_wu"] = p["od_wu"][0].astype(BF16)
    w["moe_wd"] = p["od_wd"][0].astype(BF16)
    w["od_ln2_g"], w["od_ln2_b"] = _row(p["od_ln2_g"][0]), _row(p["od_ln2_b"][0])
    return w


def _pick(n, pref):
    t = pref
    while n % t:
        t //= 2
    return t


def moe_plan(counts, n_rows):
    cnt = counts[0, :N_EXPERTS]
    padded = jnp.ceil(cnt / MOE_ROW_TILE) * MOE_ROW_TILE
    ends = jnp.cumsum(padded)
    offsets = (ends - padded).astype(jnp.int32)
    n_used = (ends[-1:] / MOE_ROW_TILE).astype(jnp.int32)
    starts = jnp.arange(n_rows // MOE_ROW_TILE, dtype=F32) * MOE_ROW_TILE
    tile_expert = jnp.minimum(jnp.sum(starts[:, None] >= ends[None, :], axis=1), N_EXPERTS - 1)
    return offsets, tile_expert.astype(jnp.int32), n_used


def moe_res_ln(h3, xa, xb, w):
    n = h3.shape[0]
    tm = _pick(n, 512)
    n_rows = 2 * n + N_EXPERTS * MOE_ROW_TILE
    meta, gate, counts = router(h3, w["router"], tm)
    offsets, tile_expert, n_used = moe_plan(counts, n_rows)
    pos = moe_positions(meta, offsets, _pick(n, 2048))
    xs_a = sc_dispatch_rows(xa, pos, n_rows)
    xs_b = sc_dispatch_rows(xb, pos, n_rows)
    ys_a, ys_b = moe_experts(xs_a, xs_b, tile_expert, n_used, w["moe_wg"], w["moe_wu"], w["moe_wd"], tf=512)
    flat = pos.reshape(1, 2 * n)
    ya = sc_gather_rows(ys_a, flat)
    yb = sc_gather_rows(ys_b, flat)
    return moe_combine_res_ln(h3, gate, ya, yb, w["od_ln2_g"], w["od_ln2_b"], tm)


def trunk(x, w):
    bsz, t, d = x.shape
    n = bsz * t
    tm = _pick(n, 512)
    nc = t // CHUNK
    xf = x.reshape(n, d)

    h0, z, xbc, dt_raw, u = ln_inproj(xf, w["ln_in_g"], w["ln_in_b"], w["wz"], w["wx"], w["wdt"], w["wu"], tm)
    xbc_act = conv_silu(xbc.reshape(bsz, t, CONV_DIM), w["conv_w"], w["conv_b"],
                        tt=_pick(t, 512), tc=1024)
    pcol, q = ssd_prep(dt_raw.reshape(bsz, t, LANES), w["dt_bias"], w["a_log"], cpp=_pick(nc, 4))
    cps = _pick(nc, 8)
    y_b = ssd_scan(xbc_act, pcol, q, rev=True, cps=cps)
    y_ssd = ssd_scan(xbc_act, pcol, q, rev=False, cps=cps,
                     fused=(y_b, z.reshape(bsz, t, D_INNER), w["d_skip"], w["gnorm"]))
    y_fn = fnet_mix(u.reshape(bsz, t, FNET_WIDTH))
    h1 = proj_res_ln([y_ssd.reshape(n, D_INNER), y_fn.reshape(n, FNET_WIDTH)],
                     [w["wo_ssd"], w["wo_fn"]], h0, w["ev_ln1_g"], w["ev_ln1_b"], tm)
    h2 = ffn_res_ln(h1, w["ffn_wg"], w["ffn_wu"], w["ffn_wd"], w["ev_ln2_g"], w["ev_ln2_b"], tm)

    qh, kh, vh = qkv_proj(h2, w["w_qkv"], tm)
    att = natten(qh.reshape(bsz, t, d), kh.reshape(bsz, t, d), vh.reshape(bsz, t, d), w["na_bias"])
    h3, xa, xb = proj_res_ln([att.reshape(n, d)], [w["od_w_out"]], h2, w["od_ln1_g"], w["od_ln1_b"], tm,
                             packed=True)
    return moe_res_ln(h3, xa, xb, w).reshape(bsz, t, d)


def kernel(x_prompt, x_sample, ln_in_g, ln_in_b, ev_w_in, ev_conv_w, ev_conv_b, ev_dt_bias_f, ev_dt_bias_b,
           ev_a_log_f, ev_a_log_b, ev_d_skip, ev_gnorm_w, ev_w_out, ev_ln1_g, ev_ln1_b, ev_ffn_wg, ev_ffn_wu,
           ev_ffn_wd, ev_ln2_g, ev_ln2_b, od_w_qkv, od_rpb, od_w_out, od_ln1_g, od_ln1_b, od_router, od_wg,
           od_wu, od_wd, od_ln2_g, od_ln2_b):
    params = dict(ln_in_g=ln_in_g, ln_in_b=ln_in_b, ev_w_in=ev_w_in, ev_conv_w=ev_conv_w, ev_conv_b=ev_conv_b,
                  ev_dt_bias_f=ev_dt_bias_f, ev_dt_bias_b=ev_dt_bias_b, ev_a_log_f=ev_a_log_f,
                  ev_a_log_b=ev_a_log_b, ev_d_skip=ev_d_skip, ev_gnorm_w=ev_gnorm_w, ev_w_out=ev_w_out,
                  ev_ln1_g=ev_ln1_g, ev_ln1_b=ev_ln1_b, ev_ffn_wg=ev_ffn_wg, ev_ffn_wu=ev_ffn_wu,
                  ev_ffn_wd=ev_ffn_wd, ev_ln2_g=ev_ln2_g, ev_ln2_b=ev_ln2_b, od_w_qkv=od_w_qkv, od_rpb=od_rpb,
                  od_w_out=od_w_out, od_ln1_g=od_ln1_g, od_ln1_b=od_ln1_b, od_router=od_router, od_wg=od_wg,
                  od_wu=od_wu, od_wd=od_wd, od_ln2_g=od_ln2_g, od_ln2_b=od_ln2_b)
    w = prepare_weights(params)
    return (trunk(x_prompt, w), trunk(x_sample, w))
```

```python
import functools
import math

import numpy as np
import jax
import jax.numpy as jnp
from jax import lax
from jax.experimental import pallas as pl
from jax.experimental.pallas import tpu as pltpu
from jax.experimental.pallas import tpu_sc as plsc

F32 = jnp.float32
BF16 = jnp.bfloat16

D_MODEL = 1024
GRID_W = 64
D_INNER = 2048
SSD_HEADS = 32
SSD_GROUPS = 4
HEADS_PER_GROUP = SSD_HEADS // SSD_GROUPS
GROUP_DIM = D_INNER // SSD_GROUPS
D_STATE = 128
D_CONV = 5
CHUNK = 128
CONV_DIM = D_INNER + 2 * SSD_GROUPS * D_STATE
FNET_GROUPS = 4
FNET_GROUP_DIM = 256
FNET_WIDTH = 1024
FNET_N2 = 128
FNET_SUB = 8
FNET_STEP = 16
NA_HEADS = 16
NA_HEAD_DIM = 64
NA_WIN_H = 8
NA_WIN_W = 16
D_FF_DENSE = 2816
N_EXPERTS = 8
D_FF_EXPERT = 3584
LN_EPS = 1e-5
RMS_EPS = 1e-5
DEPTH = 2
ALPHA = (2 * DEPTH) ** 0.25
LANES = 128
HALO = 16
NEG_BIG = -1e30
LOG2E = 1.4426950408889634
VMEM_LIMIT = 56 * 1024 * 1024
PACK_W = D_MODEL // 4
SC_WINDOW = 128
MOE_ROW_TILE = 512


def _cparams(sem):
    return pltpu.CompilerParams(dimension_semantics=sem, vmem_limit_bytes=VMEM_LIMIT)


def _const_spec(shape):
    nd = len(shape)
    return pl.BlockSpec(shape, lambda *_: (0,) * nd)


def _dot(a, b):
    return jnp.dot(a, b, preferred_element_type=F32)


def _layer_norm(xf, g, b):
    mu = jnp.mean(xf, axis=-1, keepdims=True)
    xc = xf - mu
    var = jnp.mean(xc * xc, axis=-1, keepdims=True)
    return xc * lax.rsqrt(var + LN_EPS) * g + b


def _silu(x):
    return x * jax.nn.sigmoid(x)


def _ln_inproj_kernel(x_ref, g_ref, b_ref, wz_ref, wx_ref, wdt_ref, wu_ref,
                      h_ref, z_ref, xbc_ref, dt_ref, u_ref):
    h = _layer_norm(x_ref[...], g_ref[...], b_ref[...])
    h_ref[...] = h
    hb = h.astype(BF16)
    for j in range(0, D_INNER, 512):
        z_ref[:, j:j + 512] = _dot(hb, wz_ref[:, j:j + 512]).astype(BF16)
    for j in range(0, CONV_DIM, 512):
        xbc_ref[:, j:j + 512] = _dot(hb, wx_ref[:, j:j + 512]).astype(BF16)
    dt_ref[...] = _dot(hb, wdt_ref[...])
    for j in range(0, FNET_WIDTH, 512):
        u_ref[:, j:j + 512] = _dot(hb, wu_ref[:, j:j + 512])


def ln_inproj(x, g, b, wz, wx, wdt, wu, tm):
    n = x.shape[0]
    row = lambda w: pl.BlockSpec((tm, w), lambda i: (i, 0))
    return pl.pallas_call(
        _ln_inproj_kernel,
        grid=(n // tm,),
        in_specs=[row(D_MODEL), _const_spec((1, D_MODEL)), _const_spec((1, D_MODEL)),
                  _const_spec(wz.shape), _const_spec(wx.shape), _const_spec(wdt.shape),
                  _const_spec(wu.shape)],
        out_specs=[row(D_MODEL), row(D_INNER), row(CONV_DIM), row(LANES), row(FNET_WIDTH)],
        out_shape=[jax.ShapeDtypeStruct((n, D_MODEL), F32),
                   jax.ShapeDtypeStruct((n, D_INNER), BF16),
                   jax.ShapeDtypeStruct((n, CONV_DIM), BF16),
                   jax.ShapeDtypeStruct((n, LANES), F32),
                   jax.ShapeDtypeStruct((n, FNET_WIDTH), F32)],
        compiler_params=_cparams(("parallel",)),
        name="ln_inproj",
    )(x, g, b, wz, wx, wdt, wu)


def _conv_silu_kernel(prev_ref, main_ref, next_ref, shift_ref, w_ref, b_ref, o_ref, scr, *, tt, nt):
    i = pl.program_id(1)
    zero = jnp.zeros((HALO, scr.shape[1]), BF16)
    scr[0:HALO, :] = prev_ref[0]
    scr[HALO:HALO + tt, :] = main_ref[0]
    scr[HALO + tt:2 * HALO + tt, :] = next_ref[0]

    @pl.when(i == 0)
    def _():
        scr[0:HALO, :] = zero

    @pl.when(i == nt - 1)
    def _():
        scr[HALO + tt:2 * HALO + tt, :] = zero

    half = D_CONV // 2
    strip = 2 * LANES
    for r0 in range(0, tt, CHUNK):
        for c0 in range(0, scr.shape[1], strip):
            cols = slice(c0, c0 + strip)
            ext = scr[r0:r0 + CHUNK + 2 * HALO, cols]
            acc = b_ref[:, cols] + w_ref[half:half + 1, cols] * ext[HALO:HALO + CHUNK].astype(F32)
            for j, k in enumerate([k for k in range(D_CONV) if k != half]):
                sh = _dot(shift_ref[j * CHUNK:(j + 1) * CHUNK, :], ext)
                acc = acc + w_ref[k:k + 1, cols] * sh
            o_ref[0, r0:r0 + CHUNK, cols] = _silu(acc).astype(BF16)


def _conv_shift_matrix():
    taps = [k for k in range(D_CONV) if k != D_CONV // 2]
    s = np.zeros((len(taps) * CHUNK, CHUNK + 2 * HALO), np.float32)
    for j, k in enumerate(taps):
        for r in range(CHUNK):
            s[j * CHUNK + r, HALO + r + k - D_CONV // 2] = 1.0
    return s


def conv_silu(xbc, w, b, tt, tc):
    bsz, t, c = xbc.shape
    nt = t // tt
    hb = tt // HALO
    kern = functools.partial(_conv_silu_kernel, tt=tt, nt=nt)
    shift = jnp.asarray(_conv_shift_matrix(), BF16)
    return pl.pallas_call(
        kern,
        grid=(bsz, nt, c // tc),
        in_specs=[
            pl.BlockSpec((1, HALO, tc), lambda bi, i, ci: (bi, jnp.maximum(i * hb - 1, 0), ci)),
            pl.BlockSpec((1, tt, tc), lambda bi, i, ci: (bi, i, ci)),
            pl.BlockSpec((1, HALO, tc), lambda bi, i, ci: (bi, jnp.minimum((i + 1) * hb, t // HALO - 1), ci)),
            _const_spec(shift.shape),
            pl.BlockSpec((D_CONV, tc), lambda bi, i, ci: (0, ci)),
            pl.BlockSpec((1, tc), lambda bi, i, ci: (0, ci)),
        ],
        out_specs=pl.BlockSpec((1, tt, tc), lambda bi, i, ci: (bi, i, ci)),
        out_shape=jax.ShapeDtypeStruct((bsz, t, c), BF16),
        scratch_shapes=[pltpu.VMEM((tt + 2 * HALO, tc), BF16)],
        compiler_params=_cparams(("parallel", "parallel", "parallel")),
        name="conv_silu",
    )(xbc, xbc, xbc, shift, w, b)


def _split3(x):
    hi = x.astype(BF16)
    r1 = x - hi.astype(F32)
    mid = r1.astype(BF16)
    lo = (r1 - mid.astype(F32)).astype(BF16)
    return hi, mid, lo


def _ssd_prep_kernel(dt_ref, bias_ref, alog_ref, tri_ref, trit_ref, sel_ref, pcol_ref, q_ref, *, cpp):
    lane = lax.broadcasted_iota(jnp.int32, (CHUNK, LANES), 1)
    fwd = lane < SSD_HEADS
    a_coef = -jnp.exp(alog_ref[...])
    for c in range(cpp):
        rows = slice(c * CHUNK, (c + 1) * CHUNK)
        raw = dt_ref[0, rows, :] + bias_ref[...]
        dt = jnp.maximum(raw, 0.0) + jnp.log1p(jnp.exp(-jnp.abs(raw)))
        a = dt * a_coef
        cs_f = jnp.dot(tri_ref[...], a, precision=lax.Precision.HIGHEST, preferred_element_type=F32)
        cs_b = jnp.dot(trit_ref[...], a, precision=lax.Precision.HIGHEST, preferred_element_type=F32)
        cs = jnp.where(fwd, cs_f, cs_b)
        end = jnp.where(fwd[0:1], cs[CHUNK - 1:CHUNK, :], cs[0:1, :])
        wend = jnp.exp(end - cs) * dt
        hi, mid, lo = _split3(cs * LOG2E)
        src = jnp.concatenate([hi, mid, lo, wend.astype(BF16)], axis=1)
        pc = _dot(src, sel_ref[...]).astype(BF16)
        pcol_ref[0, 0, rows, :] = pc[:, :SSD_GROUPS * LANES]
        pcol_ref[1, 0, rows, :] = pc[:, SSD_GROUPS * LANES:]
        src_t = ((cs - jnp.log(dt)) * LOG2E).T
        for d in range(2):
            for g in range(SSD_GROUPS):
                r0 = d * SSD_HEADS + g * HEADS_PER_GROUP
                q_ref[d, 0, g, c] = src_t[r0:r0 + HEADS_PER_GROUP, :]


def _prep_select_matrix():
    sel = np.zeros((4 * LANES, 2 * SSD_GROUPS * LANES), np.float32)
    for d in range(2):
        for g in range(SSD_GROUPS):
            for q in range(4):
                for i in range(HEADS_PER_GROUP):
                    src = q * LANES + d * SSD_HEADS + g * HEADS_PER_GROUP + i
                    dst = d * SSD_GROUPS * LANES + g * LANES + q * HEADS_PER_GROUP + i
                    sel[src, dst] = 1.0
    return sel


def ssd_prep(dt_raw, bias_row, alog_row, cpp):
    bsz, t, _ = dt_raw.shape
    nc = t // CHUNK
    tri = np.tril(np.ones((CHUNK, CHUNK), np.float32))
    kern = functools.partial(_ssd_prep_kernel, cpp=cpp)
    return pl.pallas_call(
        kern,
        grid=(bsz, nc // cpp),
        in_specs=[pl.BlockSpec((1, cpp * CHUNK, LANES), lambda b, j: (b, j, 0)),
                  _const_spec((1, LANES)), _const_spec((1, LANES)),
                  _const_spec((CHUNK, CHUNK)), _const_spec((CHUNK, CHUNK)),
                  _const_spec((4 * LANES, 2 * SSD_GROUPS * LANES))],
        out_specs=[pl.BlockSpec((2, 1, cpp * CHUNK, SSD_GROUPS * LANES), lambda b, j: (0, b, j, 0)),
                   pl.BlockSpec((2, 1, SSD_GROUPS, cpp, HEADS_PER_GROUP, CHUNK),
                                lambda b, j: (0, b, 0, j, 0, 0))],
        out_shape=[jax.ShapeDtypeStruct((2, bsz, t, SSD_GROUPS * LANES), BF16),
                   jax.ShapeDtypeStruct((2, bsz, SSD_GROUPS, nc, HEADS_PER_GROUP, CHUNK), F32)],
        compiler_params=_cparams(("parallel", "parallel")),
        name="ssd_prep",
    )(dt_raw, bias_row, alog_row, jnp.asarray(tri), jnp.asarray(tri.T),
      jnp.asarray(_prep_select_matrix(), BF16))


def _expand_matrix():
    e = np.zeros((LANES, HEADS_PER_GROUP * LANES + GROUP_DIM), np.float32)
    for q in range(3):
        for h in range(HEADS_PER_GROUP):
            e[q * HEADS_PER_GROUP + h, h * LANES:(h + 1) * LANES] = 1.0
    for h in range(HEADS_PER_GROUP):
        e[3 * HEADS_PER_GROUP + h,
          HEADS_PER_GROUP * LANES + h * 64:HEADS_PER_GROUP * LANES + (h + 1) * 64] = 1.0
    return e


def _ssd_kernel(*refs, rev, cps, fuse):
    if fuse:
        xs_ref, b_ref, c_ref, p_ref, q_ref, e_ref, yb_ref, z_ref, dsk_ref, gn_ref, o_ref, st_ref = refs
    else:
        xs_ref, b_ref, c_ref, p_ref, q_ref, e_ref, o_ref, st_ref = refs

    @pl.when(pl.program_id(2) == 0)
    def _():
        st_ref[...] = jnp.zeros_like(st_ref)

    row = lax.broadcasted_iota(jnp.int32, (CHUNK, CHUNK), 0)
    col = lax.broadcasted_iota(jnp.int32, (CHUNK, CHUNK), 1)
    mask = (col >= row) if rev else (col <= row)
    lo_half = lax.broadcasted_iota(jnp.int32, (CHUNK, LANES), 1) < 64
    nb = HEADS_PER_GROUP * LANES
    end = 0 if rev else CHUNK - 1
    order = range(cps - 1, -1, -1) if rev else range(cps)
    for c in order:
        rows = slice(c * CHUNK, (c + 1) * CHUNK)
        xf = xs_ref[0, rows, :].astype(F32)
        bm = b_ref[0, rows, :]
        cm = c_ref[0, rows, :]
        ex = _dot(p_ref[0, 0, rows, :], e_ref[...])
        qv = q_ref[0, 0, 0, c]
        cb = lax.dot_general(cm, bm, (((1,), (1,)), ((), ())), preferred_element_type=F32)
        cs_parts = []
        y_parts = []
        for j in range(HEADS_PER_GROUP // 2):
            h1, h2 = 2 * j, 2 * j + 1
            c1 = ex[:, h1 * LANES:(h1 + 1) * LANES]
            c2 = ex[:, h2 * LANES:(h2 + 1) * LANES]
            cs_parts.append(jnp.where(lo_half, c1, c2))
            w1 = jnp.exp2(jnp.where(mask, c1 - qv[h1:h1 + 1, :], -jnp.inf)) * cb
            w2 = jnp.exp2(jnp.where(mask, c2 - qv[h2:h2 + 1, :], -jnp.inf)) * cb
            wp = jnp.concatenate([w1, w2], axis=1).astype(BF16)
            xp = xf[:, j * LANES:(j + 1) * LANES]
            rhs = jnp.concatenate([jnp.where(lo_half, xp, 0.0), jnp.where(lo_half, 0.0, xp)],
                                  axis=0).astype(BF16)
            y_parts.append(_dot(wp, rhs))
        ydiag = jnp.concatenate(y_parts, axis=1)
        expcs = jnp.exp2(jnp.concatenate(cs_parts, axis=1))
        decay = expcs[end:end + 1, :]
        st = st_ref[...]
        yoff = _dot(cm, st.astype(BF16)) * expcs
        xsw = (xf * ex[:, nb:nb + GROUP_DIM]).astype(BF16)
        st_ref[...] = st * decay + lax.dot_general(bm, xsw, (((0,), (0,)), ((), ())),
                                                   preferred_element_type=F32)
        y = ydiag + yoff
        if fuse:
            y = y + yb_ref[0, rows, :].astype(F32) + dsk_ref[...] * xf
            y = y * _silu(z_ref[0, rows, :].astype(F32))
            ms = jnp.mean(y * y, axis=-1, keepdims=True)
            o_ref[0, rows, :] = (y * lax.rsqrt(ms + RMS_EPS) * gn_ref[...]).astype(BF16)
        else:
            o_ref[0, rows, :] = y.astype(BF16)


def ssd_scan(xbc_act, pcol, q, rev, cps, fused=None):
    bsz, t, _ = xbc_act.shape
    ncb = t // (cps * CHUNK)
    r = cps * CHUNK
    d = 1 if rev else 0
    cidx = (lambda j: ncb - 1 - j) if rev else (lambda j: j)
    e = jnp.asarray(_expand_matrix(), BF16)
    b0 = D_INNER // LANES
    c0 = b0 + SSD_GROUPS
    in_specs = [
        pl.BlockSpec((1, r, GROUP_DIM), lambda b, g, j: (b, cidx(j), g)),
        pl.BlockSpec((1, r, LANES), lambda b, g, j: (b, cidx(j), b0 + g)),
        pl.BlockSpec((1, r, LANES), lambda b, g, j: (b, cidx(j), c0 + g)),
        pl.BlockSpec((1, 1, r, LANES), lambda b, g, j: (d, b, cidx(j), g)),
        pl.BlockSpec((1, 1, 1, cps, HEADS_PER_GROUP, CHUNK), lambda b, g, j: (d, b, g, cidx(j), 0, 0)),
        _const_spec(e.shape),
    ]
    args = [xbc_act, xbc_act, xbc_act, pcol, q, e]
    if fused is not None:
        yb, z, dsk, gn = fused
        in_specs += [
            pl.BlockSpec((1, r, GROUP_DIM), lambda b, g, j: (b, cidx(j), g)),
            pl.BlockSpec((1, r, GROUP_DIM), lambda b, g, j: (b, cidx(j), g)),
            pl.BlockSpec((1, GROUP_DIM), lambda b, g, j: (0, g)),
            pl.BlockSpec((1, GROUP_DIM), lambda b, g, j: (0, g)),
        ]
        args += [yb, z, dsk, gn]
    kern = functools.partial(_ssd_kernel, rev=rev, cps=cps, fuse=fused is not None)
    return pl.pallas_call(
        kern,
        grid=(bsz, SSD_GROUPS, ncb),
        in_specs=in_specs,
        out_specs=pl.BlockSpec((1, r, GROUP_DIM), lambda b, g, j: (b, cidx(j), g)),
        out_shape=jax.ShapeDtypeStruct((bsz, t, D_INNER), BF16),
        scratch_shapes=[pltpu.VMEM((D_STATE, GROUP_DIM), F32)],
        compiler_params=_cparams(("parallel", "parallel", "arbitrary")),
        name="ssd_bwd" if rev else "ssd_fwd",
    )(*args)


def _dft_cos_sin(n):
    k = np.arange(n, dtype=np.float64)
    ang = 2.0 * np.pi * np.outer(k, k) / n
    return np.cos(ang), np.sin(ang)


def _fnet_s1_kernel(u_ref, g_ref, cos_ref, sin_ref, o_ref, *, n1):
    c = FNET_WIDTH
    rows = n1 * FNET_SUB
    re_parts, im_parts = [], []
    for h in range(FNET_STEP // FNET_SUB):
        x = u_ref[0, :, h * FNET_SUB:(h + 1) * FNET_SUB, :].reshape(rows, c).astype(BF16)
        a = _dot(g_ref[...], x)
        ar = a[:rows]
        ai = a[rows:]
        ct = jnp.concatenate([cos_ref[h]] * (c // LANES), axis=1)
        st = jnp.concatenate([sin_ref[h]] * (c // LANES), axis=1)
        re_parts.append((ar * ct + ai * st).reshape(n1, FNET_SUB, c))
        im_parts.append((ai * ct - ar * st).reshape(n1, FNET_SUB, c))
    o_ref[0, 0] = jnp.concatenate(re_parts, axis=1).astype(BF16)
    o_ref[0, 1] = jnp.concatenate(im_parts, axis=1).astype(BF16)


def _fnet_s2_kernel(br_ref, bi_ref, f2_ref, wc_ref, o_ref, *, ks):
    n2 = FNET_N2
    c = FNET_WIDTH
    gs = []
    for kk in range(ks):
        rhs = jnp.concatenate([br_ref[0, 0, kk], bi_ref[0, 0, kk]], axis=0)
        gs.append(_dot(f2_ref[...], rhs).astype(BF16))
    for grp in range(FNET_GROUPS):
        cols = slice(grp * FNET_GROUP_DIM, (grp + 1) * FNET_GROUP_DIM)
        lhs = jnp.concatenate([jnp.concatenate([g[:n2, cols], g[n2:, cols]], axis=1) for g in gs], axis=0)
        res = _dot(lhs, wc_ref[...]).astype(BF16)
        for kk in range(ks):
            o_ref[0, :, kk * c + grp * FNET_GROUP_DIM:kk * c + (grp + 1) * FNET_GROUP_DIM] = (
                res[kk * n2:(kk + 1) * n2])


def fnet_mix(u):
    bsz, t, c = u.shape
    n2 = FNET_N2
    n1 = t // n2
    sub, step = FNET_SUB, FNET_STEP
    c1, s1 = _dft_cos_sin(n1)
    c2, s2 = _dft_cos_sin(n2)
    cc, sc = _dft_cos_sin(FNET_GROUP_DIM)
    g1 = jnp.asarray(np.kron(np.concatenate([c1, -s1], axis=0), np.eye(sub)), BF16)
    f2 = jnp.asarray(np.block([[c2, s2], [-s2, c2]]), BF16)
    scale = 1.0 / math.sqrt(t * FNET_GROUP_DIM)
    wc = jnp.asarray(np.concatenate([cc, sc], axis=0) * scale, BF16)
    k1 = np.arange(n1)[None, :, None]
    pos = (np.arange(n2 // sub)[:, None, None] * sub + np.arange(sub)[None, None, :])
    ang = (2.0 * np.pi * k1 * pos / t).reshape(n2 // sub, n1 * sub)
    tw_cos = jnp.asarray(np.repeat(np.cos(ang)[:, :, None], LANES, axis=2), F32)
    tw_sin = jnp.asarray(np.repeat(np.sin(ang)[:, :, None], LANES, axis=2), F32)

    hs = step // sub
    bv = pl.pallas_call(
        functools.partial(_fnet_s1_kernel, n1=n1),
        grid=(bsz, n2 // step),
        in_specs=[pl.BlockSpec((1, n1, step, c), lambda b, j: (b, 0, j, 0)),
                  _const_spec(g1.shape),
                  pl.BlockSpec((hs, n1 * sub, LANES), lambda b, j: (j, 0, 0)),
                  pl.BlockSpec((hs, n1 * sub, LANES), lambda b, j: (j, 0, 0))],
        out_specs=pl.BlockSpec((1, 2, n1, step, c), lambda b, j: (b, 0, 0, j, 0)),
        out_shape=jax.ShapeDtypeStruct((bsz, 2, n1, n2, c), BF16),
        compiler_params=_cparams(("parallel", "parallel")),
        name="fnet_stage1",
    )(u.reshape(bsz, n1, n2, c), g1, tw_cos, tw_sin)

    ks = _pick(n1, 4)
    y = pl.pallas_call(
        functools.partial(_fnet_s2_kernel, ks=ks),
        grid=(bsz, n1 // ks),
        in_specs=[pl.BlockSpec((1, 1, ks, n2, c), lambda b, k: (b, 0, k, 0, 0)),
                  pl.BlockSpec((1, 1, ks, n2, c), lambda b, k: (b, 1, k, 0, 0)),
                  _const_spec((2 * n2, 2 * n2)),
                  _const_spec((2 * FNET_GROUP_DIM, FNET_GROUP_DIM))],
        out_specs=pl.BlockSpec((1, n2, ks * c), lambda b, k: (b, 0, k)),
        out_shape=jax.ShapeDtypeStruct((bsz, n2, n1 * c), BF16),
        compiler_params=_cparams(("parallel", "parallel")),
        name="fnet_stage2",
    )(bv, bv, f2, wc)
    return y.reshape(bsz, t, c)


def _pack_bf16_pair(a, b):
    lo = lax.bitcast_convert_type(a.astype(BF16).astype(F32), jnp.uint32)
    hi = lax.bitcast_convert_type(b.astype(BF16).astype(F32), jnp.uint32)
    word = lax.shift_right_logical(lo, jnp.uint32(16)) | hi
    return lax.bitcast_convert_type(word, jnp.int32)


def _unpack_bf16_pair(word):
    w = lax.bitcast_convert_type(word, jnp.uint32)
    a = lax.bitcast_convert_type(lax.shift_left(w, jnp.uint32(16)), F32)
    b = lax.bitcast_convert_type(w & jnp.uint32(0xFFFF0000), F32)
    return a, b


def _pack_row(x, a_ref, b_ref):
    a_ref[...] = _pack_bf16_pair(x[:, 0:PACK_W], x[:, PACK_W:2 * PACK_W])
    b_ref[...] = _pack_bf16_pair(x[:, 2 * PACK_W:3 * PACK_W], x[:, 3 * PACK_W:4 * PACK_W])


def _unpack_row(a_word, b_word):
    x0, x1 = _unpack_bf16_pair(a_word)
    x2, x3 = _unpack_bf16_pair(b_word)
    return jnp.concatenate([x0, x1, x2, x3], axis=1)


def _proj_res_ln_kernel(*refs, n_in):
    xs = refs[:n_in]
    ws = refs[n_in:2 * n_in]
    h_ref, g_ref, b_ref, o_ref = refs[2 * n_in:]
    acc = _dot(xs[0][...], ws[0][...])
    for x_ref, w_ref in zip(xs[1:], ws[1:]):
        acc = acc + _dot(x_ref[...], w_ref[...])
    o_ref[...] = _layer_norm(ALPHA * h_ref[...] + acc, g_ref[...], b_ref[...])


def proj_res_ln(xs, ws, h, g, b, tm):
    n = h.shape[0]
    n_in = len(xs)
    in_specs = [pl.BlockSpec((tm, x.shape[1]), lambda i: (i, 0)) for x in xs]
    in_specs += [_const_spec(w.shape) for w in ws]
    in_specs += [pl.BlockSpec((tm, D_MODEL), lambda i: (i, 0)),
                 _const_spec((1, D_MODEL)), _const_spec((1, D_MODEL))]
    return pl.pallas_call(
        functools.partial(_proj_res_ln_kernel, n_in=n_in),
        grid=(n // tm,),
        in_specs=in_specs,
        out_specs=pl.BlockSpec((tm, D_MODEL), lambda i: (i, 0)),
        out_shape=jax.ShapeDtypeStruct((n, D_MODEL), F32),
        compiler_params=_cparams(("parallel",)),
        name="proj_res_ln",
    )(*xs, *ws, h, g, b)


def _resident_spec(shape):
    nd = len(shape)
    return pl.BlockSpec(shape, lambda *_: (0,) * nd, pipeline_mode=pl.Buffered(1))


def _ffn_qkv_kernel(h_ref, wg_ref, wu_ref, wd_ref, g_ref, b_ref, wqkv_ref, o_ref, q_ref, k_ref, v_ref,
                    act_ref, *, tf):
    h = h_ref[...]
    hb = h.astype(BF16)
    for j in range(0, D_FF_DENSE, tf):
        gate = _dot(hb, wg_ref[:, j:j + tf])
        up = _dot(hb, wu_ref[:, j:j + tf])
        act_ref[:, j:j + tf] = (_silu(gate) * up).astype(BF16)
    y = _dot(act_ref[...], wd_ref[...])
    out = _layer_norm(ALPHA * h + y, g_ref[...], b_ref[...])
    o_ref[...] = out
    ob = out.astype(BF16)
    d = D_MODEL
    scale = NA_HEAD_DIM ** -0.5
    for j in range(0, d, 512):
        q_ref[:, j:j + 512] = (_dot(ob, wqkv_ref[:, j:j + 512]) * scale).astype(BF16)
        k_ref[:, j:j + 512] = _dot(ob, wqkv_ref[:, d + j:d + j + 512]).astype(BF16)
        v_ref[:, j:j + 512] = _dot(ob, wqkv_ref[:, 2 * d + j:2 * d + j + 512]).astype(BF16)


def ffn_res_ln_qkv(h, wg, wu, wd, g, b, wqkv, tm, tf=256):
    n = h.shape[0]
    row = pl.BlockSpec((tm, D_MODEL), lambda i: (i, 0))
    sds = jax.ShapeDtypeStruct((n, D_MODEL), BF16)
    return pl.pallas_call(
        functools.partial(_ffn_qkv_kernel, tf=tf),
        grid=(n // tm,),
        in_specs=[row, _resident_spec(wg.shape), _resident_spec(wu.shape), _resident_spec(wd.shape),
                  _const_spec((1, D_MODEL)), _const_spec((1, D_MODEL)), _resident_spec(wqkv.shape)],
        out_specs=[row, row, row, row],
        out_shape=[jax.ShapeDtypeStruct((n, D_MODEL), F32), sds, sds, sds],
        scratch_shapes=[pltpu.VMEM((tm, D_FF_DENSE), BF16)],
        compiler_params=_cparams(("parallel",)),
        name="ffn_res_ln_qkv",
    )(h, wg, wu, wd, g, b, wqkv)


def _natten_kernel(q_ref, k_ref, v_ref, bias_ref, o_ref, *, rows, hw, rq):
    nk = NA_WIN_H * GRID_W
    npair = hw // LANES
    lo_half = lax.broadcasted_iota(jnp.int32, (GRID_W, LANES), 1) < NA_HEAD_DIM
    starts = []
    scores = []
    for r in range(rq):
        i = pl.program_id(2) * rq + r
        r0 = jnp.clip(i - NA_WIN_H // 2, 0, rows - NA_WIN_H)
        case = i - r0
        starts.append(pl.multiple_of(r0 * GRID_W, GRID_W))
        for p in range(npair):
            cols = slice(p * LANES, (p + 1) * LANES)
            qp = q_ref[0, r * GRID_W:(r + 1) * GRID_W, cols].astype(F32)
            qbd = jnp.concatenate([jnp.where(lo_half, qp, 0.0), jnp.where(lo_half, 0.0, qp)],
                                  axis=0).astype(BF16)
            s = lax.dot_general(qbd, k_ref[0, pl.ds(starts[r], nk), cols], (((1,), (1,)), ((), ())),
                                preferred_element_type=F32)
            scores.append(s + bias_ref[case, p])
    s_all = jnp.concatenate(scores, axis=0)
    e_f32 = jnp.exp(s_all - jnp.max(s_all, axis=-1, keepdims=True))
    inv_all = 1.0 / jnp.sum(e_f32, axis=-1, keepdims=True)
    e_all = e_f32.astype(BF16)
    for r in range(rq):
        for p in range(npair):
            cols = slice(p * LANES, (p + 1) * LANES)
            c0 = (r * npair + p) * 2 * GRID_W
            e = e_all[c0:c0 + 2 * GRID_W]
            o2 = _dot(e, v_ref[0, pl.ds(starts[r], nk), cols]) * inv_all[c0:c0 + 2 * GRID_W]
            o_ref[0, r * GRID_W:(r + 1) * GRID_W, cols] = jnp.where(
                lo_half, o2[:GRID_W], o2[GRID_W:]).astype(BF16)


def _natten_bias_table(rpb):
    j = np.arange(GRID_W)[:, None]
    kc = np.arange(GRID_W)[None, :]
    cstart = np.clip(j - NA_WIN_W // 2, 0, GRID_W - NA_WIN_W)
    valid = (kc >= cstart) & (kc < cstart + NA_WIN_W)
    coff = np.clip(kc - j + NA_WIN_W - 1, 0, 2 * NA_WIN_W - 2)
    dd = np.arange(NA_WIN_H)[:, None]
    a = np.arange(NA_WIN_H)[None, :]
    roff = a + (NA_WIN_H - 1) - dd
    rows = rpb.astype(F32)[:, roff, :]
    onehot = (coff[None, :, :] == np.arange(2 * NA_WIN_W - 1)[:, None, None]) & valid[None]
    t = jnp.einsum("hdac,cjk->dhjak", rows, jnp.asarray(onehot, F32),
                   precision=lax.Precision.HIGHEST)
    t = jnp.where(valid[None, None, :, None, :], t, NEG_BIG)
    return t.reshape(NA_WIN_H, NA_HEADS // 2, 2 * GRID_W, NA_WIN_H * GRID_W)


def natten(q, k, v, bias, hw=512, rq=8):
    bsz, t, d = q.shape
    rows = t // GRID_W
    nh = d // hw
    pp = hw // LANES
    resident = dict(pipeline_mode=pl.Buffered(1))
    return pl.pallas_call(
        functools.partial(_natten_kernel, rows=rows, hw=hw, rq=rq),
        grid=(nh, bsz, rows // rq),
        in_specs=[pl.BlockSpec((1, rq * GRID_W, hw), lambda hh, b, i: (b, i, hh)),
                  pl.BlockSpec((1, t, hw), lambda hh, b, i: (b, 0, hh), **resident),
                  pl.BlockSpec((1, t, hw), lambda hh, b, i: (b, 0, hh), **resident),
                  pl.BlockSpec((NA_WIN_H, pp, 2 * GRID_W, NA_WIN_H * GRID_W),
                               lambda hh, b, i: (0, hh, 0, 0), **resident)],
        out_specs=pl.BlockSpec((1, rq * GRID_W, hw), lambda hh, b, i: (b, i, hh)),
        out_shape=jax.ShapeDtypeStruct((bsz, t, d), BF16),
        compiler_params=_cparams(("parallel", "parallel", "arbitrary")),
        name="natten",
    )(q, k, v, bias)


def _lane_pick(x, lane, idx):
    return jnp.sum(jnp.where(lane == idx, x, 0.0), axis=-1, keepdims=True)


def _proj_route_kernel(x_ref, wo_ref, hres_ref, g_ref, b_ref, w_ref, ltri_ref,
                       h_ref, xa_ref, xb_ref, meta_ref, gate_ref, cnt_ref, run_ref):
    @pl.when(pl.program_id(0) == 0)
    def _():
        run_ref[...] = jnp.zeros_like(run_ref)

    h = _layer_norm(ALPHA * hres_ref[...] + _dot(x_ref[...], wo_ref[...]), g_ref[...], b_ref[...])
    h_ref[...] = h
    _pack_row(h, xa_ref, xb_ref)
    logits = _dot(h.astype(BF16), w_ref[...])
    lane = lax.broadcasted_iota(jnp.int32, logits.shape, 1)
    logits = jnp.where(lane < N_EXPERTS, logits, -jnp.inf)
    m1 = jnp.max(logits, axis=-1, keepdims=True)
    i1 = jnp.min(jnp.where(logits == m1, lane, LANES), axis=-1, keepdims=True)
    rest = jnp.where(lane == i1, -jnp.inf, logits)
    m2 = jnp.max(rest, axis=-1, keepdims=True)
    i2 = jnp.min(jnp.where(rest == m2, lane, LANES), axis=-1, keepdims=True)
    e2 = jnp.exp(m2 - m1)
    inv = 1.0 / (1.0 + e2)
    gate_ref[...] = jnp.where(lane == 0, inv, jnp.where(lane == 1, e2 * inv, 0.0))

    sel = jnp.where(lane == i1, 1.0, jnp.where(lane == i2, 1.0, 0.0))
    before = _dot(ltri_ref[...], sel.astype(BF16)) + run_ref[...]
    rank1 = _lane_pick(before, lane, i1)
    rank2 = _lane_pick(before, lane, i2)
    run_ref[...] += jnp.sum(sel, axis=0, keepdims=True)
    cnt_ref[...] = jnp.broadcast_to(run_ref[...], cnt_ref.shape)
    packed = jnp.where(lane == 0, rank1, jnp.where(lane == 1, rank2, jnp.where(
        lane == 2, i1.astype(F32), jnp.where(lane == 3, i2.astype(F32), 0.0))))
    meta_ref[...] = packed.T[0:8, :]


def proj_res_ln_route(x, wo, hres, g, b, w_router, tm):
    n = hres.shape[0]
    ltri = jnp.asarray(np.tril(np.ones((tm, tm), np.float32), -1), BF16)
    row = lambda w: pl.BlockSpec((tm, w), lambda i: (i, 0))
    return pl.pallas_call(
        _proj_route_kernel,
        grid=(n // tm,),
        in_specs=[row(x.shape[1]), _const_spec(wo.shape), row(D_MODEL),
                  _const_spec((1, D_MODEL)), _const_spec((1, D_MODEL)),
                  _const_spec(w_router.shape), _const_spec((tm, tm))],
        out_specs=[row(D_MODEL), row(PACK_W), row(PACK_W),
                   pl.BlockSpec((8, tm), lambda i: (0, i)), row(LANES), _const_spec((8, LANES))],
        out_shape=[jax.ShapeDtypeStruct((n, D_MODEL), F32),
                   jax.ShapeDtypeStruct((n, PACK_W), jnp.int32),
                   jax.ShapeDtypeStruct((n, PACK_W), jnp.int32),
                   jax.ShapeDtypeStruct((8, n), F32),
                   jax.ShapeDtypeStruct((n, LANES), F32),
                   jax.ShapeDtypeStruct((8, LANES), F32)],
        scratch_shapes=[pltpu.VMEM((1, LANES), F32)],
        compiler_params=_cparams(("arbitrary",)),
        name="proj_res_ln_route",
    )(x, wo, hres, g, b, w_router, ltri)


def _moe_pos_kernel(off_ref, meta_ref, pos_ref):
    m = meta_ref[...]
    for k in range(2):
        rank = m[k:k + 1, :].astype(jnp.int32)
        expert = m[2 + k:3 + k, :].astype(jnp.int32)
        base = jnp.zeros_like(rank)
        for e in range(N_EXPERTS):
            base = jnp.where(expert == e, off_ref[e], base)
        pos_ref[k:k + 1, :] = rank + base


def moe_positions(meta, offsets, tn):
    n = meta.shape[1]
    return pl.pallas_call(
        _moe_pos_kernel,
        grid_spec=pltpu.PrefetchScalarGridSpec(
            num_scalar_prefetch=1,
            grid=(n // tn,),
            in_specs=[pl.BlockSpec((8, tn), lambda i, off: (0, i))],
            out_specs=pl.BlockSpec((2, tn), lambda i, off: (0, i)),
        ),
        out_shape=jax.ShapeDtypeStruct((2, n), jnp.int32),
        compiler_params=_cparams(("parallel",)),
        name="moe_positions",
    )(offsets, meta)


def _sc_mesh():
    return plsc.VectorSubcoreMesh(core_axis_name="core", subcore_axis_name="subcore")


def sc_dispatch_rows(x, pos, out_rows):
    n, dim = x.shape

    @pl.kernel(out_type=jax.ShapeDtypeStruct((out_rows, dim), x.dtype), mesh=_sc_mesh(), scratch_types=[])
    def k(x_hbm, i0_hbm, i1_hbm, o_hbm):
        def body(x_v, i0_v, i1_v):
            pltpu.sync_copy(x_v, o_hbm.at[i0_v.at[0]])
            pltpu.sync_copy(x_v, o_hbm.at[i1_v.at[0]])

        pltpu.emit_pipeline(
            body, grid=(n // SC_WINDOW,),
            in_specs=[pl.BlockSpec((SC_WINDOW, dim), index_map=lambda i: (i, 0)),
                      pl.BlockSpec((1, SC_WINDOW), index_map=lambda i: (0, i)),
                      pl.BlockSpec((1, SC_WINDOW), index_map=lambda i: (0, i))],
            out_specs=[],
            core_axis_name=("core", "subcore"), dimension_semantics=(pltpu.PARALLEL,),
        )(x_hbm, i0_hbm, i1_hbm)

    return k(x, pos[0:1], pos[1:2])


def sc_gather_rows(x, idx):
    ni = idx.shape[1]
    dim = x.shape[1]

    @pl.kernel(out_type=jax.ShapeDtypeStruct((ni, dim), x.dtype), mesh=_sc_mesh(), scratch_types=[])
    def k(x_hbm, i_hbm, o_hbm):
        def body(i_v, o_v):
            pltpu.sync_copy(x_hbm.at[i_v.at[0]], o_v)

        pltpu.emit_pipeline(
            body, grid=(ni // SC_WINDOW,),
            in_specs=[pl.BlockSpec((1, SC_WINDOW), index_map=lambda i: (0, i))],
            out_specs=[pl.BlockSpec((SC_WINDOW, dim), index_map=lambda i: (i, 0))],
            core_axis_name=("core", "subcore"), dimension_semantics=(pltpu.PARALLEL,),
        )(i_hbm, o_hbm)

    return k(x, idx)


def _moe_expert_kernel(te_ref, nu_ref, xa_ref, xb_ref, wg_ref, wu_ref, wd_ref, ya_ref, yb_ref, act_ref, *, tf):
    @pl.when(pl.program_id(0) < nu_ref[0])
    def _():
        xb = _unpack_row(xa_ref[...], xb_ref[...]).astype(BF16)
        for j in range(0, D_FF_EXPERT, tf):
            gate = _dot(xb, wg_ref[0, :, j:j + tf])
            up = _dot(xb, wu_ref[0, :, j:j + tf])
            act_ref[:, j:j + tf] = (_silu(gate) * up).astype(BF16)
        _pack_row(_dot(act_ref[...], wd_ref[0]), ya_ref, yb_ref)


def moe_experts(xa, xb, tile_expert, n_used, wg, wu, wd, tf):
    rows = xa.shape[0]
    nt = rows // MOE_ROW_TILE
    xspec = pl.BlockSpec((MOE_ROW_TILE, PACK_W), lambda i, te, nu: (i, 0))
    sds = jax.ShapeDtypeStruct((rows, PACK_W), jnp.int32)
    return pl.pallas_call(
        functools.partial(_moe_expert_kernel, tf=tf),
        grid_spec=pltpu.PrefetchScalarGridSpec(
            num_scalar_prefetch=2,
            grid=(nt,),
            in_specs=[xspec, xspec,
                      pl.BlockSpec((1, D_MODEL, D_FF_EXPERT), lambda i, te, nu: (te[i], 0, 0)),
                      pl.BlockSpec((1, D_MODEL, D_FF_EXPERT), lambda i, te, nu: (te[i], 0, 0)),
                      pl.BlockSpec((1, D_FF_EXPERT, D_MODEL), lambda i, te, nu: (te[i], 0, 0))],
            out_specs=[xspec, xspec],
            scratch_shapes=[pltpu.VMEM((MOE_ROW_TILE, D_FF_EXPERT), BF16)],
        ),
        out_shape=[sds, sds],
        compiler_params=_cparams(("arbitrary",)),
        name="moe_experts",
    )(tile_expert, n_used, xa, xb, wg, wu, wd)


def _moe_combine_kernel(h_ref, gate_ref, ya0_ref, yb0_ref, ya1_ref, yb1_ref, g_ref, b_ref, o_ref):
    gate = gate_ref[...]
    lane = lax.broadcasted_iota(jnp.int32, gate.shape, 1)
    g0 = _lane_pick(gate, lane, 0)
    g1 = _lane_pick(gate, lane, 1)
    y = g0 * _unpack_row(ya0_ref[...], yb0_ref[...]) + g1 * _unpack_row(ya1_ref[...], yb1_ref[...])
    o_ref[...] = _layer_norm(ALPHA * h_ref[...] + y, g_ref[...], b_ref[...])


def moe_combine_res_ln(h, gate, ya, yb, g, b, tm):
    n = h.shape[0]
    nb = n // tm
    first = pl.BlockSpec((tm, PACK_W), lambda i: (i, 0))
    second = pl.BlockSpec((tm, PACK_W), lambda i: (nb + i, 0))
    return pl.pallas_call(
        _moe_combine_kernel,
        grid=(nb,),
        in_specs=[pl.BlockSpec((tm, D_MODEL), lambda i: (i, 0)),
                  pl.BlockSpec((tm, LANES), lambda i: (i, 0)),
                  first, first, second, second,
                  _const_spec((1, D_MODEL)), _const_spec((1, D_MODEL))],
        out_specs=pl.BlockSpec((tm, D_MODEL), lambda i: (i, 0)),
        out_shape=jax.ShapeDtypeStruct((n, D_MODEL), F32),
        compiler_params=_cparams(("parallel",)),
        name="moe_combine_res_ln",
    )(h, gate, ya, yb, ya, yb, g, b)


def _row(v):
    return v.reshape(1, -1).astype(F32)


def _pad_lanes(v, fill=0.0):
    v = v.astype(F32)
    pad = LANES - v.shape[-1]
    return jnp.concatenate([v, jnp.full(v.shape[:-1] + (pad,), fill, F32)], axis=-1)


def prepare_weights(p):
    w = {}
    w_in = p["ev_w_in"][0]
    o1 = D_INNER
    o2 = o1 + CONV_DIM
    o3 = o2 + 2 * SSD_HEADS
    w["wz"] = w_in[:, :o1].astype(BF16)
    w["wx"] = w_in[:, o1:o2].astype(BF16)
    w["wdt"] = _pad_lanes(w_in[:, o2:o3]).astype(BF16)
    w["wu"] = w_in[:, o3:].astype(BF16)
    w["ln_in_g"], w["ln_in_b"] = _row(p["ln_in_g"]), _row(p["ln_in_b"])
    w["conv_w"] = p["ev_conv_w"][0].astype(F32)
    w["conv_b"] = _row(p["ev_conv_b"][0])
    w["dt_bias"] = _pad_lanes(jnp.concatenate([p["ev_dt_bias_f"][0], p["ev_dt_bias_b"][0]])[None, :])
    w["a_log"] = _pad_lanes(jnp.concatenate([p["ev_a_log_f"][0], p["ev_a_log_b"][0]])[None, :])
    w["d_skip"] = jnp.repeat(p["ev_d_skip"][0].astype(F32), D_INNER // SSD_HEADS)[None, :]
    w["gnorm"] = _row(p["ev_gnorm_w"][0])
    w_out = p["ev_w_out"][0]
    w["wo_ssd"] = w_out[:D_INNER].astype(BF16)
    w["wo_fn"] = w_out[D_INNER:].astype(BF16)
    w["ev_ln1_g"], w["ev_ln1_b"] = _row(p["ev_ln1_g"][0]), _row(p["ev_ln1_b"][0])
    w["ffn_wg"] = p["ev_ffn_wg"][0].astype(BF16)
    w["ffn_wu"] = p["ev_ffn_wu"][0].astype(BF16)
    w["ffn_wd"] = p["ev_ffn_wd"][0].astype(BF16)
    w["ev_ln2_g"], w["ev_ln2_b"] = _row(p["ev_ln2_g"][0]), _row(p["ev_ln2_b"][0])
    w["w_qkv"] = p["od_w_qkv"][0].astype(BF16)
    w["na_bias"] = _natten_bias_table(p["od_rpb"][0])
    w["od_w_out"] = p["od_w_out"][0].astype(BF16)
    w["od_ln1_g"], w["od_ln1_b"] = _row(p["od_ln1_g"][0]), _row(p["od_ln1_b"][0])
    w["router"] = _pad_lanes(p["od_router"][0]).astype(BF16)
    w["moe_wg"] = p["od_wg"][0].astype(BF16)
    w["moe_wu"] = p["od_wu"][0].astype(BF16)
    w["moe_wd"] = p["od_wd"][0].astype(BF16)
    w["od_ln2_g"], w["od_ln2_b"] = _row(p["od_ln2_g"][0]), _row(p["od_ln2_b"][0])
    return w


def _pick(n, pref):
    t = pref
    while n % t:
        t //= 2
    return t


def moe_plan(counts, n_rows):
    cnt = counts[0, :N_EXPERTS]
    padded = jnp.ceil(cnt / MOE_ROW_TILE) * MOE_ROW_TILE
    ends = jnp.cumsum(padded)
    offsets = (ends - padded).astype(jnp.int32)
    n_used = (ends[-1:] / MOE_ROW_TILE).astype(jnp.int32)
    starts = jnp.arange(n_rows // MOE_ROW_TILE, dtype=F32) * MOE_ROW_TILE
    tile_expert = jnp.minimum(jnp.sum(starts[:, None] >= ends[None, :], axis=1), N_EXPERTS - 1)
    return offsets, tile_expert.astype(jnp.int32), n_used


def moe_res_ln(h3, xa, xb, meta, gate, counts, w):
    n = h3.shape[0]
    tm = _pick(n, 512)
    n_rows = 2 * n + N_EXPERTS * MOE_ROW_TILE
    offsets, tile_expert, n_used = moe_plan(counts, n_rows)
    pos = moe_positions(meta, offsets, _pick(n, 2048))
    xs_a = sc_dispatch_rows(xa, pos, n_rows)
    xs_b = sc_dispatch_rows(xb, pos, n_rows)
    ys_a, ys_b = moe_experts(xs_a, xs_b, tile_expert, n_used, w["moe_wg"], w["moe_wu"], w["moe_wd"], tf=512)
    flat = pos.reshape(1, 2 * n)
    ya = sc_gather_rows(ys_a, flat)
    yb = sc_gather_rows(ys_b, flat)
    return moe_combine_res_ln(h3, gate, ya, yb, w["od_ln2_g"], w["od_ln2_b"], tm)


def trunk(x, w):
    bsz, t, d = x.shape
    n = bsz * t
    tm = _pick(n, 512)
    nc = t // CHUNK
    xf = x.reshape(n, d)

    h0, z, xbc, dt_raw, u = ln_inproj(xf, w["ln_in_g"], w["ln_in_b"], w["wz"], w["wx"], w["wdt"], w["wu"], tm)
    xbc_act = conv_silu(xbc.reshape(bsz, t, CONV_DIM), w["conv_w"], w["conv_b"],
                        tt=_pick(t, 512), tc=1024)
    pcol, q = ssd_prep(dt_raw.reshape(bsz, t, LANES), w["dt_bias"], w["a_log"], cpp=_pick(nc, 4))
    cps = _pick(nc, 16)
    y_b = ssd_scan(xbc_act, pcol, q, rev=True, cps=cps)
    y_ssd = ssd_scan(xbc_act, pcol, q, rev=False, cps=cps,
                     fused=(y_b, z.reshape(bsz, t, D_INNER), w["d_skip"], w["gnorm"]))
    y_fn = fnet_mix(u.reshape(bsz, t, FNET_WIDTH))
    h1 = proj_res_ln([y_ssd.reshape(n, D_INNER), y_fn.reshape(n, FNET_WIDTH)],
                     [w["wo_ssd"], w["wo_fn"]], h0, w["ev_ln1_g"], w["ev_ln1_b"], tm)
    h2, qh, kh, vh = ffn_res_ln_qkv(h1, w["ffn_wg"], w["ffn_wu"], w["ffn_wd"], w["ev_ln2_g"], w["ev_ln2_b"],
                                    w["w_qkv"], tm)

    att = natten(qh.reshape(bsz, t, d), kh.reshape(bsz, t, d), vh.reshape(bsz, t, d), w["na_bias"])
    h3, xa, xb, meta, gate, counts = proj_res_ln_route(att.reshape(n, d), w["od_w_out"], h2, w["od_ln1_g"],
                                                       w["od_ln1_b"], w["router"], tm)
    return moe_res_ln(h3, xa, xb, meta, gate, counts, w).reshape(bsz, t, d)


def kernel(x_prompt, x_sample, ln_in_g, ln_in_b, ev_w_in, ev_conv_w, ev_conv_b, ev_dt_bias_f, ev_dt_bias_b,
           ev_a_log_f, ev_a_log_b, ev_d_skip, ev_gnorm_w, ev_w_out, ev_ln1_g, ev_ln1_b, ev_ffn_wg, ev_ffn_wu,
           ev_ffn_wd, ev_ln2_g, ev_ln2_b, od_w_qkv, od_rpb, od_w_out, od_ln1_g, od_ln1_b, od_router, od_wg,
           od_wu, od_wd, od_ln2_g, od_ln2_b):
    params = dict(ln_in_g=ln_in_g, ln_in_b=ln_in_b, ev_w_in=ev_w_in, ev_conv_w=ev_conv_w, ev_conv_b=ev_conv_b,
                  ev_dt_bias_f=ev_dt_bias_f, ev_dt_bias_b=ev_dt_bias_b, ev_a_log_f=ev_a_log_f,
                  ev_a_log_b=ev_a_log_b, ev_d_skip=ev_d_skip, ev_gnorm_w=ev_gnorm_w, ev_w_out=ev_w_out,
                  ev_ln1_g=ev_ln1_g, ev_ln1_b=ev_ln1_b, ev_ffn_wg=ev_ffn_wg, ev_ffn_wu=ev_ffn_wu,
                  ev_ffn_wd=ev_ffn_wd, ev_ln2_g=ev_ln2_g, ev_ln2_b=ev_ln2_b, od_w_qkv=od_w_qkv, od_rpb=od_rpb,
                  od_w_out=od_w_out, od_ln1_g=od_ln1_g, od_ln1_b=od_ln1_b, od_router=od_router, od_wg=od_wg,
                  od_wu=od_wu, od_wd=od_wd, od_ln2_g=od_ln2_g, od_ln2_b=od_ln2_b)
    w = prepare_weights(params)
    return (trunk(x_prompt, w), trunk(x_sample, w))
```

```python
import functools
import math

import numpy as np
import jax
import jax.numpy as jnp
from jax import lax
from jax.experimental import pallas as pl
from jax.experimental.pallas import tpu as pltpu
from jax.experimental.pallas import tpu_sc as plsc

F32 = jnp.float32
BF16 = jnp.bfloat16

D_MODEL = 1024
GRID_W = 64
D_INNER = 2048
SSD_HEADS = 32
SSD_GROUPS = 4
HEADS_PER_GROUP = SSD_HEADS // SSD_GROUPS
GROUP_DIM = D_INNER // SSD_GROUPS
D_STATE = 128
D_CONV = 5
CHUNK = 128
CONV_DIM = D_INNER + 2 * SSD_GROUPS * D_STATE
FNET_GROUPS = 4
FNET_GROUP_DIM = 256
FNET_WIDTH = 1024
FNET_N2 = 128
FNET_SUB = 8
FNET_STEP = 16
NA_HEADS = 16
NA_HEAD_DIM = 64
NA_WIN_H = 8
NA_WIN_W = 16
D_FF_DENSE = 2816
N_EXPERTS = 8
D_FF_EXPERT = 3584
LN_EPS = 1e-5
RMS_EPS = 1e-5
DEPTH = 2
ALPHA = (2 * DEPTH) ** 0.25
LANES = 128
HALO = 16
NEG_BIG = -1e30
LOG2E = 1.4426950408889634
VMEM_LIMIT = 56 * 1024 * 1024
PACK_W = D_MODEL // 4
SC_WINDOW = 128
MOE_ROW_TILE = 512


def _cparams(sem):
    return pltpu.CompilerParams(dimension_semantics=sem, vmem_limit_bytes=VMEM_LIMIT)


def _const_spec(shape):
    nd = len(shape)
    return pl.BlockSpec(shape, lambda *_: (0,) * nd)


def _dot(a, b):
    return jnp.dot(a, b, preferred_element_type=F32)


def _layer_norm(xf, g, b):
    mu = jnp.mean(xf, axis=-1, keepdims=True)
    xc = xf - mu
    var = jnp.mean(xc * xc, axis=-1, keepdims=True)
    return xc * lax.rsqrt(var + LN_EPS) * g + b


def _silu(x):
    return x * jax.nn.sigmoid(x)


def _ln_inproj_kernel(x_ref, g_ref, b_ref, wz_ref, wx_ref, wdt_ref, wu_ref,
                      h_ref, z_ref, xbc_ref, dt_ref, u_ref):
    h = _layer_norm(x_ref[...], g_ref[...], b_ref[...])
    h_ref[...] = h
    hb = h.astype(BF16)
    for j in range(0, D_INNER, 512):
        z_ref[:, j:j + 512] = _dot(hb, wz_ref[:, j:j + 512]).astype(BF16)
    for j in range(0, CONV_DIM, 512):
        xbc_ref[:, j:j + 512] = _dot(hb, wx_ref[:, j:j + 512]).astype(BF16)
    dt_ref[...] = _dot(hb, wdt_ref[...])
    for j in range(0, FNET_WIDTH, 512):
        u_ref[:, j:j + 512] = _dot(hb, wu_ref[:, j:j + 512])


def ln_inproj(x, g, b, wz, wx, wdt, wu, tm):
    n = x.shape[0]
    row = lambda w: pl.BlockSpec((tm, w), lambda i: (i, 0))
    return pl.pallas_call(
        _ln_inproj_kernel,
        grid=(n // tm,),
        in_specs=[row(D_MODEL), _const_spec((1, D_MODEL)), _const_spec((1, D_MODEL)),
                  _const_spec(wz.shape), _const_spec(wx.shape), _const_spec(wdt.shape),
                  _const_spec(wu.shape)],
        out_specs=[row(D_MODEL), row(D_INNER), row(CONV_DIM), row(LANES), row(FNET_WIDTH)],
        out_shape=[jax.ShapeDtypeStruct((n, D_MODEL), F32),
                   jax.ShapeDtypeStruct((n, D_INNER), BF16),
                   jax.ShapeDtypeStruct((n, CONV_DIM), BF16),
                   jax.ShapeDtypeStruct((n, LANES), F32),
                   jax.ShapeDtypeStruct((n, FNET_WIDTH), F32)],
        compiler_params=_cparams(("parallel",)),
        name="ln_inproj",
    )(x, g, b, wz, wx, wdt, wu)


def _conv_silu_kernel(prev_ref, main_ref, next_ref, shift_ref, w_ref, b_ref, o_ref, scr, *, tt, nt):
    i = pl.program_id(1)
    zero = jnp.zeros((HALO, scr.shape[1]), BF16)
    scr[0:HALO, :] = prev_ref[0]
    scr[HALO:HALO + tt, :] = main_ref[0]
    scr[HALO + tt:2 * HALO + tt, :] = next_ref[0]

    @pl.when(i == 0)
    def _():
        scr[0:HALO, :] = zero

    @pl.when(i == nt - 1)
    def _():
        scr[HALO + tt:2 * HALO + tt, :] = zero

    half = D_CONV // 2
    strip = 2 * LANES
    for r0 in range(0, tt, CHUNK):
        for c0 in range(0, scr.shape[1], strip):
            cols = slice(c0, c0 + strip)
            ext = scr[r0:r0 + CHUNK + 2 * HALO, cols]
            acc = b_ref[:, cols] + w_ref[half:half + 1, cols] * ext[HALO:HALO + CHUNK].astype(F32)
            for j, k in enumerate([k for k in range(D_CONV) if k != half]):
                sh = _dot(shift_ref[j * CHUNK:(j + 1) * CHUNK, :], ext)
                acc = acc + w_ref[k:k + 1, cols] * sh
            o_ref[0, r0:r0 + CHUNK, cols] = _silu(acc).astype(BF16)


def _conv_shift_matrix():
    taps = [k for k in range(D_CONV) if k != D_CONV // 2]
    s = np.zeros((len(taps) * CHUNK, CHUNK + 2 * HALO), np.float32)
    for j, k in enumerate(taps):
        for r in range(CHUNK):
            s[j * CHUNK + r, HALO + r + k - D_CONV // 2] = 1.0
    return s


def conv_silu(xbc, w, b, tt, tc):
    bsz, t, c = xbc.shape
    nt = t // tt
    hb = tt // HALO
    kern = functools.partial(_conv_silu_kernel, tt=tt, nt=nt)
    shift = jnp.asarray(_conv_shift_matrix(), BF16)
    return pl.pallas_call(
        kern,
        grid=(bsz, nt, c // tc),
        in_specs=[
            pl.BlockSpec((1, HALO, tc), lambda bi, i, ci: (bi, jnp.maximum(i * hb - 1, 0), ci)),
            pl.BlockSpec((1, tt, tc), lambda bi, i, ci: (bi, i, ci)),
            pl.BlockSpec((1, HALO, tc), lambda bi, i, ci: (bi, jnp.minimum((i + 1) * hb, t // HALO - 1), ci)),
            _const_spec(shift.shape),
            pl.BlockSpec((D_CONV, tc), lambda bi, i, ci: (0, ci)),
            pl.BlockSpec((1, tc), lambda bi, i, ci: (0, ci)),
        ],
        out_specs=pl.BlockSpec((1, tt, tc), lambda bi, i, ci: (bi, i, ci)),
        out_shape=jax.ShapeDtypeStruct((bsz, t, c), BF16),
        scratch_shapes=[pltpu.VMEM((tt + 2 * HALO, tc), BF16)],
        compiler_params=_cparams(("parallel", "parallel", "parallel")),
        name="conv_silu",
    )(xbc, xbc, xbc, shift, w, b)


def _split3(x):
    hi = x.astype(BF16)
    r1 = x - hi.astype(F32)
    mid = r1.astype(BF16)
    lo = (r1 - mid.astype(F32)).astype(BF16)
    return hi, mid, lo


def _ssd_prep_kernel(dt_ref, bias_ref, alog_ref, tri_ref, trit_ref, sel_ref, pcol_ref, q_ref, *, cpp):
    lane = lax.broadcasted_iota(jnp.int32, (CHUNK, LANES), 1)
    fwd = lane < SSD_HEADS
    a_coef = -jnp.exp(alog_ref[...])
    for c in range(cpp):
        rows = slice(c * CHUNK, (c + 1) * CHUNK)
        raw = dt_ref[0, rows, :] + bias_ref[...]
        dt = jnp.maximum(raw, 0.0) + jnp.log1p(jnp.exp(-jnp.abs(raw)))
        a = dt * a_coef
        cs_f = jnp.dot(tri_ref[...], a, precision=lax.Precision.HIGHEST, preferred_element_type=F32)
        cs_b = jnp.dot(trit_ref[...], a, precision=lax.Precision.HIGHEST, preferred_element_type=F32)
        cs = jnp.where(fwd, cs_f, cs_b)
        end = jnp.where(fwd[0:1], cs[CHUNK - 1:CHUNK, :], cs[0:1, :])
        wend = jnp.exp(end - cs) * dt
        hi, mid, lo = _split3(cs * LOG2E)
        src = jnp.concatenate([hi, mid, lo, wend.astype(BF16)], axis=1)
        pc = _dot(src, sel_ref[...]).astype(BF16)
        pcol_ref[0, 0, rows, :] = pc[:, :SSD_GROUPS * LANES]
        pcol_ref[1, 0, rows, :] = pc[:, SSD_GROUPS * LANES:]
        src_t = ((cs - jnp.log(dt)) * LOG2E).T
        for d in range(2):
            for g in range(SSD_GROUPS):
                r0 = d * SSD_HEADS + g * HEADS_PER_GROUP
                q_ref[d, 0, g, c] = src_t[r0:r0 + HEADS_PER_GROUP, :]


def _prep_select_matrix():
    sel = np.zeros((4 * LANES, 2 * SSD_GROUPS * LANES), np.float32)
    for d in range(2):
        for g in range(SSD_GROUPS):
            for q in range(4):
                for i in range(HEADS_PER_GROUP):
                    src = q * LANES + d * SSD_HEADS + g * HEADS_PER_GROUP + i
                    dst = d * SSD_GROUPS * LANES + g * LANES + q * HEADS_PER_GROUP + i
                    sel[src, dst] = 1.0
    return sel


def ssd_prep(dt_raw, bias_row, alog_row, cpp):
    bsz, t, _ = dt_raw.shape
    nc = t // CHUNK
    tri = np.tril(np.ones((CHUNK, CHUNK), np.float32))
    kern = functools.partial(_ssd_prep_kernel, cpp=cpp)
    return pl.pallas_call(
        kern,
        grid=(bsz, nc // cpp),
        in_specs=[pl.BlockSpec((1, cpp * CHUNK, LANES), lambda b, j: (b, j, 0)),
                  _const_spec((1, LANES)), _const_spec((1, LANES)),
                  _const_spec((CHUNK, CHUNK)), _const_spec((CHUNK, CHUNK)),
                  _const_spec((4 * LANES, 2 * SSD_GROUPS * LANES))],
        out_specs=[pl.BlockSpec((2, 1, cpp * CHUNK, SSD_GROUPS * LANES), lambda b, j: (0, b, j, 0)),
                   pl.BlockSpec((2, 1, SSD_GROUPS, cpp, HEADS_PER_GROUP, CHUNK),
                                lambda b, j: (0, b, 0, j, 0, 0))],
        out_shape=[jax.ShapeDtypeStruct((2, bsz, t, SSD_GROUPS * LANES), BF16),
                   jax.ShapeDtypeStruct((2, bsz, SSD_GROUPS, nc, HEADS_PER_GROUP, CHUNK), F32)],
        compiler_params=_cparams(("parallel", "parallel")),
        name="ssd_prep",
    )(dt_raw, bias_row, alog_row, jnp.asarray(tri), jnp.asarray(tri.T),
      jnp.asarray(_prep_select_matrix(), BF16))


def _expand_matrix():
    e = np.zeros((LANES, HEADS_PER_GROUP * LANES + GROUP_DIM), np.float32)
    for q in range(3):
        for h in range(HEADS_PER_GROUP):
            e[q * HEADS_PER_GROUP + h, h * LANES:(h + 1) * LANES] = 1.0
    for h in range(HEADS_PER_GROUP):
        e[3 * HEADS_PER_GROUP + h,
          HEADS_PER_GROUP * LANES + h * 64:HEADS_PER_GROUP * LANES + (h + 1) * 64] = 1.0
    return e


def _ssd_kernel(*refs, rev, cps, fuse):
    if fuse:
        xs_ref, b_ref, c_ref, p_ref, q_ref, e_ref, yb_ref, z_ref, dsk_ref, gn_ref, o_ref, st_ref = refs
    else:
        xs_ref, b_ref, c_ref, p_ref, q_ref, e_ref, o_ref, st_ref = refs

    @pl.when(pl.program_id(2) == 0)
    def _():
        st_ref[...] = jnp.zeros_like(st_ref)

    row = lax.broadcasted_iota(jnp.int32, (CHUNK, CHUNK), 0)
    col = lax.broadcasted_iota(jnp.int32, (CHUNK, CHUNK), 1)
    mask = (col >= row) if rev else (col <= row)
    lo_half = lax.broadcasted_iota(jnp.int32, (CHUNK, LANES), 1) < 64
    nb = HEADS_PER_GROUP * LANES
    end = 0 if rev else CHUNK - 1
    order = range(cps - 1, -1, -1) if rev else range(cps)
    for c in order:
        rows = slice(c * CHUNK, (c + 1) * CHUNK)
        xf = xs_ref[0, rows, :].astype(F32)
        bm = b_ref[0, rows, :]
        cm = c_ref[0, rows, :]
        ex = _dot(p_ref[0, 0, rows, :], e_ref[...])
        qv = q_ref[0, 0, 0, c]
        cb = lax.dot_general(cm, bm, (((1,), (1,)), ((), ())), preferred_element_type=F32)
        cs_parts = []
        y_parts = []
        for j in range(HEADS_PER_GROUP // 2):
            h1, h2 = 2 * j, 2 * j + 1
            c1 = ex[:, h1 * LANES:(h1 + 1) * LANES]
            c2 = ex[:, h2 * LANES:(h2 + 1) * LANES]
            cs_parts.append(jnp.where(lo_half, c1, c2))
            w1 = jnp.exp2(jnp.where(mask, c1 - qv[h1:h1 + 1, :], -jnp.inf)) * cb
            w2 = jnp.exp2(jnp.where(mask, c2 - qv[h2:h2 + 1, :], -jnp.inf)) * cb
            wp = jnp.concatenate([w1, w2], axis=1).astype(BF16)
            xp = xf[:, j * LANES:(j + 1) * LANES]
            rhs = jnp.concatenate([jnp.where(lo_half, xp, 0.0), jnp.where(lo_half, 0.0, xp)],
                                  axis=0).astype(BF16)
            y_parts.append(_dot(wp, rhs))
        ydiag = jnp.concatenate(y_parts, axis=1)
        expcs = jnp.exp2(jnp.concatenate(cs_parts, axis=1))
        decay = expcs[end:end + 1, :]
        st = st_ref[...]
        yoff = _dot(cm, st.astype(BF16)) * expcs
        xsw = (xf * ex[:, nb:nb + GROUP_DIM]).astype(BF16)
        st_ref[...] = st * decay + lax.dot_general(bm, xsw, (((0,), (0,)), ((), ())),
                                                   preferred_element_type=F32)
        y = ydiag + yoff
        if fuse:
            y = y + yb_ref[0, rows, :].astype(F32) + dsk_ref[...] * xf
            y = y * _silu(z_ref[0, rows, :].astype(F32))
            ms = jnp.mean(y * y, axis=-1, keepdims=True)
            o_ref[0, rows, :] = (y * lax.rsqrt(ms + RMS_EPS) * gn_ref[...]).astype(BF16)
        else:
            o_ref[0, rows, :] = y.astype(BF16)


def ssd_scan(xbc_act, pcol, q, rev, cps, fused=None):
    bsz, t, _ = xbc_act.shape
    ncb = t // (cps * CHUNK)
    r = cps * CHUNK
    d = 1 if rev else 0
    cidx = (lambda j: ncb - 1 - j) if rev else (lambda j: j)
    e = jnp.asarray(_expand_matrix(), BF16)
    b0 = D_INNER // LANES
    c0 = b0 + SSD_GROUPS
    in_specs = [
        pl.BlockSpec((1, r, GROUP_DIM), lambda b, g, j: (b, cidx(j), g)),
        pl.BlockSpec((1, r, LANES), lambda b, g, j: (b, cidx(j), b0 + g)),
        pl.BlockSpec((1, r, LANES), lambda b, g, j: (b, cidx(j), c0 + g)),
        pl.BlockSpec((1, 1, r, LANES), lambda b, g, j: (d, b, cidx(j), g)),
        pl.BlockSpec((1, 1, 1, cps, HEADS_PER_GROUP, CHUNK), lambda b, g, j: (d, b, g, cidx(j), 0, 0)),
        _const_spec(e.shape),
    ]
    args = [xbc_act, xbc_act, xbc_act, pcol, q, e]
    if fused is not None:
        yb, z, dsk, gn = fused
        in_specs += [
            pl.BlockSpec((1, r, GROUP_DIM), lambda b, g, j: (b, cidx(j), g)),
            pl.BlockSpec((1, r, GROUP_DIM), lambda b, g, j: (b, cidx(j), g)),
            pl.BlockSpec((1, GROUP_DIM), lambda b, g, j: (0, g)),
            pl.BlockSpec((1, GROUP_DIM), lambda b, g, j: (0, g)),
        ]
        args += [yb, z, dsk, gn]
    kern = functools.partial(_ssd_kernel, rev=rev, cps=cps, fuse=fused is not None)
    return pl.pallas_call(
        kern,
        grid=(bsz, SSD_GROUPS, ncb),
        in_specs=in_specs,
        out_specs=pl.BlockSpec((1, r, GROUP_DIM), lambda b, g, j: (b, cidx(j), g)),
        out_shape=jax.ShapeDtypeStruct((bsz, t, D_INNER), BF16),
        scratch_shapes=[pltpu.VMEM((D_STATE, GROUP_DIM), F32)],
        compiler_params=_cparams(("parallel", "parallel", "arbitrary")),
        name="ssd_bwd" if rev else "ssd_fwd",
    )(*args)


def _dft_cos_sin(n):
    k = np.arange(n, dtype=np.float64)
    ang = 2.0 * np.pi * np.outer(k, k) / n
    return np.cos(ang), np.sin(ang)


def _fnet_s1_kernel(u_ref, g_ref, cos_ref, sin_ref, o_ref, *, n1):
    c = FNET_WIDTH
    rows = n1 * FNET_SUB
    re_parts, im_parts = [], []
    for h in range(FNET_STEP // FNET_SUB):
        x = u_ref[0, :, h * FNET_SUB:(h + 1) * FNET_SUB, :].reshape(rows, c).astype(BF16)
        a = _dot(g_ref[...], x)
        ar = a[:rows]
        ai = a[rows:]
        ct = jnp.concatenate([cos_ref[h]] * (c // LANES), axis=1)
        st = jnp.concatenate([sin_ref[h]] * (c // LANES), axis=1)
        re_parts.append((ar * ct + ai * st).reshape(n1, FNET_SUB, c))
        im_parts.append((ai * ct - ar * st).reshape(n1, FNET_SUB, c))
    o_ref[0, 0] = jnp.concatenate(re_parts, axis=1).astype(BF16)
    o_ref[0, 1] = jnp.concatenate(im_parts, axis=1).astype(BF16)


def _fnet_s2_kernel(br_ref, bi_ref, f2_ref, wc_ref, o_ref, *, ks):
    n2 = FNET_N2
    c = FNET_WIDTH
    gs = []
    for kk in range(ks):
        rhs = jnp.concatenate([br_ref[0, 0, kk], bi_ref[0, 0, kk]], axis=0)
        gs.append(_dot(f2_ref[...], rhs).astype(BF16))
    for grp in range(FNET_GROUPS):
        cols = slice(grp * FNET_GROUP_DIM, (grp + 1) * FNET_GROUP_DIM)
        lhs = jnp.concatenate([jnp.concatenate([g[:n2, cols], g[n2:, cols]], axis=1) for g in gs], axis=0)
        res = _dot(lhs, wc_ref[...]).astype(BF16)
        for kk in range(ks):
            o_ref[0, :, kk * c + grp * FNET_GROUP_DIM:kk * c + (grp + 1) * FNET_GROUP_DIM] = (
                res[kk * n2:(kk + 1) * n2])


def fnet_mix(u):
    bsz, t, c = u.shape
    n2 = FNET_N2
    n1 = t // n2
    sub, step = FNET_SUB, FNET_STEP
    c1, s1 = _dft_cos_sin(n1)
    c2, s2 = _dft_cos_sin(n2)
    cc, sc = _dft_cos_sin(FNET_GROUP_DIM)
    g1 = jnp.asarray(np.kron(np.concatenate([c1, -s1], axis=0), np.eye(sub)), BF16)
    f2 = jnp.asarray(np.block([[c2, s2], [-s2, c2]]), BF16)
    scale = 1.0 / math.sqrt(t * FNET_GROUP_DIM)
    wc = jnp.asarray(np.concatenate([cc, sc], axis=0) * scale, BF16)
    k1 = np.arange(n1)[None, :, None]
    pos = (np.arange(n2 // sub)[:, None, None] * sub + np.arange(sub)[None, None, :])
    ang = (2.0 * np.pi * k1 * pos / t).reshape(n2 // sub, n1 * sub)
    tw_cos = jnp.asarray(np.repeat(np.cos(ang)[:, :, None], LANES, axis=2), F32)
    tw_sin = jnp.asarray(np.repeat(np.sin(ang)[:, :, None], LANES, axis=2), F32)

    hs = step // sub
    bv = pl.pallas_call(
        functools.partial(_fnet_s1_kernel, n1=n1),
        grid=(bsz, n2 // step),
        in_specs=[pl.BlockSpec((1, n1, step, c), lambda b, j: (b, 0, j, 0)),
                  _const_spec(g1.shape),
                  pl.BlockSpec((hs, n1 * sub, LANES), lambda b, j: (j, 0, 0)),
                  pl.BlockSpec((hs, n1 * sub, LANES), lambda b, j: (j, 0, 0))],
        out_specs=pl.BlockSpec((1, 2, n1, step, c), lambda b, j: (b, 0, 0, j, 0)),
        out_shape=jax.ShapeDtypeStruct((bsz, 2, n1, n2, c), BF16),
        compiler_params=_cparams(("parallel", "parallel")),
        name="fnet_stage1",
    )(u.reshape(bsz, n1, n2, c), g1, tw_cos, tw_sin)

    ks = _pick(n1, 4)
    y = pl.pallas_call(
        functools.partial(_fnet_s2_kernel, ks=ks),
        grid=(bsz, n1 // ks),
        in_specs=[pl.BlockSpec((1, 1, ks, n2, c), lambda b, k: (b, 0, k, 0, 0)),
                  pl.BlockSpec((1, 1, ks, n2, c), lambda b, k: (b, 1, k, 0, 0)),
                  _const_spec((2 * n2, 2 * n2)),
                  _const_spec((2 * FNET_GROUP_DIM, FNET_GROUP_DIM))],
        out_specs=pl.BlockSpec((1, n2, ks * c), lambda b, k: (b, 0, k)),
        out_shape=jax.ShapeDtypeStruct((bsz, n2, n1 * c), BF16),
        compiler_params=_cparams(("parallel", "parallel")),
        name="fnet_stage2",
    )(bv, bv, f2, wc)
    return y.reshape(bsz, t, c)


def _pack_bf16_pair(a, b):
    lo = lax.bitcast_convert_type(a.astype(BF16).astype(F32), jnp.uint32)
    hi = lax.bitcast_convert_type(b.astype(BF16).astype(F32), jnp.uint32)
    word = lax.shift_right_logical(lo, jnp.uint32(16)) | hi
    return lax.bitcast_convert_type(word, jnp.int32)


def _unpack_bf16_pair(word):
    w = lax.bitcast_convert_type(word, jnp.uint32)
    a = lax.bitcast_convert_type(lax.shift_left(w, jnp.uint32(16)), F32)
    b = lax.bitcast_convert_type(w & jnp.uint32(0xFFFF0000), F32)
    return a, b


def _pack_row(x, a_ref, b_ref):
    a_ref[...] = _pack_bf16_pair(x[:, 0:PACK_W], x[:, PACK_W:2 * PACK_W])
    b_ref[...] = _pack_bf16_pair(x[:, 2 * PACK_W:3 * PACK_W], x[:, 3 * PACK_W:4 * PACK_W])


def _unpack_row(a_word, b_word):
    x0, x1 = _unpack_bf16_pair(a_word)
    x2, x3 = _unpack_bf16_pair(b_word)
    return jnp.concatenate([x0, x1, x2, x3], axis=1)


def _proj_res_ln_kernel(*refs, n_in):
    xs = refs[:n_in]
    ws = refs[n_in:2 * n_in]
    h_ref, g_ref, b_ref, o_ref = refs[2 * n_in:]
    acc = _dot(xs[0][...], ws[0][...])
    for x_ref, w_ref in zip(xs[1:], ws[1:]):
        acc = acc + _dot(x_ref[...], w_ref[...])
    o_ref[...] = _layer_norm(ALPHA * h_ref[...] + acc, g_ref[...], b_ref[...])


def proj_res_ln(xs, ws, h, g, b, tm):
    n = h.shape[0]
    n_in = len(xs)
    in_specs = [pl.BlockSpec((tm, x.shape[1]), lambda i: (i, 0)) for x in xs]
    in_specs += [_const_spec(w.shape) for w in ws]
    in_specs += [pl.BlockSpec((tm, D_MODEL), lambda i: (i, 0)),
                 _const_spec((1, D_MODEL)), _const_spec((1, D_MODEL))]
    return pl.pallas_call(
        functools.partial(_proj_res_ln_kernel, n_in=n_in),
        grid=(n // tm,),
        in_specs=in_specs,
        out_specs=pl.BlockSpec((tm, D_MODEL), lambda i: (i, 0)),
        out_shape=jax.ShapeDtypeStruct((n, D_MODEL), F32),
        compiler_params=_cparams(("parallel",)),
        name="proj_res_ln",
    )(*xs, *ws, h, g, b)


def _resident_spec(shape):
    nd = len(shape)
    return pl.BlockSpec(shape, lambda *_: (0,) * nd, pipeline_mode=pl.Buffered(1))


def _ffn_qkv_kernel(h_ref, wg_ref, wu_ref, wd_ref, g_ref, b_ref, wqkv_ref, o_ref, q_ref, k_ref, v_ref,
                    act_ref, *, tf):
    h = h_ref[...]
    hb = h.astype(BF16)
    for j in range(0, D_FF_DENSE, tf):
        gate = _dot(hb, wg_ref[:, j:j + tf])
        up = _dot(hb, wu_ref[:, j:j + tf])
        act_ref[:, j:j + tf] = (_silu(gate) * up).astype(BF16)
    y = _dot(act_ref[...], wd_ref[...])
    out = _layer_norm(ALPHA * h + y, g_ref[...], b_ref[...])
    o_ref[...] = out
    ob = out.astype(BF16)
    d = D_MODEL
    scale = NA_HEAD_DIM ** -0.5
    for j in range(0, d, 512):
        q_ref[:, j:j + 512] = (_dot(ob, wqkv_ref[:, j:j + 512]) * scale).astype(BF16)
        k_ref[:, j:j + 512] = _dot(ob, wqkv_ref[:, d + j:d + j + 512]).astype(BF16)
        v_ref[:, j:j + 512] = _dot(ob, wqkv_ref[:, 2 * d + j:2 * d + j + 512]).astype(BF16)


def ffn_res_ln_qkv(h, wg, wu, wd, g, b, wqkv, tm, tf=256):
    n = h.shape[0]
    row = pl.BlockSpec((tm, D_MODEL), lambda i: (i, 0))
    sds = jax.ShapeDtypeStruct((n, D_MODEL), BF16)
    return pl.pallas_call(
        functools.partial(_ffn_qkv_kernel, tf=tf),
        grid=(n // tm,),
        in_specs=[row, _resident_spec(wg.shape), _resident_spec(wu.shape), _resident_spec(wd.shape),
                  _const_spec((1, D_MODEL)), _const_spec((1, D_MODEL)), _resident_spec(wqkv.shape)],
        out_specs=[row, row, row, row],
        out_shape=[jax.ShapeDtypeStruct((n, D_MODEL), F32), sds, sds, sds],
        scratch_shapes=[pltpu.VMEM((tm, D_FF_DENSE), BF16)],
        compiler_params=_cparams(("parallel",)),
        name="ffn_res_ln_qkv",
    )(h, wg, wu, wd, g, b, wqkv)


def _natten_kernel(q_ref, k_ref, v_ref, bias_ref, o_ref, *, rows, hw, rq):
    nk = NA_WIN_H * GRID_W
    npair = hw // LANES
    lo_half = lax.broadcasted_iota(jnp.int32, (GRID_W, LANES), 1) < NA_HEAD_DIM
    starts = []
    scores = []
    for r in range(rq):
        i = pl.program_id(2) * rq + r
        r0 = jnp.clip(i - NA_WIN_H // 2, 0, rows - NA_WIN_H)
        case = i - r0
        starts.append(pl.multiple_of(r0 * GRID_W, GRID_W))
        for p in range(npair):
            cols = slice(p * LANES, (p + 1) * LANES)
            qp = q_ref[0, r * GRID_W:(r + 1) * GRID_W, cols].astype(F32)
            qbd = jnp.concatenate([jnp.where(lo_half, qp, 0.0), jnp.where(lo_half, 0.0, qp)],
                                  axis=0).astype(BF16)
            s = lax.dot_general(qbd, k_ref[0, pl.ds(starts[r], nk), cols], (((1,), (1,)), ((), ())),
                                preferred_element_type=F32)
            scores.append(s + bias_ref[case, p])
    s_all = jnp.concatenate(scores, axis=0)
    e_f32 = jnp.exp(s_all - jnp.max(s_all, axis=-1, keepdims=True))
    inv_all = 1.0 / jnp.sum(e_f32, axis=-1, keepdims=True)
    e_all = e_f32.astype(BF16)
    for r in range(rq):
        for p in range(npair):
            cols = slice(p * LANES, (p + 1) * LANES)
            c0 = (r * npair + p) * 2 * GRID_W
            e = e_all[c0:c0 + 2 * GRID_W]
            o2 = _dot(e, v_ref[0, pl.ds(starts[r], nk), cols]) * inv_all[c0:c0 + 2 * GRID_W]
            o_ref[0, r * GRID_W:(r + 1) * GRID_W, cols] = jnp.where(
                lo_half, o2[:GRID_W], o2[GRID_W:]).astype(BF16)


def _natten_bias_table(rpb):
    j = np.arange(GRID_W)[:, None]
    kc = np.arange(GRID_W)[None, :]
    cstart = np.clip(j - NA_WIN_W // 2, 0, GRID_W - NA_WIN_W)
    valid = (kc >= cstart) & (kc < cstart + NA_WIN_W)
    coff = np.clip(kc - j + NA_WIN_W - 1, 0, 2 * NA_WIN_W - 2)
    dd = np.arange(NA_WIN_H)[:, None]
    a = np.arange(NA_WIN_H)[None, :]
    roff = a + (NA_WIN_H - 1) - dd
    rows = rpb.astype(F32)[:, roff, :]
    onehot = (coff[None, :, :] == np.arange(2 * NA_WIN_W - 1)[:, None, None]) & valid[None]
    t = jnp.einsum("hdac,cjk->dhjak", rows, jnp.asarray(onehot, F32),
                   precision=lax.Precision.HIGHEST)
    t = jnp.where(valid[None, None, :, None, :], t, NEG_BIG)
    return t.reshape(NA_WIN_H, NA_HEADS // 2, 2 * GRID_W, NA_WIN_H * GRID_W)


def natten(q, k, v, bias, hw=512, rq=8):
    bsz, t, d = q.shape
    rows = t // GRID_W
    nh = d // hw
    pp = hw // LANES
    resident = dict(pipeline_mode=pl.Buffered(1))
    return pl.pallas_call(
        functools.partial(_natten_kernel, rows=rows, hw=hw, rq=rq),
        grid=(nh, bsz, rows // rq),
        in_specs=[pl.BlockSpec((1, rq * GRID_W, hw), lambda hh, b, i: (b, i, hh)),
                  pl.BlockSpec((1, t, hw), lambda hh, b, i: (b, 0, hh), **resident),
                  pl.BlockSpec((1, t, hw), lambda hh, b, i: (b, 0, hh), **resident),
                  pl.BlockSpec((NA_WIN_H, pp, 2 * GRID_W, NA_WIN_H * GRID_W),
                               lambda hh, b, i: (0, hh, 0, 0), **resident)],
        out_specs=pl.BlockSpec((1, rq * GRID_W, hw), lambda hh, b, i: (b, i, hh)),
        out_shape=jax.ShapeDtypeStruct((bsz, t, d), BF16),
        compiler_params=_cparams(("parallel", "parallel", "arbitrary")),
        name="natten",
    )(q, k, v, bias)


def _lane_pick(x, lane, idx):
    return jnp.sum(jnp.where(lane == idx, x, 0.0), axis=-1, keepdims=True)


def _proj_route_kernel(x_ref, wo_ref, hres_ref, g_ref, b_ref, w_ref, ltri_ref,
                       h_ref, xa_ref, xb_ref, meta_ref, gate_ref, cnt_ref, run_ref):
    @pl.when(pl.program_id(0) == 0)
    def _():
        run_ref[...] = jnp.zeros_like(run_ref)

    h = _layer_norm(ALPHA * hres_ref[...] + _dot(x_ref[...], wo_ref[...]), g_ref[...], b_ref[...])
    h_ref[...] = h
    _pack_row(h, xa_ref, xb_ref)
    logits = _dot(h.astype(BF16), w_ref[...])
    lane = lax.broadcasted_iota(jnp.int32, logits.shape, 1)
    logits = jnp.where(lane < N_EXPERTS, logits, -jnp.inf)
    m1 = jnp.max(logits, axis=-1, keepdims=True)
    i1 = jnp.min(jnp.where(logits == m1, lane, LANES), axis=-1, keepdims=True)
    rest = jnp.where(lane == i1, -jnp.inf, logits)
    m2 = jnp.max(rest, axis=-1, keepdims=True)
    i2 = jnp.min(jnp.where(rest == m2, lane, LANES), axis=-1, keepdims=True)
    e2 = jnp.exp(m2 - m1)
    inv = 1.0 / (1.0 + e2)
    gate_ref[...] = jnp.where(lane == 0, inv, jnp.where(lane == 1, e2 * inv, 0.0))

    sel = jnp.where(lane == i1, 1.0, jnp.where(lane == i2, 1.0, 0.0))
    before = _dot(ltri_ref[...], sel.astype(BF16)) + run_ref[...]
    rank1 = _lane_pick(before, lane, i1)
    rank2 = _lane_pick(before, lane, i2)
    run_ref[...] += jnp.sum(sel, axis=0, keepdims=True)
    cnt_ref[...] = jnp.broadcast_to(run_ref[...], cnt_ref.shape)
    packed = jnp.where(lane == 0, rank1, jnp.where(lane == 1, rank2, jnp.where(
        lane == 2, i1.astype(F32), jnp.where(lane == 3, i2.astype(F32), 0.0))))
    meta_ref[...] = packed.T[0:8, :]


def proj_res_ln_route(x, wo, hres, g, b, w_router, tm):
    n = hres.shape[0]
    ltri = jnp.asarray(np.tril(np.ones((tm, tm), np.float32), -1), BF16)
    row = lambda w: pl.BlockSpec((tm, w), lambda i: (i, 0))
    return pl.pallas_call(
        _proj_route_kernel,
        grid=(n // tm,),
        in_specs=[row(x.shape[1]), _const_spec(wo.shape), row(D_MODEL),
                  _const_spec((1, D_MODEL)), _const_spec((1, D_MODEL)),
                  _const_spec(w_router.shape), _const_spec((tm, tm))],
        out_specs=[row(D_MODEL), row(PACK_W), row(PACK_W),
                   pl.BlockSpec((8, tm), lambda i: (0, i)), row(LANES), _const_spec((8, LANES))],
        out_shape=[jax.ShapeDtypeStruct((n, D_MODEL), F32),
                   jax.ShapeDtypeStruct((n, PACK_W), jnp.int32),
                   jax.ShapeDtypeStruct((n, PACK_W), jnp.int32),
                   jax.ShapeDtypeStruct((8, n), F32),
                   jax.ShapeDtypeStruct((n, LANES), F32),
                   jax.ShapeDtypeStruct((8, LANES), F32)],
        scratch_shapes=[pltpu.VMEM((1, LANES), F32)],
        compiler_params=_cparams(("arbitrary",)),
        name="proj_res_ln_route",
    )(x, wo, hres, g, b, w_router, ltri)


def _moe_pos_kernel(off_ref, meta_ref, pos_ref):
    m = meta_ref[...]
    for k in range(2):
        rank = m[k:k + 1, :].astype(jnp.int32)
        expert = m[2 + k:3 + k, :].astype(jnp.int32)
        base = jnp.zeros_like(rank)
        for e in range(N_EXPERTS):
            base = jnp.where(expert == e, off_ref[e], base)
        pos_ref[k:k + 1, :] = rank + base


def moe_positions(meta, offsets, tn):
    n = meta.shape[1]
    return pl.pallas_call(
        _moe_pos_kernel,
        grid_spec=pltpu.PrefetchScalarGridSpec(
            num_scalar_prefetch=1,
            grid=(n // tn,),
            in_specs=[pl.BlockSpec((8, tn), lambda i, off: (0, i))],
            out_specs=pl.BlockSpec((2, tn), lambda i, off: (0, i)),
        ),
        out_shape=jax.ShapeDtypeStruct((2, n), jnp.int32),
        compiler_params=_cparams(("parallel",)),
        name="moe_positions",
    )(offsets, meta)


def _sc_mesh():
    return plsc.VectorSubcoreMesh(core_axis_name="core", subcore_axis_name="subcore")


def sc_dispatch_rows(x, pos, out_rows):
    n, dim = x.shape

    @pl.kernel(out_type=jax.ShapeDtypeStruct((out_rows, dim), x.dtype), mesh=_sc_mesh(), scratch_types=[])
    def k(x_hbm, i0_hbm, i1_hbm, o_hbm):
        def body(x_v, i0_v, i1_v):
            pltpu.sync_copy(x_v, o_hbm.at[i0_v.at[0]])
            pltpu.sync_copy(x_v, o_hbm.at[i1_v.at[0]])

        pltpu.emit_pipeline(
            body, grid=(n // SC_WINDOW,),
            in_specs=[pl.BlockSpec((SC_WINDOW, dim), index_map=lambda i: (i, 0)),
                      pl.BlockSpec((1, SC_WINDOW), index_map=lambda i: (0, i)),
                      pl.BlockSpec((1, SC_WINDOW), index_map=lambda i: (0, i))],
            out_specs=[],
            core_axis_name=("core", "subcore"), dimension_semantics=(pltpu.PARALLEL,),
        )(x_hbm, i0_hbm, i1_hbm)

    return k(x, pos[0:1], pos[1:2])


def sc_gather_rows(x, idx):
    ni = idx.shape[1]
    dim = x.shape[1]

    @pl.kernel(out_type=jax.ShapeDtypeStruct((ni, dim), x.dtype), mesh=_sc_mesh(), scratch_types=[])
    def k(x_hbm, i_hbm, o_hbm):
        def body(i_v, o_v):
            pltpu.sync_copy(x_hbm.at[i_v.at[0]], o_v)

        pltpu.emit_pipeline(
            body, grid=(ni // SC_WINDOW,),
            in_specs=[pl.BlockSpec((1, SC_WINDOW), index_map=lambda i: (0, i))],
            out_specs=[pl.BlockSpec((SC_WINDOW, dim), index_map=lambda i: (i, 0))],
            core_axis_name=("core", "subcore"), dimension_semantics=(pltpu.PARALLEL,),
        )(i_hbm, o_hbm)

    return k(x, idx)


def _moe_expert_kernel(te_ref, nu_ref, xa_ref, xb_ref, wg_ref, wu_ref, wd_ref, ya_ref, yb_ref, act_ref, *, tf):
    @pl.when(pl.program_id(0) < nu_ref[0])
    def _():
        xb = _unpack_row(xa_ref[...], xb_ref[...]).astype(BF16)
        for j in range(0, D_FF_EXPERT, tf):
            gate = _dot(xb, wg_ref[0, :, j:j + tf])
            up = _dot(xb, wu_ref[0, :, j:j + tf])
            act_ref[:, j:j + tf] = (_silu(gate) * up).astype(BF16)
        _pack_row(_dot(act_ref[...], wd_ref[0]), ya_ref, yb_ref)


def moe_experts(xa, xb, tile_expert, n_used, wg, wu, wd, tf):
    rows = xa.shape[0]
    nt = rows // MOE_ROW_TILE
    xspec = pl.BlockSpec((MOE_ROW_TILE, PACK_W), lambda i, te, nu: (i, 0))
    sds = jax.ShapeDtypeStruct((rows, PACK_W), jnp.int32)
    return pl.pallas_call(
        functools.partial(_moe_expert_kernel, tf=tf),
        grid_spec=pltpu.PrefetchScalarGridSpec(
            num_scalar_prefetch=2,
            grid=(nt,),
            in_specs=[xspec, xspec,
                      pl.BlockSpec((1, D_MODEL, D_FF_EXPERT), lambda i, te, nu: (te[i], 0, 0)),
                      pl.BlockSpec((1, D_MODEL, D_FF_EXPERT), lambda i, te, nu: (te[i], 0, 0)),
                      pl.BlockSpec((1, D_FF_EXPERT, D_MODEL), lambda i, te, nu: (te[i], 0, 0))],
            out_specs=[xspec, xspec],
            scratch_shapes=[pltpu.VMEM((MOE_ROW_TILE, D_FF_EXPERT), BF16)],
        ),
        out_shape=[sds, sds],
        compiler_params=_cparams(("arbitrary",)),
        name="moe_experts",
    )(tile_expert, n_used, xa, xb, wg, wu, wd)


def _moe_combine_kernel(h_ref, gate_ref, ya0_ref, yb0_ref, ya1_ref, yb1_ref, g_ref, b_ref, o_ref):
    gate = gate_ref[...]
    lane = lax.broadcasted_iota(jnp.int32, gate.shape, 1)
    g0 = _lane_pick(gate, lane, 0)
    g1 = _lane_pick(gate, lane, 1)
    y = g0 * _unpack_row(ya0_ref[...], yb0_ref[...]) + g1 * _unpack_row(ya1_ref[...], yb1_ref[...])
    o_ref[...] = _layer_norm(ALPHA * h_ref[...] + y, g_ref[...], b_ref[...])


def moe_combine_res_ln(h, gate, ya, yb, g, b, tm):
    n = h.shape[0]
    nb = n // tm
    first = pl.BlockSpec((tm, PACK_W), lambda i: (i, 0))
    second = pl.BlockSpec((tm, PACK_W), lambda i: (nb + i, 0))
    return pl.pallas_call(
        _moe_combine_kernel,
        grid=(nb,),
        in_specs=[pl.BlockSpec((tm, D_MODEL), lambda i: (i, 0)),
                  pl.BlockSpec((tm, LANES), lambda i: (i, 0)),
                  first, first, second, second,
                  _const_spec((1, D_MODEL)), _const_spec((1, D_MODEL))],
        out_specs=pl.BlockSpec((tm, D_MODEL), lambda i: (i, 0)),
        out_shape=jax.ShapeDtypeStruct((n, D_MODEL), F32),
        compiler_params=_cparams(("parallel",)),
        name="moe_combine_res_ln",
    )(h, gate, ya, yb, ya, yb, g, b)


def _row(v):
    return v.reshape(1, -1).astype(F32)


def _pad_lanes(v, fill=0.0):
    v = v.astype(F32)
    pad = LANES - v.shape[-1]
    return jnp.concatenate([v, jnp.full(v.shape[:-1] + (pad,), fill, F32)], axis=-1)


def prepare_weights(p):
    w = {}
    w_in = p["ev_w_in"][0]
    o1 = D_INNER
    o2 = o1 + CONV_DIM
    o3 = o2 + 2 * SSD_HEADS
    w["wz"] = w_in[:, :o1].astype(BF16)
    w["wx"] = w_in[:, o1:o2].astype(BF16)
    w["wdt"] = _pad_lanes(w_in[:, o2:o3]).astype(BF16)
    w["wu"] = w_in[:, o3:].astype(BF16)
    w["ln_in_g"], w["ln_in_b"] = _row(p["ln_in_g"]), _row(p["ln_in_b"])
    w["conv_w"] = p["ev_conv_w"][0].astype(F32)
    w["conv_b"] = _row(p["ev_conv_b"][0])
    w["dt_bias"] = _pad_lanes(jnp.concatenate([p["ev_dt_bias_f"][0], p["ev_dt_bias_b"][0]])[None, :])
    w["a_log"] = _pad_lanes(jnp.concatenate([p["ev_a_log_f"][0], p["ev_a_log_b"][0]])[None, :])
    w["d_skip"] = jnp.repeat(p["ev_d_skip"][0].astype(F32), D_INNER // SSD_HEADS)[None, :]
    w["gnorm"] = _row(p["ev_gnorm_w"][0])
    w_out = p["ev_w_out"][0]
    w["wo_ssd"] = w_out[:D_INNER].astype(BF16)
    w["wo_fn"] = w_out[D_INNER:].astype(BF16)
    w["ev_ln1_g"], w["ev_ln1_b"] = _row(p["ev_ln1_g"][0]), _row(p["ev_ln1_b"][0])
    w["ffn_wg"] = p["ev_ffn_wg"][0].astype(BF16)
    w["ffn_wu"] = p["ev_ffn_wu"][0].astype(BF16)
    w["ffn_wd"] = p["ev_ffn_wd"][0].astype(BF16)
    w["ev_ln2_g"], w["ev_ln2_b"] = _row(p["ev_ln2_g"][0]), _row(p["ev_ln2_b"][0])
    w["w_qkv"] = p["od_w_qkv"][0].astype(BF16)
    w["na_bias"] = _natten_bias_table(p["od_rpb"][0])
    w["od_w_out"] = p["od_w_out"][0].astype(BF16)
    w["od_ln1_g"], w["od_ln1_b"] = _row(p["od_ln1_g"][0]), _row(p["od_ln1_b"][0])
    w["router"] = _pad_lanes(p["od_router"][0]).astype(BF16)
    w["moe_wg"] = p["od_wg"][0].astype(BF16)
    w["moe_wu"] = p["od_wu"][0].astype(BF16)
    w["moe_wd"] = p["od_wd"][0].astype(BF16)
    w["od_ln2_g"], w["od_ln2_b"] = _row(p["od_ln2_g"][0]), _row(p["od_ln2_b"][0])
    return w


def _pick(n, pref):
    t = pref
    while n % t:
        t //= 2
    return t


def moe_plan(counts, n_rows):
    cnt = counts[0, :N_EXPERTS]
    padded = jnp.ceil(cnt / MOE_ROW_TILE) * MOE_ROW_TILE
    ends = jnp.cumsum(padded)
    offsets = (ends - padded).astype(jnp.int32)
    n_used = (ends[-1:] / MOE_ROW_TILE).astype(jnp.int32)
    starts = jnp.arange(n_rows // MOE_ROW_TILE, dtype=F32) * MOE_ROW_TILE
    tile_expert = jnp.minimum(jnp.sum(starts[:, None] >= ends[None, :], axis=1), N_EXPERTS - 1)
    return offsets, tile_expert.astype(jnp.int32), n_used


def moe_experts_routed(xa, xb, meta, counts, w):
    n = xa.shape[0]
    n_rows = 2 * n + N_EXPERTS * MOE_ROW_TILE
    offsets, tile_expert, n_used = moe_plan(counts, n_rows)
    pos = moe_positions(meta, offsets, _pick(n, 2048))
    xs_a = sc_dispatch_rows(xa, pos, n_rows)
    xs_b = sc_dispatch_rows(xb, pos, n_rows)
    ys_a, ys_b = moe_experts(xs_a, xs_b, tile_expert, n_used, w["moe_wg"], w["moe_wu"], w["moe_wd"], tf=512)
    return ys_a, ys_b, pos


def moe_finish(h3, gate, ys_a, ys_b, pos, w):
    n = h3.shape[0]
    flat = pos.reshape(1, 2 * n)
    ya = sc_gather_rows(ys_a, flat)
    yb = sc_gather_rows(ys_b, flat)
    return moe_combine_res_ln(h3, gate, ya, yb, w["od_ln2_g"], w["od_ln2_b"], _pick(n, 512))


def moe_res_ln(h3, xa, xb, meta, gate, counts, w):
    ys_a, ys_b, pos = moe_experts_routed(xa, xb, meta, counts, w)
    return moe_finish(h3, gate, ys_a, ys_b, pos, w)


def trunk_front(x, w):
    bsz, t, d = x.shape
    n = bsz * t
    tm = _pick(n, 512)
    nc = t // CHUNK
    xf = x.reshape(n, d)

    h0, z, xbc, dt_raw, u = ln_inproj(xf, w["ln_in_g"], w["ln_in_b"], w["wz"], w["wx"], w["wdt"], w["wu"], tm)
    xbc_act = conv_silu(xbc.reshape(bsz, t, CONV_DIM), w["conv_w"], w["conv_b"],
                        tt=_pick(t, 1024), tc=1024)
    pcol, q = ssd_prep(dt_raw.reshape(bsz, t, LANES), w["dt_bias"], w["a_log"], cpp=_pick(nc, 8))
    cps = _pick(nc, 16)
    y_b = ssd_scan(xbc_act, pcol, q, rev=True, cps=cps)
    y_ssd = ssd_scan(xbc_act, pcol, q, rev=False, cps=cps,
                     fused=(y_b, z.reshape(bsz, t, D_INNER), w["d_skip"], w["gnorm"]))
    y_fn = fnet_mix(u.reshape(bsz, t, FNET_WIDTH))
    h1 = proj_res_ln([y_ssd.reshape(n, D_INNER), y_fn.reshape(n, FNET_WIDTH)],
                     [w["wo_ssd"], w["wo_fn"]], h0, w["ev_ln1_g"], w["ev_ln1_b"], tm)
    h2, qh, kh, vh = ffn_res_ln_qkv(h1, w["ffn_wg"], w["ffn_wu"], w["ffn_wd"], w["ev_ln2_g"], w["ev_ln2_b"],
                                    w["w_qkv"], tm)

    att = natten(qh.reshape(bsz, t, d), kh.reshape(bsz, t, d), vh.reshape(bsz, t, d), w["na_bias"])
    h3, xa, xb, meta, gate, counts = proj_res_ln_route(att.reshape(n, d), w["od_w_out"], h2, w["od_ln1_g"],
                                                       w["od_ln1_b"], w["router"], tm)
    ys_a, ys_b, pos = moe_experts_routed(xa, xb, meta, counts, w)
    return (bsz, t, d), h3, gate, ys_a, ys_b, pos


def trunk_back(state, w):
    shape, h3, gate, ys_a, ys_b, pos = state
    return moe_finish(h3, gate, ys_a, ys_b, pos, w).reshape(shape)


def trunk(x, w):
    return trunk_back(trunk_front(x, w), w)


def kernel(x_prompt, x_sample, ln_in_g, ln_in_b, ev_w_in, ev_conv_w, ev_conv_b, ev_dt_bias_f, ev_dt_bias_b,
           ev_a_log_f, ev_a_log_b, ev_d_skip, ev_gnorm_w, ev_w_out, ev_ln1_g, ev_ln1_b, ev_ffn_wg, ev_ffn_wu,
           ev_ffn_wd, ev_ln2_g, ev_ln2_b, od_w_qkv, od_rpb, od_w_out, od_ln1_g, od_ln1_b, od_router, od_wg,
           od_wu, od_wd, od_ln2_g, od_ln2_b):
    params = dict(ln_in_g=ln_in_g, ln_in_b=ln_in_b, ev_w_in=ev_w_in, ev_conv_w=ev_conv_w, ev_conv_b=ev_conv_b,
                  ev_dt_bias_f=ev_dt_bias_f, ev_dt_bias_b=ev_dt_bias_b, ev_a_log_f=ev_a_log_f,
                  ev_a_log_b=ev_a_log_b, ev_d_skip=ev_d_skip, ev_gnorm_w=ev_gnorm_w, ev_w_out=ev_w_out,
                  ev_ln1_g=ev_ln1_g, ev_ln1_b=ev_ln1_b, ev_ffn_wg=ev_ffn_wg, ev_ffn_wu=ev_ffn_wu,
                  ev_ffn_wd=ev_ffn_wd, ev_ln2_g=ev_ln2_g, ev_ln2_b=ev_ln2_b, od_w_qkv=od_w_qkv, od_rpb=od_rpb,
                  od_w_out=od_w_out, od_ln1_g=od_ln1_g, od_ln1_b=od_ln1_b, od_router=od_router, od_wg=od_wg,
                  od_wu=od_wu, od_wd=od_wd, od_ln2_g=od_ln2_g, od_ln2_b=od_ln2_b)
    w = prepare_weights(params)
    front_prompt = trunk_front(x_prompt, w)
    front_sample = trunk_front(x_sample, w)
    return (trunk_back(front_prompt, w), trunk_back(front_sample, w))
```

```python
import functools
import math

import numpy as np
import jax
import jax.numpy as jnp
from jax import lax
from jax.experimental import pallas as pl
from jax.experimental.pallas import tpu as pltpu
from jax.experimental.pallas import tpu_sc as plsc

F32 = jnp.float32
BF16 = jnp.bfloat16

D_MODEL = 1024
GRID_W = 64
D_INNER = 2048
SSD_HEADS = 32
SSD_GROUPS = 4
HEADS_PER_GROUP = SSD_HEADS // SSD_GROUPS
GROUP_DIM = D_INNER // SSD_GROUPS
D_STATE = 128
D_CONV = 5
CHUNK = 128
CONV_DIM = D_INNER + 2 * SSD_GROUPS * D_STATE
FNET_GROUPS = 4
FNET_GROUP_DIM = 256
FNET_WIDTH = 1024
FNET_N2 = 128
FNET_SUB = 8
FNET_STEP = 16
NA_HEADS = 16
NA_HEAD_DIM = 64
NA_WIN_H = 8
NA_WIN_W = 16
D_FF_DENSE = 2816
N_EXPERTS = 8
D_FF_EXPERT = 3584
LN_EPS = 1e-5
RMS_EPS = 1e-5
DEPTH = 2
ALPHA = (2 * DEPTH) ** 0.25
LANES = 128
HALO = 16
NEG_BIG = -1e30
LOG2E = 1.4426950408889634
VMEM_LIMIT = 56 * 1024 * 1024
PACK_W = D_MODEL // 4
SC_WINDOW = 128
MOE_ROW_TILE = 512


def _cparams(sem):
    return pltpu.CompilerParams(dimension_semantics=sem, vmem_limit_bytes=VMEM_LIMIT)


def _const_spec(shape):
    nd = len(shape)
    return pl.BlockSpec(shape, lambda *_: (0,) * nd)


def _dot(a, b):
    return jnp.dot(a, b, preferred_element_type=F32)


def _layer_norm(xf, g, b):
    mu = jnp.mean(xf, axis=-1, keepdims=True)
    xc = xf - mu
    var = jnp.mean(xc * xc, axis=-1, keepdims=True)
    return xc * lax.rsqrt(var + LN_EPS) * g + b


def _silu(x):
    return x * jax.nn.sigmoid(x)


def _ln_inproj_kernel(x_ref, g_ref, b_ref, wz_ref, wx_ref, wdt_ref, wu_ref,
                      h_ref, z_ref, xbc_ref, dt_ref, u_ref):
    h = _layer_norm(x_ref[...], g_ref[...], b_ref[...])
    h_ref[...] = h
    hb = h.astype(BF16)
    for j in range(0, D_INNER, 512):
        z_ref[:, j:j + 512] = _dot(hb, wz_ref[:, j:j + 512]).astype(BF16)
    for j in range(0, CONV_DIM, 512):
        xbc_ref[:, j:j + 512] = _dot(hb, wx_ref[:, j:j + 512]).astype(BF16)
    dt_ref[...] = _dot(hb, wdt_ref[...])
    for j in range(0, FNET_WIDTH, 512):
        u_ref[:, j:j + 512] = _dot(hb, wu_ref[:, j:j + 512])


def ln_inproj(x, g, b, wz, wx, wdt, wu, tm):
    n = x.shape[0]
    row = lambda w: pl.BlockSpec((tm, w), lambda i: (i, 0))
    return pl.pallas_call(
        _ln_inproj_kernel,
        grid=(n // tm,),
        in_specs=[row(D_MODEL), _const_spec((1, D_MODEL)), _const_spec((1, D_MODEL)),
                  _const_spec(wz.shape), _const_spec(wx.shape), _const_spec(wdt.shape),
                  _const_spec(wu.shape)],
        out_specs=[row(D_MODEL), row(D_INNER), row(CONV_DIM), row(LANES), row(FNET_WIDTH)],
        out_shape=[jax.ShapeDtypeStruct((n, D_MODEL), F32),
                   jax.ShapeDtypeStruct((n, D_INNER), BF16),
                   jax.ShapeDtypeStruct((n, CONV_DIM), BF16),
                   jax.ShapeDtypeStruct((n, LANES), F32),
                   jax.ShapeDtypeStruct((n, FNET_WIDTH), F32)],
        compiler_params=_cparams(("parallel",)),
        name="ln_inproj",
    )(x, g, b, wz, wx, wdt, wu)


def _conv_silu_kernel(prev_ref, main_ref, next_ref, shift_ref, w_ref, b_ref, o_ref, scr, *, tt, nt):
    i = pl.program_id(1)
    zero = jnp.zeros((HALO, scr.shape[1]), BF16)
    scr[0:HALO, :] = prev_ref[0]
    scr[HALO:HALO + tt, :] = main_ref[0]
    scr[HALO + tt:2 * HALO + tt, :] = next_ref[0]

    @pl.when(i == 0)
    def _():
        scr[0:HALO, :] = zero

    @pl.when(i == nt - 1)
    def _():
        scr[HALO + tt:2 * HALO + tt, :] = zero

    half = D_CONV // 2
    strip = 2 * LANES
    for r0 in range(0, tt, CHUNK):
        for c0 in range(0, scr.shape[1], strip):
            cols = slice(c0, c0 + strip)
            ext = scr[r0:r0 + CHUNK + 2 * HALO, cols]
            acc = b_ref[:, cols] + w_ref[half:half + 1, cols] * ext[HALO:HALO + CHUNK].astype(F32)
            for j, k in enumerate([k for k in range(D_CONV) if k != half]):
                sh = _dot(shift_ref[j * CHUNK:(j + 1) * CHUNK, :], ext)
                acc = acc + w_ref[k:k + 1, cols] * sh
            o_ref[0, r0:r0 + CHUNK, cols] = _silu(acc).astype(BF16)


def _conv_shift_matrix():
    taps = [k for k in range(D_CONV) if k != D_CONV // 2]
    s = np.zeros((len(taps) * CHUNK, CHUNK + 2 * HALO), np.float32)
    for j, k in enumerate(taps):
        for r in range(CHUNK):
            s[j * CHUNK + r, HALO + r + k - D_CONV // 2] = 1.0
    return s


def conv_silu(xbc, w, b, tt, tc):
    bsz, t, c = xbc.shape
    nt = t // tt
    hb = tt // HALO
    kern = functools.partial(_conv_silu_kernel, tt=tt, nt=nt)
    shift = jnp.asarray(_conv_shift_matrix(), BF16)
    return pl.pallas_call(
        kern,
        grid=(bsz, nt, c // tc),
        in_specs=[
            pl.BlockSpec((1, HALO, tc), lambda bi, i, ci: (bi, jnp.maximum(i * hb - 1, 0), ci)),
            pl.BlockSpec((1, tt, tc), lambda bi, i, ci: (bi, i, ci)),
            pl.BlockSpec((1, HALO, tc), lambda bi, i, ci: (bi, jnp.minimum((i + 1) * hb, t // HALO - 1), ci)),
            _const_spec(shift.shape),
            pl.BlockSpec((D_CONV, tc), lambda bi, i, ci: (0, ci)),
            pl.BlockSpec((1, tc), lambda bi, i, ci: (0, ci)),
        ],
        out_specs=pl.BlockSpec((1, tt, tc), lambda bi, i, ci: (bi, i, ci)),
        out_shape=jax.ShapeDtypeStruct((bsz, t, c), BF16),
        scratch_shapes=[pltpu.VMEM((tt + 2 * HALO, tc), BF16)],
        compiler_params=_cparams(("parallel", "parallel", "parallel")),
        name="conv_silu",
    )(xbc, xbc, xbc, shift, w, b)


def _split3(x):
    hi = x.astype(BF16)
    r1 = x - hi.astype(F32)
    mid = r1.astype(BF16)
    lo = (r1 - mid.astype(F32)).astype(BF16)
    return hi, mid, lo


def _ssd_prep_kernel(dt_ref, bias_ref, alog_ref, tri_ref, trit_ref, sel_ref, pcol_ref, q_ref, *, cpp):
    lane = lax.broadcasted_iota(jnp.int32, (CHUNK, LANES), 1)
    fwd = lane < SSD_HEADS
    a_coef = -jnp.exp(alog_ref[...])
    for c in range(cpp):
        rows = slice(c * CHUNK, (c + 1) * CHUNK)
        raw = dt_ref[0, rows, :] + bias_ref[...]
        dt = jnp.maximum(raw, 0.0) + jnp.log1p(jnp.exp(-jnp.abs(raw)))
        a = dt * a_coef
        cs_f = jnp.dot(tri_ref[...], a, precision=lax.Precision.HIGHEST, preferred_element_type=F32)
        cs_b = jnp.dot(trit_ref[...], a, precision=lax.Precision.HIGHEST, preferred_element_type=F32)
        cs = jnp.where(fwd, cs_f, cs_b)
        end = jnp.where(fwd[0:1], cs[CHUNK - 1:CHUNK, :], cs[0:1, :])
        wend = jnp.exp(end - cs) * dt
        hi, mid, lo = _split3(cs * LOG2E)
        src = jnp.concatenate([hi, mid, lo, wend.astype(BF16)], axis=1)
        pc = _dot(src, sel_ref[...]).astype(BF16)
        pcol_ref[0, 0, rows, :] = pc[:, :SSD_GROUPS * LANES]
        pcol_ref[1, 0, rows, :] = pc[:, SSD_GROUPS * LANES:]
        src_t = ((cs - jnp.log(dt)) * LOG2E).T
        for d in range(2):
            for g in range(SSD_GROUPS):
                r0 = d * SSD_HEADS + g * HEADS_PER_GROUP
                q_ref[d, 0, g, c] = src_t[r0:r0 + HEADS_PER_GROUP, :]


def _prep_select_matrix():
    sel = np.zeros((4 * LANES, 2 * SSD_GROUPS * LANES), np.float32)
    for d in range(2):
        for g in range(SSD_GROUPS):
            for q in range(4):
                for i in range(HEADS_PER_GROUP):
                    src = q * LANES + d * SSD_HEADS + g * HEADS_PER_GROUP + i
                    dst = d * SSD_GROUPS * LANES + g * LANES + q * HEADS_PER_GROUP + i
                    sel[src, dst] = 1.0
    return sel


def ssd_prep(dt_raw, bias_row, alog_row, cpp):
    bsz, t, _ = dt_raw.shape
    nc = t // CHUNK
    tri = np.tril(np.ones((CHUNK, CHUNK), np.float32))
    kern = functools.partial(_ssd_prep_kernel, cpp=cpp)
    return pl.pallas_call(
        kern,
        grid=(bsz, nc // cpp),
        in_specs=[pl.BlockSpec((1, cpp * CHUNK, LANES), lambda b, j: (b, j, 0)),
                  _const_spec((1, LANES)), _const_spec((1, LANES)),
                  _const_spec((CHUNK, CHUNK)), _const_spec((CHUNK, CHUNK)),
                  _const_spec((4 * LANES, 2 * SSD_GROUPS * LANES))],
        out_specs=[pl.BlockSpec((2, 1, cpp * CHUNK, SSD_GROUPS * LANES), lambda b, j: (0, b, j, 0)),
                   pl.BlockSpec((2, 1, SSD_GROUPS, cpp, HEADS_PER_GROUP, CHUNK),
                                lambda b, j: (0, b, 0, j, 0, 0))],
        out_shape=[jax.ShapeDtypeStruct((2, bsz, t, SSD_GROUPS * LANES), BF16),
                   jax.ShapeDtypeStruct((2, bsz, SSD_GROUPS, nc, HEADS_PER_GROUP, CHUNK), F32)],
        compiler_params=_cparams(("parallel", "parallel")),
        name="ssd_prep",
    )(dt_raw, bias_row, alog_row, jnp.asarray(tri), jnp.asarray(tri.T),
      jnp.asarray(_prep_select_matrix(), BF16))


def _expand_matrix():
    e = np.zeros((LANES, HEADS_PER_GROUP * LANES + GROUP_DIM), np.float32)
    for q in range(3):
        for h in range(HEADS_PER_GROUP):
            e[q * HEADS_PER_GROUP + h, h * LANES:(h + 1) * LANES] = 1.0
    for h in range(HEADS_PER_GROUP):
        e[3 * HEADS_PER_GROUP + h,
          HEADS_PER_GROUP * LANES + h * 64:HEADS_PER_GROUP * LANES + (h + 1) * 64] = 1.0
    return e


def _ssd_kernel(*refs, rev, cps, fuse):
    if fuse:
        xs_ref, b_ref, c_ref, p_ref, q_ref, e_ref, yb_ref, z_ref, dsk_ref, gn_ref, o_ref, st_ref = refs
    else:
        xs_ref, b_ref, c_ref, p_ref, q_ref, e_ref, o_ref, st_ref = refs

    @pl.when(pl.program_id(2) == 0)
    def _():
        st_ref[...] = jnp.zeros_like(st_ref)

    row = lax.broadcasted_iota(jnp.int32, (CHUNK, CHUNK), 0)
    col = lax.broadcasted_iota(jnp.int32, (CHUNK, CHUNK), 1)
    mask = (col >= row) if rev else (col <= row)
    lo_half = lax.broadcasted_iota(jnp.int32, (CHUNK, LANES), 1) < 64
    nb = HEADS_PER_GROUP * LANES
    end = 0 if rev else CHUNK - 1
    order = range(cps - 1, -1, -1) if rev else range(cps)
    for c in order:
        rows = slice(c * CHUNK, (c + 1) * CHUNK)
        xf = xs_ref[0, rows, :].astype(F32)
        bm = b_ref[0, rows, :]
        cm = c_ref[0, rows, :]
        ex = _dot(p_ref[0, 0, rows, :], e_ref[...])
        qv = q_ref[0, 0, 0, c]
        cb = lax.dot_general(cm, bm, (((1,), (1,)), ((), ())), preferred_element_type=F32)
        cs_parts = []
        y_parts = []
        for j in range(HEADS_PER_GROUP // 2):
            h1, h2 = 2 * j, 2 * j + 1
            c1 = ex[:, h1 * LANES:(h1 + 1) * LANES]
            c2 = ex[:, h2 * LANES:(h2 + 1) * LANES]
            cs_parts.append(jnp.where(lo_half, c1, c2))
            w1 = jnp.exp2(jnp.where(mask, c1 - qv[h1:h1 + 1, :], -jnp.inf)) * cb
            w2 = jnp.exp2(jnp.where(mask, c2 - qv[h2:h2 + 1, :], -jnp.inf)) * cb
            wp = jnp.concatenate([w1, w2], axis=1).astype(BF16)
            xp = xf[:, j * LANES:(j + 1) * LANES]
            rhs = jnp.concatenate([jnp.where(lo_half, xp, 0.0), jnp.where(lo_half, 0.0, xp)],
                                  axis=0).astype(BF16)
            y_parts.append(_dot(wp, rhs))
        ydiag = jnp.concatenate(y_parts, axis=1)
        expcs = jnp.exp2(jnp.concatenate(cs_parts, axis=1))
        decay = expcs[end:end + 1, :]
        st = st_ref[...]
        yoff = _dot(cm, st.astype(BF16)) * expcs
        xsw = (xf * ex[:, nb:nb + GROUP_DIM]).astype(BF16)
        st_ref[...] = st * decay + lax.dot_general(bm, xsw, (((0,), (0,)), ((), ())),
                                                   preferred_element_type=F32)
        y = ydiag + yoff
        if fuse:
            y = y + yb_ref[0, rows, :].astype(F32) + dsk_ref[...] * xf
            y = y * _silu(z_ref[0, rows, :].astype(F32))
            ms = jnp.mean(y * y, axis=-1, keepdims=True)
            o_ref[0, rows, :] = (y * lax.rsqrt(ms + RMS_EPS) * gn_ref[...]).astype(BF16)
        else:
            o_ref[0, rows, :] = y.astype(BF16)


def ssd_scan(xbc_act, pcol, q, rev, cps, fused=None):
    bsz, t, _ = xbc_act.shape
    ncb = t // (cps * CHUNK)
    r = cps * CHUNK
    d = 1 if rev else 0
    cidx = (lambda j: ncb - 1 - j) if rev else (lambda j: j)
    e = jnp.asarray(_expand_matrix(), BF16)
    b0 = D_INNER // LANES
    c0 = b0 + SSD_GROUPS
    in_specs = [
        pl.BlockSpec((1, r, GROUP_DIM), lambda b, g, j: (b, cidx(j), g)),
        pl.BlockSpec((1, r, LANES), lambda b, g, j: (b, cidx(j), b0 + g)),
        pl.BlockSpec((1, r, LANES), lambda b, g, j: (b, cidx(j), c0 + g)),
        pl.BlockSpec((1, 1, r, LANES), lambda b, g, j: (d, b, cidx(j), g)),
        pl.BlockSpec((1, 1, 1, cps, HEADS_PER_GROUP, CHUNK), lambda b, g, j: (d, b, g, cidx(j), 0, 0)),
        _const_spec(e.shape),
    ]
    args = [xbc_act, xbc_act, xbc_act, pcol, q, e]
    if fused is not None:
        yb, z, dsk, gn = fused
        in_specs += [
            pl.BlockSpec((1, r, GROUP_DIM), lambda b, g, j: (b, cidx(j), g)),
            pl.BlockSpec((1, r, GROUP_DIM), lambda b, g, j: (b, cidx(j), g)),
            pl.BlockSpec((1, GROUP_DIM), lambda b, g, j: (0, g)),
            pl.BlockSpec((1, GROUP_DIM), lambda b, g, j: (0, g)),
        ]
        args += [yb, z, dsk, gn]
    kern = functools.partial(_ssd_kernel, rev=rev, cps=cps, fuse=fused is not None)
    return pl.pallas_call(
        kern,
        grid=(bsz, SSD_GROUPS, ncb),
        in_specs=in_specs,
        out_specs=pl.BlockSpec((1, r, GROUP_DIM), lambda b, g, j: (b, cidx(j), g)),
        out_shape=jax.ShapeDtypeStruct((bsz, t, D_INNER), BF16),
        scratch_shapes=[pltpu.VMEM((D_STATE, GROUP_DIM), F32)],
        compiler_params=_cparams(("parallel", "parallel", "arbitrary")),
        name="ssd_bwd" if rev else "ssd_fwd",
    )(*args)


def _dft_cos_sin(n):
    k = np.arange(n, dtype=np.float64)
    ang = 2.0 * np.pi * np.outer(k, k) / n
    return np.cos(ang), np.sin(ang)


def _fnet_s1_kernel(u_ref, g_ref, cos_ref, sin_ref, o_ref, *, n1):
    c = FNET_WIDTH
    rows = n1 * FNET_SUB
    re_parts, im_parts = [], []
    for h in range(FNET_STEP // FNET_SUB):
        x = u_ref[0, :, h * FNET_SUB:(h + 1) * FNET_SUB, :].reshape(rows, c).astype(BF16)
        a = _dot(g_ref[...], x)
        ar = a[:rows]
        ai = a[rows:]
        ct = jnp.concatenate([cos_ref[h]] * (c // LANES), axis=1)
        st = jnp.concatenate([sin_ref[h]] * (c // LANES), axis=1)
        re_parts.append((ar * ct + ai * st).reshape(n1, FNET_SUB, c))
        im_parts.append((ai * ct - ar * st).reshape(n1, FNET_SUB, c))
    o_ref[0, 0] = jnp.concatenate(re_parts, axis=1).astype(BF16)
    o_ref[0, 1] = jnp.concatenate(im_parts, axis=1).astype(BF16)


def _fnet_s2_kernel(br_ref, bi_ref, f2_ref, wc_ref, o_ref, *, ks):
    n2 = FNET_N2
    c = FNET_WIDTH
    gs = []
    for kk in range(ks):
        rhs = jnp.concatenate([br_ref[0, 0, kk], bi_ref[0, 0, kk]], axis=0)
        gs.append(_dot(f2_ref[...], rhs).astype(BF16))
    for grp in range(FNET_GROUPS):
        cols = slice(grp * FNET_GROUP_DIM, (grp + 1) * FNET_GROUP_DIM)
        lhs = jnp.concatenate([jnp.concatenate([g[:n2, cols], g[n2:, cols]], axis=1) for g in gs], axis=0)
        res = _dot(lhs, wc_ref[...]).astype(BF16)
        for kk in range(ks):
            o_ref[0, :, kk * c + grp * FNET_GROUP_DIM:kk * c + (grp + 1) * FNET_GROUP_DIM] = (
                res[kk * n2:(kk + 1) * n2])


def fnet_mix(u):
    bsz, t, c = u.shape
    n2 = FNET_N2
    n1 = t // n2
    sub, step = FNET_SUB, FNET_STEP
    c1, s1 = _dft_cos_sin(n1)
    c2, s2 = _dft_cos_sin(n2)
    cc, sc = _dft_cos_sin(FNET_GROUP_DIM)
    g1 = jnp.asarray(np.kron(np.concatenate([c1, -s1], axis=0), np.eye(sub)), BF16)
    f2 = jnp.asarray(np.block([[c2, s2], [-s2, c2]]), BF16)
    scale = 1.0 / math.sqrt(t * FNET_GROUP_DIM)
    wc = jnp.asarray(np.concatenate([cc, sc], axis=0) * scale, BF16)
    k1 = np.arange(n1)[None, :, None]
    pos = (np.arange(n2 // sub)[:, None, None] * sub + np.arange(sub)[None, None, :])
    ang = (2.0 * np.pi * k1 * pos / t).reshape(n2 // sub, n1 * sub)
    tw_cos = jnp.asarray(np.repeat(np.cos(ang)[:, :, None], LANES, axis=2), F32)
    tw_sin = jnp.asarray(np.repeat(np.sin(ang)[:, :, None], LANES, axis=2), F32)

    hs = step // sub
    bv = pl.pallas_call(
        functools.partial(_fnet_s1_kernel, n1=n1),
        grid=(bsz, n2 // step),
        in_specs=[pl.BlockSpec((1, n1, step, c), lambda b, j: (b, 0, j, 0)),
                  _const_spec(g1.shape),
                  pl.BlockSpec((hs, n1 * sub, LANES), lambda b, j: (j, 0, 0)),
                  pl.BlockSpec((hs, n1 * sub, LANES), lambda b, j: (j, 0, 0))],
        out_specs=pl.BlockSpec((1, 2, n1, step, c), lambda b, j: (b, 0, 0, j, 0)),
        out_shape=jax.ShapeDtypeStruct((bsz, 2, n1, n2, c), BF16),
        compiler_params=_cparams(("parallel", "parallel")),
        name="fnet_stage1",
    )(u.reshape(bsz, n1, n2, c), g1, tw_cos, tw_sin)

    ks = _pick(n1, 4)
    y = pl.pallas_call(
        functools.partial(_fnet_s2_kernel, ks=ks),
        grid=(bsz, n1 // ks),
        in_specs=[pl.BlockSpec((1, 1, ks, n2, c), lambda b, k: (b, 0, k, 0, 0)),
                  pl.BlockSpec((1, 1, ks, n2, c), lambda b, k: (b, 1, k, 0, 0)),
                  _const_spec((2 * n2, 2 * n2)),
                  _const_spec((2 * FNET_GROUP_DIM, FNET_GROUP_DIM))],
        out_specs=pl.BlockSpec((1, n2, ks * c), lambda b, k: (b, 0, k)),
        out_shape=jax.ShapeDtypeStruct((bsz, n2, n1 * c), BF16),
        compiler_params=_cparams(("parallel", "parallel")),
        name="fnet_stage2",
    )(bv, bv, f2, wc)
    return y.reshape(bsz, t, c)


def _pack_bf16_pair(a, b):
    lo = lax.bitcast_convert_type(a.astype(BF16).astype(F32), jnp.uint32)
    hi = lax.bitcast_convert_type(b.astype(BF16).astype(F32), jnp.uint32)
    word = lax.shift_right_logical(lo, jnp.uint32(16)) | hi
    return lax.bitcast_convert_type(word, jnp.int32)


def _unpack_bf16_pair(word):
    w = lax.bitcast_convert_type(word, jnp.uint32)
    a = lax.bitcast_convert_type(lax.shift_left(w, jnp.uint32(16)), F32)
    b = lax.bitcast_convert_type(w & jnp.uint32(0xFFFF0000), F32)
    return a, b


def _pack_row(x, a_ref, b_ref):
    a_ref[...] = _pack_bf16_pair(x[:, 0:PACK_W], x[:, PACK_W:2 * PACK_W])
    b_ref[...] = _pack_bf16_pair(x[:, 2 * PACK_W:3 * PACK_W], x[:, 3 * PACK_W:4 * PACK_W])


def _unpack_row(a_word, b_word):
    x0, x1 = _unpack_bf16_pair(a_word)
    x2, x3 = _unpack_bf16_pair(b_word)
    return jnp.concatenate([x0, x1, x2, x3], axis=1)


def _proj_res_ln_kernel(*refs, n_in):
    xs = refs[:n_in]
    ws = refs[n_in:2 * n_in]
    h_ref, g_ref, b_ref, o_ref = refs[2 * n_in:]
    acc = _dot(xs[0][...], ws[0][...])
    for x_ref, w_ref in zip(xs[1:], ws[1:]):
        acc = acc + _dot(x_ref[...], w_ref[...])
    o_ref[...] = _layer_norm(ALPHA * h_ref[...] + acc, g_ref[...], b_ref[...])


def proj_res_ln(xs, ws, h, g, b, tm):
    n = h.shape[0]
    n_in = len(xs)
    in_specs = [pl.BlockSpec((tm, x.shape[1]), lambda i: (i, 0)) for x in xs]
    in_specs += [_const_spec(w.shape) for w in ws]
    in_specs += [pl.BlockSpec((tm, D_MODEL), lambda i: (i, 0)),
                 _const_spec((1, D_MODEL)), _const_spec((1, D_MODEL))]
    return pl.pallas_call(
        functools.partial(_proj_res_ln_kernel, n_in=n_in),
        grid=(n // tm,),
        in_specs=in_specs,
        out_specs=pl.BlockSpec((tm, D_MODEL), lambda i: (i, 0)),
        out_shape=jax.ShapeDtypeStruct((n, D_MODEL), F32),
        compiler_params=_cparams(("parallel",)),
        name="proj_res_ln",
    )(*xs, *ws, h, g, b)


def _resident_spec(shape):
    nd = len(shape)
    return pl.BlockSpec(shape, lambda *_: (0,) * nd, pipeline_mode=pl.Buffered(1))


def _ffn_qkv_kernel(h_ref, wg_ref, wu_ref, wd_ref, g_ref, b_ref, wqkv_ref, o_ref, q_ref, k_ref, v_ref,
                    act_ref, *, tf):
    h = h_ref[...]
    hb = h.astype(BF16)
    for j in range(0, D_FF_DENSE, tf):
        gate = _dot(hb, wg_ref[:, j:j + tf])
        up = _dot(hb, wu_ref[:, j:j + tf])
        act_ref[:, j:j + tf] = (_silu(gate) * up).astype(BF16)
    y = _dot(act_ref[...], wd_ref[...])
    out = _layer_norm(ALPHA * h + y, g_ref[...], b_ref[...])
    o_ref[...] = out
    ob = out.astype(BF16)
    d = D_MODEL
    scale = NA_HEAD_DIM ** -0.5
    for j in range(0, d, 512):
        q_ref[:, j:j + 512] = (_dot(ob, wqkv_ref[:, j:j + 512]) * scale).astype(BF16)
        k_ref[:, j:j + 512] = _dot(ob, wqkv_ref[:, d + j:d + j + 512]).astype(BF16)
        v_ref[:, j:j + 512] = _dot(ob, wqkv_ref[:, 2 * d + j:2 * d + j + 512]).astype(BF16)


def ffn_res_ln_qkv(h, wg, wu, wd, g, b, wqkv, tm, tf=256):
    n = h.shape[0]
    row = pl.BlockSpec((tm, D_MODEL), lambda i: (i, 0))
    sds = jax.ShapeDtypeStruct((n, D_MODEL), BF16)
    return pl.pallas_call(
        functools.partial(_ffn_qkv_kernel, tf=tf),
        grid=(n // tm,),
        in_specs=[row, _resident_spec(wg.shape), _resident_spec(wu.shape), _resident_spec(wd.shape),
                  _const_spec((1, D_MODEL)), _const_spec((1, D_MODEL)), _resident_spec(wqkv.shape)],
        out_specs=[row, row, row, row],
        out_shape=[jax.ShapeDtypeStruct((n, D_MODEL), F32), sds, sds, sds],
        scratch_shapes=[pltpu.VMEM((tm, D_FF_DENSE), BF16)],
        compiler_params=_cparams(("parallel",)),
        name="ffn_res_ln_qkv",
    )(h, wg, wu, wd, g, b, wqkv)


def _natten_kernel(q_ref, k_ref, v_ref, bias_ref, o_ref, *, rows, hw, rq):
    nk = NA_WIN_H * GRID_W
    npair = hw // LANES
    lo_half = lax.broadcasted_iota(jnp.int32, (GRID_W, LANES), 1) < NA_HEAD_DIM
    starts = []
    scores = []
    for r in range(rq):
        i = pl.program_id(2) * rq + r
        r0 = jnp.clip(i - NA_WIN_H // 2, 0, rows - NA_WIN_H)
        case = i - r0
        starts.append(pl.multiple_of(r0 * GRID_W, GRID_W))
        for p in range(npair):
            cols = slice(p * LANES, (p + 1) * LANES)
            qp = q_ref[0, r * GRID_W:(r + 1) * GRID_W, cols].astype(F32)
            qbd = jnp.concatenate([jnp.where(lo_half, qp, 0.0), jnp.where(lo_half, 0.0, qp)],
                                  axis=0).astype(BF16)
            s = lax.dot_general(qbd, k_ref[0, pl.ds(starts[r], nk), cols], (((1,), (1,)), ((), ())),
                                preferred_element_type=F32)
            scores.append(s + bias_ref[case, p])
    s_all = jnp.concatenate(scores, axis=0)
    e_f32 = jnp.exp(s_all - jnp.max(s_all, axis=-1, keepdims=True))
    inv_all = 1.0 / jnp.sum(e_f32, axis=-1, keepdims=True)
    e_all = e_f32.astype(BF16)
    for r in range(rq):
        for p in range(npair):
            cols = slice(p * LANES, (p + 1) * LANES)
            c0 = (r * npair + p) * 2 * GRID_W
            e = e_all[c0:c0 + 2 * GRID_W]
            o2 = _dot(e, v_ref[0, pl.ds(starts[r], nk), cols]) * inv_all[c0:c0 + 2 * GRID_W]
            o_ref[0, r * GRID_W:(r + 1) * GRID_W, cols] = jnp.where(
                lo_half, o2[:GRID_W], o2[GRID_W:]).astype(BF16)


def _natten_bias_table(rpb):
    j = np.arange(GRID_W)[:, None]
    kc = np.arange(GRID_W)[None, :]
    cstart = np.clip(j - NA_WIN_W // 2, 0, GRID_W - NA_WIN_W)
    valid = (kc >= cstart) & (kc < cstart + NA_WIN_W)
    coff = np.clip(kc - j + NA_WIN_W - 1, 0, 2 * NA_WIN_W - 2)
    dd = np.arange(NA_WIN_H)[:, None]
    a = np.arange(NA_WIN_H)[None, :]
    roff = a + (NA_WIN_H - 1) - dd
    rows = rpb.astype(F32)[:, roff, :]
    onehot = (coff[None, :, :] == np.arange(2 * NA_WIN_W - 1)[:, None, None]) & valid[None]
    t = jnp.einsum("hdac,cjk->dhjak", rows, jnp.asarray(onehot, F32),
                   precision=lax.Precision.HIGHEST)
    t = jnp.where(valid[None, None, :, None, :], t, NEG_BIG)
    return t.reshape(NA_WIN_H, NA_HEADS // 2, 2 * GRID_W, NA_WIN_H * GRID_W)


def natten(q, k, v, bias, hw=512, rq=8):
    bsz, t, d = q.shape
    rows = t // GRID_W
    nh = d // hw
    pp = hw // LANES
    resident = dict(pipeline_mode=pl.Buffered(1))
    return pl.pallas_call(
        functools.partial(_natten_kernel, rows=rows, hw=hw, rq=rq),
        grid=(nh, bsz, rows // rq),
        in_specs=[pl.BlockSpec((1, rq * GRID_W, hw), lambda hh, b, i: (b, i, hh)),
                  pl.BlockSpec((1, t, hw), lambda hh, b, i: (b, 0, hh), **resident),
                  pl.BlockSpec((1, t, hw), lambda hh, b, i: (b, 0, hh), **resident),
                  pl.BlockSpec((NA_WIN_H, pp, 2 * GRID_W, NA_WIN_H * GRID_W),
                               lambda hh, b, i: (0, hh, 0, 0), **resident)],
        out_specs=pl.BlockSpec((1, rq * GRID_W, hw), lambda hh, b, i: (b, i, hh)),
        out_shape=jax.ShapeDtypeStruct((bsz, t, d), BF16),
        compiler_params=_cparams(("parallel", "parallel", "arbitrary")),
        name="natten",
    )(q, k, v, bias)


def _lane_pick(x, lane, idx):
    return jnp.sum(jnp.where(lane == idx, x, 0.0), axis=-1, keepdims=True)


def _proj_route_kernel(x_ref, wo_ref, hres_ref, g_ref, b_ref, w_ref, ltri_ref,
                       h_ref, xa_ref, xb_ref, meta_ref, gate_ref, cnt_ref, run_ref):
    @pl.when(pl.program_id(0) == 0)
    def _():
        run_ref[...] = jnp.zeros_like(run_ref)

    h = _layer_norm(ALPHA * hres_ref[...] + _dot(x_ref[...], wo_ref[...]), g_ref[...], b_ref[...])
    h_ref[...] = h
    _pack_row(h, xa_ref, xb_ref)
    logits = _dot(h.astype(BF16), w_ref[...])
    lane = lax.broadcasted_iota(jnp.int32, logits.shape, 1)
    logits = jnp.where(lane < N_EXPERTS, logits, -jnp.inf)
    m1 = jnp.max(logits, axis=-1, keepdims=True)
    i1 = jnp.min(jnp.where(logits == m1, lane, LANES), axis=-1, keepdims=True)
    rest = jnp.where(lane == i1, -jnp.inf, logits)
    m2 = jnp.max(rest, axis=-1, keepdims=True)
    i2 = jnp.min(jnp.where(rest == m2, lane, LANES), axis=-1, keepdims=True)
    e2 = jnp.exp(m2 - m1)
    inv = 1.0 / (1.0 + e2)
    gate_ref[...] = jnp.where(lane == 0, inv, jnp.where(lane == 1, e2 * inv, 0.0))

    sel = jnp.where(lane == i1, 1.0, jnp.where(lane == i2, 1.0, 0.0))
    before = _dot(ltri_ref[...], sel.astype(BF16)) + run_ref[...]
    rank1 = _lane_pick(before, lane, i1)
    rank2 = _lane_pick(before, lane, i2)
    run_ref[...] += jnp.sum(sel, axis=0, keepdims=True)
    cnt_ref[...] = jnp.broadcast_to(run_ref[...], cnt_ref.shape)
    packed = jnp.where(lane == 0, rank1, jnp.where(lane == 1, rank2, jnp.where(
        lane == 2, i1.astype(F32), jnp.where(lane == 3, i2.astype(F32), 0.0))))
    meta_ref[...] = packed.T[0:8, :]


def proj_res_ln_route(x, wo, hres, g, b, w_router, tm):
    n = hres.shape[0]
    ltri = jnp.asarray(np.tril(np.ones((tm, tm), np.float32), -1), BF16)
    row = lambda w: pl.BlockSpec((tm, w), lambda i: (i, 0))
    return pl.pallas_call(
        _proj_route_kernel,
        grid=(n // tm,),
        in_specs=[row(x.shape[1]), _const_spec(wo.shape), row(D_MODEL),
                  _const_spec((1, D_MODEL)), _const_spec((1, D_MODEL)),
                  _const_spec(w_router.shape), _const_spec((tm, tm))],
        out_specs=[row(D_MODEL), row(PACK_W), row(PACK_W),
                   pl.BlockSpec((8, tm), lambda i: (0, i)), row(LANES), _const_spec((8, LANES))],
        out_shape=[jax.ShapeDtypeStruct((n, D_MODEL), F32),
                   jax.ShapeDtypeStruct((n, PACK_W), jnp.int32),
                   jax.ShapeDtypeStruct((n, PACK_W), jnp.int32),
                   jax.ShapeDtypeStruct((8, n), F32),
                   jax.ShapeDtypeStruct((n, LANES), F32),
                   jax.ShapeDtypeStruct((8, LANES), F32)],
        scratch_shapes=[pltpu.VMEM((1, LANES), F32)],
        compiler_params=_cparams(("arbitrary",)),
        name="proj_res_ln_route",
    )(x, wo, hres, g, b, w_router, ltri)


def _moe_pos_kernel(off_ref, meta_ref, pos_ref):
    m = meta_ref[...]
    for k in range(2):
        rank = m[k:k + 1, :].astype(jnp.int32)
        expert = m[2 + k:3 + k, :].astype(jnp.int32)
        base = jnp.zeros_like(rank)
        for e in range(N_EXPERTS):
            base = jnp.where(expert == e, off_ref[e], base)
        pos_ref[k:k + 1, :] = rank + base


def moe_positions(meta, offsets, tn):
    n = meta.shape[1]
    return pl.pallas_call(
        _moe_pos_kernel,
        grid_spec=pltpu.PrefetchScalarGridSpec(
            num_scalar_prefetch=1,
            grid=(n // tn,),
            in_specs=[pl.BlockSpec((8, tn), lambda i, off: (0, i))],
            out_specs=pl.BlockSpec((2, tn), lambda i, off: (0, i)),
        ),
        out_shape=jax.ShapeDtypeStruct((2, n), jnp.int32),
        compiler_params=_cparams(("parallel",)),
        name="moe_positions",
    )(offsets, meta)


def _sc_mesh():
    return plsc.VectorSubcoreMesh(core_axis_name="core", subcore_axis_name="subcore")


def sc_dispatch_rows(x, pos, out_rows):
    n, dim = x.shape

    @pl.kernel(out_type=jax.ShapeDtypeStruct((out_rows, dim), x.dtype), mesh=_sc_mesh(), scratch_types=[])
    def k(x_hbm, i0_hbm, i1_hbm, o_hbm):
        def body(x_v, i0_v, i1_v):
            pltpu.sync_copy(x_v, o_hbm.at[i0_v.at[0]])
            pltpu.sync_copy(x_v, o_hbm.at[i1_v.at[0]])

        pltpu.emit_pipeline(
            body, grid=(n // SC_WINDOW,),
            in_specs=[pl.BlockSpec((SC_WINDOW, dim), index_map=lambda i: (i, 0)),
                      pl.BlockSpec((1, SC_WINDOW), index_map=lambda i: (0, i)),
                      pl.BlockSpec((1, SC_WINDOW), index_map=lambda i: (0, i))],
            out_specs=[],
            core_axis_name=("core", "subcore"), dimension_semantics=(pltpu.PARALLEL,),
        )(x_hbm, i0_hbm, i1_hbm)

    return k(x, pos[0:1], pos[1:2])


def sc_gather_rows(x, idx):
    ni = idx.shape[1]
    dim = x.shape[1]

    @pl.kernel(out_type=jax.ShapeDtypeStruct((ni, dim), x.dtype), mesh=_sc_mesh(), scratch_types=[])
    def k(x_hbm, i_hbm, o_hbm):
        def body(i_v, o_v):
            pltpu.sync_copy(x_hbm.at[i_v.at[0]], o_v)

        pltpu.emit_pipeline(
            body, grid=(ni // SC_WINDOW,),
            in_specs=[pl.BlockSpec((1, SC_WINDOW), index_map=lambda i: (0, i))],
            out_specs=[pl.BlockSpec((SC_WINDOW, dim), index_map=lambda i: (i, 0))],
            core_axis_name=("core", "subcore"), dimension_semantics=(pltpu.PARALLEL,),
        )(i_hbm, o_hbm)

    return k(x, idx)


def _moe_expert_kernel(te_ref, nu_ref, xa_ref, xb_ref, wg_ref, wu_ref, wd_ref, ya_ref, yb_ref, act_ref, *, tf):
    @pl.when(pl.program_id(0) < nu_ref[0])
    def _():
        xb = _unpack_row(xa_ref[...], xb_ref[...]).astype(BF16)
        for j in range(0, D_FF_EXPERT, tf):
            gate = _dot(xb, wg_ref[0, :, j:j + tf])
            up = _dot(xb, wu_ref[0, :, j:j + tf])
            act_ref[:, j:j + tf] = (_silu(gate) * up).astype(BF16)
        _pack_row(_dot(act_ref[...], wd_ref[0]), ya_ref, yb_ref)


def moe_experts(xa, xb, tile_expert, n_used, wg, wu, wd, tf):
    rows = xa.shape[0]
    nt = rows // MOE_ROW_TILE
    xspec = pl.BlockSpec((MOE_ROW_TILE, PACK_W), lambda i, te, nu: (i, 0))
    sds = jax.ShapeDtypeStruct((rows, PACK_W), jnp.int32)
    return pl.pallas_call(
        functools.partial(_moe_expert_kernel, tf=tf),
        grid_spec=pltpu.PrefetchScalarGridSpec(
            num_scalar_prefetch=2,
            grid=(nt,),
            in_specs=[xspec, xspec,
                      pl.BlockSpec((1, D_MODEL, D_FF_EXPERT), lambda i, te, nu: (te[i], 0, 0)),
                      pl.BlockSpec((1, D_MODEL, D_FF_EXPERT), lambda i, te, nu: (te[i], 0, 0)),
                      pl.BlockSpec((1, D_FF_EXPERT, D_MODEL), lambda i, te, nu: (te[i], 0, 0))],
            out_specs=[xspec, xspec],
            scratch_shapes=[pltpu.VMEM((MOE_ROW_TILE, D_FF_EXPERT), BF16)],
        ),
        out_shape=[sds, sds],
        compiler_params=_cparams(("arbitrary",)),
        name="moe_experts",
    )(tile_expert, n_used, xa, xb, wg, wu, wd)


def _moe_combine_kernel(h_ref, gate_ref, ya0_ref, yb0_ref, ya1_ref, yb1_ref, g_ref, b_ref, o_ref):
    gate = gate_ref[...]
    lane = lax.broadcasted_iota(jnp.int32, gate.shape, 1)
    g0 = _lane_pick(gate, lane, 0)
    g1 = _lane_pick(gate, lane, 1)
    y = g0 * _unpack_row(ya0_ref[...], yb0_ref[...]) + g1 * _unpack_row(ya1_ref[...], yb1_ref[...])
    o_ref[...] = _layer_norm(ALPHA * h_ref[...] + y, g_ref[...], b_ref[...])


def moe_combine_res_ln(h, gate, ya, yb, g, b, tm):
    n = h.shape[0]
    nb = n // tm
    first = pl.BlockSpec((tm, PACK_W), lambda i: (i, 0))
    second = pl.BlockSpec((tm, PACK_W), lambda i: (nb + i, 0))
    return pl.pallas_call(
        _moe_combine_kernel,
        grid=(nb,),
        in_specs=[pl.BlockSpec((tm, D_MODEL), lambda i: (i, 0)),
                  pl.BlockSpec((tm, LANES), lambda i: (i, 0)),
                  first, first, second, second,
                  _const_spec((1, D_MODEL)), _const_spec((1, D_MODEL))],
        out_specs=pl.BlockSpec((tm, D_MODEL), lambda i: (i, 0)),
        out_shape=jax.ShapeDtypeStruct((n, D_MODEL), F32),
        compiler_params=_cparams(("parallel",)),
        name="moe_combine_res_ln",
    )(h, gate, ya, yb, ya, yb, g, b)


def _row(v):
    return v.reshape(1, -1).astype(F32)


def _pad_lanes(v, fill=0.0):
    v = v.astype(F32)
    pad = LANES - v.shape[-1]
    return jnp.concatenate([v, jnp.full(v.shape[:-1] + (pad,), fill, F32)], axis=-1)


def prepare_weights(p):
    w = {}
    w_in = p["ev_w_in"][0]
    o1 = D_INNER
    o2 = o1 + CONV_DIM
    o3 = o2 + 2 * SSD_HEADS
    w["wz"] = w_in[:, :o1].astype(BF16)
    w["wx"] = w_in[:, o1:o2].astype(BF16)
    w["wdt"] = _pad_lanes(w_in[:, o2:o3]).astype(BF16)
    w["wu"] = w_in[:, o3:].astype(BF16)
    w["ln_in_g"], w["ln_in_b"] = _row(p["ln_in_g"]), _row(p["ln_in_b"])
    w["conv_w"] = p["ev_conv_w"][0].astype(F32)
    w["conv_b"] = _row(p["ev_conv_b"][0])
    w["dt_bias"] = _pad_lanes(jnp.concatenate([p["ev_dt_bias_f"][0], p["ev_dt_bias_b"][0]])[None, :])
    w["a_log"] = _pad_lanes(jnp.concatenate([p["ev_a_log_f"][0], p["ev_a_log_b"][0]])[None, :])
    w["d_skip"] = jnp.repeat(p["ev_d_skip"][0].astype(F32), D_INNER // SSD_HEADS)[None, :]
    w["gnorm"] = _row(p["ev_gnorm_w"][0])
    w_out = p["ev_w_out"][0]
    w["wo_ssd"] = w_out[:D_INNER].astype(BF16)
    w["wo_fn"] = w_out[D_INNER:].astype(BF16)
    w["ev_ln1_g"], w["ev_ln1_b"] = _row(p["ev_ln1_g"][0]), _row(p["ev_ln1_b"][0])
    w["ffn_wg"] = p["ev_ffn_wg"][0].astype(BF16)
    w["ffn_wu"] = p["ev_ffn_wu"][0].astype(BF16)
    w["ffn_wd"] = p["ev_ffn_wd"][0].astype(BF16)
    w["ev_ln2_g"], w["ev_ln2_b"] = _row(p["ev_ln2_g"][0]), _row(p["ev_ln2_b"][0])
    w["w_qkv"] = p["od_w_qkv"][0].astype(BF16)
    w["na_bias"] = _natten_bias_table(p["od_rpb"][0])
    w["od_w_out"] = p["od_w_out"][0].astype(BF16)
    w["od_ln1_g"], w["od_ln1_b"] = _row(p["od_ln1_g"][0]), _row(p["od_ln1_b"][0])
    w["router"] = _pad_lanes(p["od_router"][0]).astype(BF16)
    w["moe_wg"] = p["od_wg"][0].astype(BF16)
    w["moe_wu"] = p["od_wu"][0].astype(BF16)
    w["moe_wd"] = p["od_wd"][0].astype(BF16)
    w["od_ln2_g"], w["od_ln2_b"] = _row(p["od_ln2_g"][0]), _row(p["od_ln2_b"][0])
    return w


def _pick(n, pref):
    t = pref
    while n % t:
        t //= 2
    return t


def moe_plan(counts, n_rows):
    cnt = counts[0, :N_EXPERTS]
    padded = jnp.ceil(cnt / MOE_ROW_TILE) * MOE_ROW_TILE
    ends = jnp.cumsum(padded)
    offsets = (ends - padded).astype(jnp.int32)
    n_used = (ends[-1:] / MOE_ROW_TILE).astype(jnp.int32)
    starts = jnp.arange(n_rows // MOE_ROW_TILE, dtype=F32) * MOE_ROW_TILE
    tile_expert = jnp.minimum(jnp.sum(starts[:, None] >= ends[None, :], axis=1), N_EXPERTS - 1)
    return offsets, tile_expert.astype(jnp.int32), n_used


def moe_experts_routed(xa, xb, meta, counts, w):
    n = xa.shape[0]
    n_rows = 2 * n + N_EXPERTS * MOE_ROW_TILE
    offsets, tile_expert, n_used = moe_plan(counts, n_rows)
    pos = moe_positions(meta, offsets, _pick(n, 2048))
    xs_a = sc_dispatch_rows(xa, pos, n_rows)
    xs_b = sc_dispatch_rows(xb, pos, n_rows)
    ys_a, ys_b = moe_experts(xs_a, xs_b, tile_expert, n_used, w["moe_wg"], w["moe_wu"], w["moe_wd"], tf=512)
    return ys_a, ys_b, pos


def moe_gather_back(ys_a, ys_b, pos):
    flat = pos.reshape(1, pos.shape[0] * pos.shape[1])
    return sc_gather_rows(ys_a, flat), sc_gather_rows(ys_b, flat)


def moe_res_ln(h3, xa, xb, meta, gate, counts, w):
    ys_a, ys_b, pos = moe_experts_routed(xa, xb, meta, counts, w)
    ya, yb = moe_gather_back(ys_a, ys_b, pos)
    return moe_combine_res_ln(h3, gate, ya, yb, w["od_ln2_g"], w["od_ln2_b"], _pick(h3.shape[0], 512))


def trunk_front(x, w):
    bsz, t, d = x.shape
    n = bsz * t
    tm = _pick(n, 512)
    nc = t // CHUNK
    xf = x.reshape(n, d)

    h0, z, xbc, dt_raw, u = ln_inproj(xf, w["ln_in_g"], w["ln_in_b"], w["wz"], w["wx"], w["wdt"], w["wu"], tm)
    xbc_act = conv_silu(xbc.reshape(bsz, t, CONV_DIM), w["conv_w"], w["conv_b"],
                        tt=_pick(t, 1024), tc=1024)
    pcol, q = ssd_prep(dt_raw.reshape(bsz, t, LANES), w["dt_bias"], w["a_log"], cpp=_pick(nc, 8))
    cps = _pick(nc, 16)
    y_b = ssd_scan(xbc_act, pcol, q, rev=True, cps=cps)
    y_ssd = ssd_scan(xbc_act, pcol, q, rev=False, cps=cps,
                     fused=(y_b, z.reshape(bsz, t, D_INNER), w["d_skip"], w["gnorm"]))
    y_fn = fnet_mix(u.reshape(bsz, t, FNET_WIDTH))
    h1 = proj_res_ln([y_ssd.reshape(n, D_INNER), y_fn.reshape(n, FNET_WIDTH)],
                     [w["wo_ssd"], w["wo_fn"]], h0, w["ev_ln1_g"], w["ev_ln1_b"], tm)
    h2, qh, kh, vh = ffn_res_ln_qkv(h1, w["ffn_wg"], w["ffn_wu"], w["ffn_wd"], w["ev_ln2_g"], w["ev_ln2_b"],
                                    w["w_qkv"], tm)

    att = natten(qh.reshape(bsz, t, d), kh.reshape(bsz, t, d), vh.reshape(bsz, t, d), w["na_bias"])
    h3, xa, xb, meta, gate, counts = proj_res_ln_route(att.reshape(n, d), w["od_w_out"], h2, w["od_ln1_g"],
                                                       w["od_ln1_b"], w["router"], tm)
    ys_a, ys_b, pos = moe_experts_routed(xa, xb, meta, counts, w)
    ya, yb = moe_gather_back(ys_a, ys_b, pos)
    return (bsz, t, d), h3, gate, ya, yb


def trunk_back(state, w):
    shape, h3, gate, ya, yb = state
    out = moe_combine_res_ln(h3, gate, ya, yb, w["od_ln2_g"], w["od_ln2_b"], _pick(h3.shape[0], 512))
    return out.reshape(shape)


def trunk(x, w):
    return trunk_back(trunk_front(x, w), w)


def kernel(x_prompt, x_sample, ln_in_g, ln_in_b, ev_w_in, ev_conv_w, ev_conv_b, ev_dt_bias_f, ev_dt_bias_b,
           ev_a_log_f, ev_a_log_b, ev_d_skip, ev_gnorm_w, ev_w_out, ev_ln1_g, ev_ln1_b, ev_ffn_wg, ev_ffn_wu,
           ev_ffn_wd, ev_ln2_g, ev_ln2_b, od_w_qkv, od_rpb, od_w_out, od_ln1_g, od_ln1_b, od_router, od_wg,
           od_wu, od_wd, od_ln2_g, od_ln2_b):
    params = dict(ln_in_g=ln_in_g, ln_in_b=ln_in_b, ev_w_in=ev_w_in, ev_conv_w=ev_conv_w, ev_conv_b=ev_conv_b,
                  ev_dt_bias_f=ev_dt_bias_f, ev_dt_bias_b=ev_dt_bias_b, ev_a_log_f=ev_a_log_f,
                  ev_a_log_b=ev_a_log_b, ev_d_skip=ev_d_skip, ev_gnorm_w=ev_gnorm_w, ev_w_out=ev_w_out,
                  ev_ln1_g=ev_ln1_g, ev_ln1_b=ev_ln1_b, ev_ffn_wg=ev_ffn_wg, ev_ffn_wu=ev_ffn_wu,
                  ev_ffn_wd=ev_ffn_wd, ev_ln2_g=ev_ln2_g, ev_ln2_b=ev_ln2_b, od_w_qkv=od_w_qkv, od_rpb=od_rpb,
                  od_w_out=od_w_out, od_ln1_g=od_ln1_g, od_ln1_b=od_ln1_b, od_router=od_router, od_wg=od_wg,
                  od_wu=od_wu, od_wd=od_wd, od_ln2_g=od_ln2_g, od_ln2_b=od_ln2_b)
    w = prepare_weights(params)
    front_prompt = trunk_front(x_prompt, w)
    front_sample = trunk_front(x_sample, w)
    return (trunk_back(front_prompt, w), trunk_back(front_sample, w))
```
